```python
import jax, jax.numpy as jnp
from jax import lax
import numpy as np

D_MODEL = 2048
BATCH = 4
SEQ = 2048
DEPTH = 2

HEAD_DIM = 64
Q_BLOCK = 128
ROPE_THETA = 10000.0
EPS = 1e-6

N_GROUPS = 4
GROUP_WIDTH = D_MODEL // N_GROUPS
MIX_WIDTH = N_GROUPS * GROUP_WIDTH

SB_HEADS = GROUP_WIDTH // HEAD_DIM
CONV_CH = GROUP_WIDTH
CONV_WIDTH = 31
CONV_LN_EPS = 1e-5
NSA_HEADS = GROUP_WIDTH // HEAD_DIM
NSA_KV_HEADS = 2
NSA_REP = NSA_HEADS // NSA_KV_HEADS
NSA_KV_COLS = NSA_KV_HEADS * HEAD_DIM
NSA_CMP_LEN = 32
NSA_CMP_STRIDE = 16
NSA_SEL_LEN = 64
NSA_N_SEL = 16
NSA_WINDOW = 512
NSA_FORCE_BONUS = 1e3
MLA_HEADS = 8
MLA_Q_RANK = 384
MLA_KV_RANK = 256
MLA_NOPE_DIM = 64
MLA_ROPE_DIM = 32
MLA_QK_DIM = MLA_NOPE_DIM + MLA_ROPE_DIM
MLA_V_DIM = GROUP_WIDTH // MLA_HEADS
D_FF = 5632
FFN_CONV_WIDTH = 3

SB_COLS = 3 * SB_HEADS * HEAD_DIM
CONV_COLS = 2 * CONV_CH
NSA_Q_COLS = NSA_HEADS * HEAD_DIM
NSA_COLS = NSA_Q_COLS + 6 * NSA_KV_COLS + 3 * NSA_HEADS
MLA_COLS = MLA_Q_RANK + MLA_KV_RANK + MLA_ROPE_DIM
IN_COLS = SB_COLS + CONV_COLS + NSA_COLS + MLA_COLS
OFF_CONV = SB_COLS
OFF_NSA = OFF_CONV + CONV_COLS
OFF_MLA = OFF_NSA + NSA_COLS

kernel_name = 'hybrid_sb_conformer_nsa_mla_block'


def rms_norm(x, g, eps=EPS):
    x32 = x.astype(jnp.float32)
    y = x32 * lax.rsqrt(jnp.mean(x32 * x32, axis=-1, keepdims=True) + eps)
    return (y * g.astype(jnp.float32)).astype(x.dtype)


def rope(x, pos):
    d = x.shape[-1]
    inv = ROPE_THETA ** (-jnp.arange(0, d, 2, dtype=jnp.float32) / d)
    ang = pos.astype(jnp.float32)[:, None] * inv[None, :]
    cos, sin = jnp.cos(ang), jnp.sin(ang)
    x32 = x.astype(jnp.float32)
    x1, x2 = x32[..., : d // 2], x32[..., d // 2:]
    return jnp.concatenate([x1 * cos - x2 * sin, x1 * sin + x2 * cos], -1).astype(x.dtype)


def causal_dwconv(x, w, b):
    k_w, ch = w.shape
    y = lax.conv_general_dilated(x, w[:, None, :].astype(x.dtype), window_strides=(1,),
                                 padding=[(k_w - 1, 0)], dimension_numbers=('NWC', 'WIO', 'NWC'),
                                 feature_group_count=ch)
    return y + b


def masked_softmax(s, mask):
    s = jnp.where(mask, s.astype(jnp.float32), -jnp.inf)
    m = jnp.max(s, axis=-1, keepdims=True)
    m = jnp.where(jnp.isfinite(m), m, 0.0)
    p = jnp.exp(s - m)
    return p / jnp.maximum(jnp.sum(p, axis=-1, keepdims=True), 1e-30)


def stick_breaking_attention(q, k, v):
    B, H, T, d = q.shape
    nq = T // Q_BLOCK
    scale = d ** -0.5
    kpos = jnp.arange(T)
    qb = q.reshape(B, H, nq, Q_BLOCK, d).transpose(2, 0, 1, 3, 4)
    tb = kpos.reshape(nq, Q_BLOCK)

    def block(args):
        qi, ti = args
        z = jnp.einsum('bhqd,bhkd->bhqk', qi, k).astype(jnp.float32) * scale
        mask = kpos[None, :] < ti[:, None]
        log_beta = jax.nn.log_sigmoid(z)
        log_1m = jnp.where(mask, jax.nn.log_sigmoid(-z), 0.0)
        after = lax.cumsum(log_1m, axis=3, reverse=True) - log_1m
        a = jnp.where(mask, jnp.exp(log_beta + after), 0.0)
        return jnp.einsum('bhqk,bhkd->bhqd', a.astype(v.dtype), v)

    o = lax.map(block, (qb, tb))
    return o.transpose(1, 2, 0, 3, 4).reshape(B, H, T, d)


def causal_attention(q, k, v):
    B, H, T, dk = q.shape
    nq = T // Q_BLOCK
    scale = dk ** -0.5
    kpos = jnp.arange(T)
    qb = q.reshape(B, H, nq, Q_BLOCK, dk).transpose(2, 0, 1, 3, 4)
    tb = kpos.reshape(nq, Q_BLOCK)

    def block(args):
        qi, ti = args
        s = jnp.einsum('bhqd,bhkd->bhqk', qi, k) * scale
        p = masked_softmax(s, kpos[None, :] <= ti[:, None])
        return jnp.einsum('bhqk,bhkd->bhqd', p.astype(v.dtype), v)

    o = lax.map(block, (qb, tb))
    return o.transpose(1, 2, 0, 3, 4).reshape(B, H, T, v.shape[-1])


def conformer_conv(u, dw_w, dw_b, ln_g, ln_b, pw_w, pw_b):
    a, g = jnp.split(u, 2, axis=-1)
    h = causal_dwconv(a * jax.nn.sigmoid(g), dw_w, dw_b)
    h32 = h.astype(jnp.float32)
    mu = jnp.mean(h32, axis=-1, keepdims=True)
    var = jnp.mean(jnp.square(h32 - mu), axis=-1, keepdims=True)
    h = ((h32 - mu) * lax.rsqrt(var + CONV_LN_EPS) * ln_g + ln_b).astype(h.dtype)
    return jax.nn.silu(h) @ pw_w + pw_b


def nsa_attention(q, k_cmp, v_cmp, k_slc, v_slc, k_win, v_win, gates, cmp_pe, cmp_w):
    B, G, R, T, d = q.shape
    scale = d ** -0.5
    pos = jnp.arange(T)
    nq = T // Q_BLOCK

    n_cmp = (T - NSA_CMP_LEN) // NSA_CMP_STRIDE + 1
    starts = jnp.arange(n_cmp) * NSA_CMP_STRIDE
    idx = starts[:, None] + jnp.arange(NSA_CMP_LEN)[None, :]

    def compress(t, pe, w):
        blocks = t[:, :, idx] + pe
        return blocks.reshape(B, G, n_cmp, NSA_CMP_LEN * d) @ w

    kc = compress(k_cmp, cmp_pe[0], cmp_w[0])
    vc = compress(v_cmp, cmp_pe[1], cmp_w[1])
    s = jnp.einsum('bgrtd,bgnd->bgrtn', q, kc) * scale
    cmp_mask = (starts + NSA_CMP_LEN - 1)[None, :] <= pos[:, None]
    p_cmp = masked_softmax(s, cmp_mask)
    o_cmp = jnp.einsum('bgrtn,bgnd->bgrtd', p_cmp.astype(vc.dtype), vc)

    n_sblk = T // NSA_SEL_LEN
    n_sel = min(NSA_N_SEL, n_sblk)
    sel_start = jnp.arange(n_sblk) * NSA_SEL_LEN
    overlap = jnp.clip(jnp.minimum(starts[:, None] + NSA_CMP_LEN, sel_start[None, :] + NSA_SEL_LEN)
                       - jnp.maximum(starts[:, None], sel_start[None, :]), 0, None)
    overlap = overlap.astype(jnp.float32) / NSA_CMP_LEN
    imp = jnp.einsum('bgrtn,ns->bgts', p_cmp, overlap)
    cur = pos // NSA_SEL_LEN
    blk = jnp.arange(n_sblk)
    forced = (blk[None, :] == 0) | (blk[None, :] == cur[:, None]) | (blk[None, :] == cur[:, None] - 1)
    eligible = blk[None, :] <= cur[:, None]
    score = jnp.where(eligible, imp + jnp.where(forced, NSA_FORCE_BONUS, 0.0), -jnp.inf)
    _, sel_idx = lax.top_k(score, n_sel)

    kb = k_slc.reshape(B, G, n_sblk, NSA_SEL_LEN, d)
    vb = v_slc.reshape(B, G, n_sblk, NSA_SEL_LEN, d)
    bi = jnp.arange(B)[:, None, None, None]
    gi = jnp.arange(G)[None, :, None, None]
    qb = q.reshape(B, G, R, nq, Q_BLOCK, d).transpose(3, 0, 1, 2, 4, 5)
    ib = sel_idx.reshape(B, G, nq, Q_BLOCK, n_sel).transpose(2, 0, 1, 3, 4)
    tb = pos.reshape(nq, Q_BLOCK)

    def sel_block(args):
        qi, ii, ti = args
        kg = kb[bi, gi, ii].reshape(B, G, Q_BLOCK, n_sel * NSA_SEL_LEN, d)
        vg = vb[bi, gi, ii].reshape(B, G, Q_BLOCK, n_sel * NSA_SEL_LEN, d)
        kpos = (ii[..., None] * NSA_SEL_LEN + jnp.arange(NSA_SEL_LEN)).reshape(B, G, Q_BLOCK, -1)
        s_ = jnp.einsum('bgrqd,bgqkd->bgrqk', qi, kg) * scale
        p = masked_softmax(s_, (kpos <= ti[:, None])[:, :, None])
        return jnp.einsum('bgrqk,bgqkd->bgrqd', p.astype(vg.dtype), vg)

    o_slc = lax.map(sel_block, (qb, ib, tb)).transpose(1, 2, 3, 0, 4, 5).reshape(B, G, R, T, d)

    n_pad = NSA_WINDOW // Q_BLOCK
    n_band = (n_pad + 1) * Q_BLOCK
    bidx = jnp.arange(nq)[:, None] + jnp.arange(n_pad + 1)[None, :]

    def band(t):
        tp = jnp.pad(t, ((0, 0), (0, 0), (NSA_WINDOW, 0), (0, 0))).reshape(B, G, nq + n_pad, Q_BLOCK, d)
        return tp[:, :, bidx].reshape(B, G, nq, n_band, d)

    kw, vw = band(k_win), band(v_win)
    qw = q.reshape(B, G, R, nq, Q_BLOCK, d)
    kpos = jnp.arange(nq)[:, None] * Q_BLOCK - NSA_WINDOW + jnp.arange(n_band)[None, :]
    diff = tb[:, :, None] - kpos[:, None, :]
    wmask = (diff >= 0) & (diff < NSA_WINDOW) & (kpos[:, None, :] >= 0)
    s_w = jnp.einsum('bgrnqd,bgnkd->bgrnqk', qw, kw) * scale
    p_w = masked_softmax(s_w, wmask)
    o_win = jnp.einsum('bgrnqk,bgnkd->bgrnqd', p_w.astype(vw.dtype), vw).reshape(B, G, R, T, d)

    return gates[..., 0:1] * o_cmp + gates[..., 1:2] * o_slc + gates[..., 2:3] * o_win


def mla_attention(u, pos, q_lat_norm, kv_lat_norm, w_uq, w_ukv, q_norm, k_norm):
    B, T, _ = u.shape
    q_lat, kv_lat, k_rope = jnp.split(u, [MLA_Q_RANK, MLA_Q_RANK + MLA_KV_RANK], axis=-1)
    q = (rms_norm(q_lat, q_lat_norm) @ w_uq).reshape(B, T, MLA_HEADS, MLA_QK_DIM)
    kv = (rms_norm(kv_lat, kv_lat_norm) @ w_ukv).reshape(B, T, MLA_HEADS, MLA_NOPE_DIM + MLA_V_DIM)
    k_nope, v = kv[..., :MLA_NOPE_DIM], kv[..., MLA_NOPE_DIM:]
    k = jnp.concatenate([k_nope, jnp.broadcast_to(k_rope[:, :, None, :], (B, T, MLA_HEADS, MLA_ROPE_DIM))], -1)
    q = rms_norm(q, q_norm).transpose(0, 2, 1, 3)
    k = rms_norm(k, k_norm).transpose(0, 2, 1, 3)
    q = jnp.concatenate([q[..., :MLA_NOPE_DIM], rope(q[..., MLA_NOPE_DIM:], pos)], -1)
    k = jnp.concatenate([k[..., :MLA_NOPE_DIM], rope(k[..., MLA_NOPE_DIM:], pos)], -1)
    o = causal_attention(q, k, v.transpose(0, 2, 1, 3))
    return o.transpose(0, 2, 1, 3).reshape(B, T, MLA_HEADS * MLA_V_DIM)


def hybrid_mixer(h, w_in, conv_dw_w, conv_dw_b, conv_ln_g, conv_ln_b, conv_pw_w, conv_pw_b,
                 nsa_q_norm, nsa_k_norm, nsa_cmp_pe, nsa_cmp_w,
                 mla_q_lat_norm, mla_kv_lat_norm, mla_w_uq, mla_w_ukv, mla_q_norm, mla_k_norm,
                 group_norm, w_o):
    B, T, _ = h.shape
    pos = jnp.arange(T)
    u = h @ w_in
    u_sb, u_conv, u_nsa, u_mla = jnp.split(u, [OFF_CONV, OFF_NSA, OFF_MLA], axis=-1)

    def heads(t, n):
        return t.reshape(B, T, n, -1).transpose(0, 2, 1, 3)

    q_sb, k_sb, v_sb = jnp.split(u_sb, 3, axis=-1)
    o_sb = stick_breaking_attention(heads(q_sb, SB_HEADS), heads(k_sb, SB_HEADS), heads(v_sb, SB_HEADS))
    o_sb = o_sb.transpose(0, 2, 1, 3).reshape(B, T, SB_HEADS * HEAD_DIM)

    o_conv = conformer_conv(u_conv, conv_dw_w, conv_dw_b, conv_ln_g, conv_ln_b, conv_pw_w, conv_pw_b)

    q_n = u_nsa[..., :NSA_Q_COLS]
    kv_n = u_nsa[..., NSA_Q_COLS:NSA_Q_COLS + 6 * NSA_KV_COLS]
    g_n = u_nsa[..., NSA_Q_COLS + 6 * NSA_KV_COLS:]
    q_n = rope(rms_norm(q_n.reshape(B, T, NSA_HEADS, HEAD_DIM), nsa_q_norm).transpose(0, 2, 1, 3), pos)
    q_n = q_n.reshape(B, NSA_KV_HEADS, NSA_REP, T, HEAD_DIM)
    kc, vc, ks, vs, kw, vw = [t.reshape(B, T, NSA_KV_HEADS, HEAD_DIM) for t in jnp.split(kv_n, 6, axis=-1)]
    kc, ks, kw = [rope(rms_norm(kk, nsa_k_norm[i]).transpose(0, 2, 1, 3), pos) for i, kk in enumerate((kc, ks, kw))]
    vc, vs, vw = [vv.transpose(0, 2, 1, 3) for vv in (vc, vs, vw)]
    gates = jax.nn.sigmoid(g_n).reshape(B, T, NSA_HEADS, 3).transpose(0, 2, 1, 3)
    gates = gates.reshape(B, NSA_KV_HEADS, NSA_REP, T, 3)
    o_nsa = nsa_attention(q_n, kc, vc, ks, vs, kw, vw, gates, nsa_cmp_pe, nsa_cmp_w)
    o_nsa = o_nsa.reshape(B, NSA_HEADS, T, HEAD_DIM).transpose(0, 2, 1, 3).reshape(B, T, NSA_Q_COLS)

    o_mla = mla_attention(u_mla, pos, mla_q_lat_norm, mla_kv_lat_norm, mla_w_uq, mla_w_ukv, mla_q_norm, mla_k_norm)

    y = jnp.concatenate([o_sb, o_conv, o_nsa, o_mla], axis=-1).reshape(B, T, N_GROUPS, GROUP_WIDTH)
    y = rms_norm(y, group_norm.reshape(N_GROUPS, GROUP_WIDTH)).reshape(B, T, MIX_WIDTH)
    return y @ w_o


def conv_glu_ffn(h, w_up, conv_w, conv_b, w_down):
    u = causal_dwconv(h @ w_up, conv_w, conv_b)
    a, g = jnp.split(u, 2, axis=-1)
    return (jax.nn.silu(g) * a) @ w_down


def setup_inputs(seed: int = 0) -> dict:
    key = jax.random.key(seed)
    ks = list(jax.random.split(key, 40))
    L, D = DEPTH, D_MODEL

    def nrm(shape, scale):
        return jax.random.normal(ks.pop(), shape, jnp.float32) * scale

    def gain(shape):
        return 1.0 + nrm(shape, 0.1)

    return {
        'x': nrm((BATCH, SEQ, D), 1.0),
        'c': nrm((BATCH, D), 1.0),
        'ada_w': nrm((L, D, 6 * D), 0.5 * D ** -0.5),
        'ada_b': nrm((L, 6 * D), 0.02),
        'norm_mix': gain((L, D)),
        'norm_ffn': gain((L, D)),
        'w_in': nrm((L, D, IN_COLS), D ** -0.5),
        'conv_dw_w': nrm((L, CONV_WIDTH, CONV_CH), CONV_WIDTH ** -0.5),
        'conv_dw_b': nrm((L, CONV_CH), 0.02),
        'conv_ln_g': gain((L, CONV_CH)),
        'conv_ln_b': nrm((L, CONV_CH), 0.02),
        'conv_pw_w': nrm((L, CONV_CH, CONV_CH), CONV_CH ** -0.5),
        'conv_pw_b': nrm((L, CONV_CH), 0.02),
        'nsa_q_norm': gain((L, HEAD_DIM)),
        'nsa_k_norm': gain((L, 3, HEAD_DIM)),
        'nsa_cmp_pe': nrm((L, 2, NSA_CMP_LEN, HEAD_DIM), 0.1),
        'nsa_cmp_w': nrm((L, 2, NSA_CMP_LEN * HEAD_DIM, HEAD_DIM), (NSA_CMP_LEN * HEAD_DIM) ** -0.5),
        'mla_q_lat_norm': gain((L, MLA_Q_RANK)),
        'mla_kv_lat_norm': gain((L, MLA_KV_RANK)),
        'mla_w_uq': nrm((L, MLA_Q_RANK, MLA_HEADS * MLA_QK_DIM), MLA_Q_RANK ** -0.5),
        'mla_w_ukv': nrm((L, MLA_KV_RANK, MLA_HEADS * (MLA_NOPE_DIM + MLA_V_DIM)), MLA_KV_RANK ** -0.5),
        'mla_q_norm': gain((L, MLA_QK_DIM)),
        'mla_k_norm': gain((L, MLA_QK_DIM)),
        'group_norm': gain((L, MIX_WIDTH)),
        'w_o': nrm((L, MIX_WIDTH, D), MIX_WIDTH ** -0.5),
        'ffn_up': nrm((L, D, 2 * D_FF), D ** -0.5),
        'ffn_conv_w': nrm((L, FFN_CONV_WIDTH, 2 * D_FF), FFN_CONV_WIDTH ** -0.5),
        'ffn_conv_b': nrm((L, 2 * D_FF), 0.02),
        'ffn_down': nrm((L, D_FF, D), D_FF ** -0.5),
    }


def reference(x, c, ada_w, ada_b, norm_mix, norm_ffn, w_in, conv_dw_w, conv_dw_b, conv_ln_g, conv_ln_b,
              conv_pw_w, conv_pw_b, nsa_q_norm, nsa_k_norm, nsa_cmp_pe, nsa_cmp_w,
              mla_q_lat_norm, mla_kv_lat_norm, mla_w_uq, mla_w_ukv, mla_q_norm, mla_k_norm,
              group_norm, w_o, ffn_up, ffn_conv_w, ffn_conv_b, ffn_down):
    cond = jax.nn.silu(c)
    for l in range(DEPTH):
        mod = cond @ ada_w[l] + ada_b[l]
        sh1, sc1, g1, sh2, sc2, g2 = [m[:, None, :] for m in jnp.split(mod, 6, axis=-1)]
        h = rms_norm(x, norm_mix[l]) * (1.0 + sc1) + sh1
        x = x + g1 * hybrid_mixer(h, w_in[l], conv_dw_w[l], conv_dw_b[l], conv_ln_g[l], conv_ln_b[l],
                                  conv_pw_w[l], conv_pw_b[l], nsa_q_norm[l], nsa_k_norm[l],
                                  nsa_cmp_pe[l], nsa_cmp_w[l], mla_q_lat_norm[l], mla_kv_lat_norm[l],
                                  mla_w_uq[l], mla_w_ukv[l], mla_q_norm[l], mla_k_norm[l],
                                  group_norm[l], w_o[l])
        h = rms_norm(x, norm_ffn[l]) * (1.0 + sc2) + sh2
        x = x + g2 * conv_glu_ffn(h, ffn_up[l], ffn_conv_w[l], ffn_conv_b[l], ffn_down[l])
    return x
```

```python
import functools

import numpy as np
import jax
import jax.numpy as jnp
from jax import lax
from jax.experimental import pallas as pl
from jax.experimental.pallas import tpu as pltpu

F32 = jnp.float32
BF16 = jnp.bfloat16

LANES = 128
VMEM_LIMIT = 56 * 1024 * 1024

HEAD_DIM = 64
HEAD_SHIFT = 6
ROPE_THETA = 10000.0
EPS = 1e-6
GROUP_WIDTH = 512
CONV_WIDTH = 31
CONV_LN_EPS = 1e-5
NSA_CMP_LEN = 32
NSA_CMP_STRIDE = 16
NSA_SEL_LEN = 64
SEL_SHIFT = 6
NSA_N_SEL = 16
NSA_WINDOW = 512
NSA_FORCE_BONUS = 1e3
MLA_Q_RANK = 384
MLA_KV_RANK = 256
MLA_NOPE = 64
MLA_ROPE = 32
MLA_QK = MLA_NOPE + MLA_ROPE
MLA_HEADS = 8

U_SBQ, U_SBK, U_SBV = 0, 512, 1024
U_NKV = 1536
U_MLA = 2304
U_NQ = 3072
U_CA, U_CG = 3584, 4096
U_COLS = 4608

TQ = 256
TK = 256
NEG = -1e30


def _cparams(*sem):
    return pltpu.CompilerParams(dimension_semantics=sem, vmem_limit_bytes=VMEM_LIMIT)


def _dot(a, b):
    return jnp.dot(a, b, preferred_element_type=F32)


def _dot_nt(a, b):
    return lax.dot_general(a, b, (((1,), (1,)), ((), ())), preferred_element_type=F32)


def _split_dot(x, w):
    hi = x.astype(BF16)
    lo = (x - hi.astype(F32)).astype(BF16)
    return _dot(hi, w) + _dot(lo, w)


def _lane_iota(n=LANES):
    return lax.broadcasted_iota(jnp.int32, (1, n), 1)


def _ada_kernel(c_ref, w_ref, b_ref, o_ref):
    c = c_ref[...]
    cond = (c * jax.nn.sigmoid(c)).astype(BF16)
    o_ref[0] = _dot(cond, w_ref[0].astype(BF16)) + b_ref[0]


def _ada(c, ada_w, ada_b):
    n_layers, d, n = ada_w.shape
    b = c.shape[0]
    rows = 8
    tn = 1024
    cpad = jnp.pad(c, ((0, rows - b), (0, 0)))
    out = pl.pallas_call(
        _ada_kernel,
        grid=(n_layers, n // tn),
        in_specs=[
            pl.BlockSpec((rows, d), lambda l, j: (0, 0)),
            pl.BlockSpec((1, d, tn), lambda l, j: (l, 0, j)),
            pl.BlockSpec((1, 1, tn), lambda l, j: (l, 0, j)),
        ],
        out_specs=pl.BlockSpec((1, rows, tn), lambda l, j: (l, 0, j)),
        out_shape=jax.ShapeDtypeStruct((n_layers, rows, n), F32),
        compiler_params=_cparams("parallel", "parallel"),
        name="ada_mod",
    )(cpad, ada_w, ada_b.reshape(n_layers, 1, n))
    return out[:, :b].reshape(n_layers, b, 6, d)


def _modulated_norm(x, g, shift, scale):
    ms = jnp.mean(x * x, axis=-1, keepdims=True)
    return x * lax.rsqrt(ms + EPS) * g * (1.0 + scale) + shift


def _in_kernel(x_ref, mod_ref, g_ref, w_ref, o_ref, h_ref):
    @pl.when(pl.program_id(1) == 0)
    def _():
        m = mod_ref[0]
        h_ref[...] = _modulated_norm(x_ref[...], g_ref[...], m[0:1], m[1:2]).astype(BF16)

    o_ref[...] = _dot(h_ref[...], w_ref[...])


def _in_proj(x2, mod, g, w, seq):
    m_rows, d = x2.shape
    n = w.shape[1]
    tm, tn = 512, 512
    per_seq = seq // tm
    return pl.pallas_call(
        _in_kernel,
        grid=(m_rows // tm, n // tn),
        in_specs=[
            pl.BlockSpec((tm, d), lambda i, j: (i, 0)),
            pl.BlockSpec((1, 6, d), lambda i, j: (i // per_seq, 0, 0)),
            pl.BlockSpec((1, d), lambda i, j: (0, 0)),
            pl.BlockSpec((d, tn), lambda i, j: (0, j)),
        ],
        out_specs=pl.BlockSpec((tm, tn), lambda i, j: (i, j)),
        out_shape=jax.ShapeDtypeStruct((m_rows, n), F32),
        scratch_shapes=[pltpu.VMEM((tm, d), BF16)],
        compiler_params=_cparams("parallel", "arbitrary"),
        name="in_proj",
    )(x2, mod, g, w)


def _sb_kernel(q_ref, k_ref, v_ref, o_ref, carry_ref, acc_ref, *, nkb):
    i = pl.program_id(2)
    lane = _lane_iota()
    row = i * TQ + lax.broadcasted_iota(jnp.int32, (TQ, 1), 0)
    q2 = q_ref[0] * (HEAD_DIM ** -0.5)
    qs = (jnp.where(lane < HEAD_DIM, q2, 0.0).astype(BF16),
          jnp.where(lane >= HEAD_DIM, q2, 0.0).astype(BF16))
    tri = (lax.broadcasted_iota(jnp.int32, (TK, TK), 0)
           > lax.broadcasted_iota(jnp.int32, (TK, TK), 1)).astype(BF16)
    carry_ref[...] = jnp.zeros_like(carry_ref)
    acc_ref[...] = jnp.zeros_like(acc_ref)

    for c in reversed(range(nkb)):
        @pl.when(c <= i)
        def _():
            kt = k_ref[0, c * TK:(c + 1) * TK, :].astype(BF16)
            vt = v_ref[0, c * TK:(c + 1) * TK, :].astype(BF16)
            kpos = c * TK + _lane_iota(TK)
            mask = kpos < row
            for hd in range(2):
                z = _dot_nt(qs[hd], kt)
                sp = jnp.log1p(jnp.exp(-jnp.abs(z)))
                log_beta = jnp.minimum(z, 0.0) - sp
                log_1m = jnp.where(mask, -jnp.maximum(z, 0.0) - sp, 0.0)
                after = _split_dot(log_1m, tri) + carry_ref[hd]
                a = jnp.where(mask, jnp.exp(log_beta + after), 0.0)
                acc_ref[hd] += _dot(a.astype(BF16), vt)
                carry_ref[hd] += jnp.sum(log_1m, axis=-1, keepdims=True)

    o_ref[0] = jnp.where(lane < HEAD_DIM, acc_ref[0], acc_ref[1])


def _sb_attention(u3):
    b, t, _ = u3.shape
    nq = t // TQ
    pairs = GROUP_WIDTH // LANES
    return pl.pallas_call(
        functools.partial(_sb_kernel, nkb=t // TK),
        grid=(b, pairs, nq),
        in_specs=[
            pl.BlockSpec((1, TQ, LANES), lambda bb, p, i: (bb, i, U_SBQ // LANES + p)),
            pl.BlockSpec((1, t, LANES), lambda bb, p, i: (bb, 0, U_SBK // LANES + p)),
            pl.BlockSpec((1, t, LANES), lambda bb, p, i: (bb, 0, U_SBV // LANES + p)),
        ],
        out_specs=pl.BlockSpec((1, TQ, LANES), lambda bb, p, i: (bb, i, p)),
        out_shape=jax.ShapeDtypeStruct((b, t, GROUP_WIDTH), F32),
        scratch_shapes=[pltpu.VMEM((2, TQ, 1), F32), pltpu.VMEM((2, TQ, LANES), F32)],
        compiler_params=_cparams("parallel", "parallel", "arbitrary"),
        name="sb_attention",
    )(u3, u3, u3)


CONV_HALO = 32


def _conv_kernel(a_ref, g_ref, ah_ref, gh_ref, dww_ref, dwb_ref, lng_ref, lnb_ref, pw_ref, pwb_ref,
                 o_ref, h_ref, *, tm):
    i = pl.program_id(1)
    h_ref[CONV_HALO:, :] = a_ref[0] * jax.nn.sigmoid(g_ref[0])
    halo = ah_ref[0] * jax.nn.sigmoid(gh_ref[0])
    h_ref[0:CONV_HALO, :] = jnp.where(i == 0, 0.0, halo)
    acc = jnp.zeros((tm, GROUP_WIDTH), F32) + dwb_ref[...]
    base = CONV_HALO - (CONV_WIDTH - 1)
    for k in range(CONV_WIDTH):
        acc = acc + dww_ref[k:k + 1, :] * h_ref[base + k:base + k + tm, :]
    mu = jnp.mean(acc, axis=-1, keepdims=True)
    cen = acc - mu
    var = jnp.mean(cen * cen, axis=-1, keepdims=True)
    hn = cen * lax.rsqrt(var + CONV_LN_EPS) * lng_ref[...] + lnb_ref[...]
    act = hn * jax.nn.sigmoid(hn)
    o_ref[0] = _dot(act.astype(BF16), pw_ref[...]) + pwb_ref[...]


def _conformer(u3, dw_w, dw_b, ln_g, ln_b, pw_w, pw_b):
    b, t, _ = u3.shape
    tm = 512
    cw = GROUP_WIDTH
    hb = tm // CONV_HALO

    def halo_map(col):
        return lambda bb, i: (bb, jnp.maximum(i * hb - 1, 0), col)

    row = lambda bb, i: (0, 0)
    return pl.pallas_call(
        functools.partial(_conv_kernel, tm=tm),
        grid=(b, t // tm),
        in_specs=[
            pl.BlockSpec((1, tm, cw), lambda bb, i: (bb, i, U_CA // cw)),
            pl.BlockSpec((1, tm, cw), lambda bb, i: (bb, i, U_CG // cw)),
            pl.BlockSpec((1, CONV_HALO, cw), halo_map(U_CA // cw)),
            pl.BlockSpec((1, CONV_HALO, cw), halo_map(U_CG // cw)),
            pl.BlockSpec((CONV_WIDTH, cw), row),
            pl.BlockSpec((1, cw), row),
            pl.BlockSpec((1, cw), row),
            pl.BlockSpec((1, cw), row),
            pl.BlockSpec((cw, cw), row),
            pl.BlockSpec((1, cw), row),
        ],
        out_specs=pl.BlockSpec((1, tm, cw), lambda bb, i: (bb, i, 0)),
        out_shape=jax.ShapeDtypeStruct((b, t, cw), F32),
        scratch_shapes=[pltpu.VMEM((tm + CONV_HALO, cw), F32)],
        compiler_params=_cparams("parallel", "parallel"),
        name="conformer_conv",
    )(u3, u3, u3, u3, dw_w, dw_b.reshape(1, cw), ln_g.reshape(1, cw), ln_b.reshape(1, cw),
      pw_w.astype(BF16), pw_b.reshape(1, cw))


def _head_norm_rope(x, gain, cos, sin_signed, seg_mean):
    ms = _split_dot(x * x, seg_mean)
    y = x * lax.rsqrt(ms + EPS) * gain
    lane = _lane_iota()
    half = HEAD_DIM // 2
    first = (lane & (HEAD_DIM - 1)) < half
    partner = jnp.where(first, pltpu.roll(y, LANES - half, 1), pltpu.roll(y, half, 1))
    return y * cos + partner * sin_signed


def _dup(x, g):
    lane = _lane_iota()
    sw = pltpu.roll(x, HEAD_DIM, 1)
    if g == 0:
        return jnp.where(lane < HEAD_DIM, x, sw)
    return jnp.where(lane < HEAD_DIM, sw, x)


def _nsa_prep_kernel(q_ref, kv_ref, gk_ref, cos_ref, sin_ref, qg_ref, kg_ref,
                     qn_ref, kc_ref, ks_ref, vs_ref, kw_ref, vw_ref, gt_ref):
    cos = cos_ref[...]
    sin = sin_ref[...]
    seg = lax.broadcasted_iota(jnp.int32, (LANES, LANES), 0) >> HEAD_SHIFT
    seg_mean = jnp.where(seg == lax.broadcasted_iota(jnp.int32, (LANES, LANES), 1) >> HEAD_SHIFT,
                         1.0 / HEAD_DIM, 0.0).astype(BF16)
    scale = HEAD_DIM ** -0.5
    for p in range(GROUP_WIDTH // LANES):
        x = q_ref[:, p * LANES:(p + 1) * LANES]
        qn_ref[:, p * LANES:(p + 1) * LANES] = (
            _head_norm_rope(x, qg_ref[...], cos, sin, seg_mean) * scale).astype(BF16)

    def blk(n):
        return kv_ref[:, n * LANES:(n + 1) * LANES]

    kc_ref[...] = _head_norm_rope(blk(0), kg_ref[0:1, :], cos, sin, seg_mean)
    ks = _head_norm_rope(blk(2), kg_ref[1:2, :], cos, sin, seg_mean)
    kw = _head_norm_rope(blk(4), kg_ref[2:3, :], cos, sin, seg_mean)
    vs = blk(3)
    vw = blk(5)
    for g in range(2):
        sl = slice(g * LANES, (g + 1) * LANES)
        ks_ref[:, sl] = _dup(ks, g).astype(BF16)
        vs_ref[:, sl] = _dup(vs, g).astype(BF16)
        kw_ref[:, sl] = _dup(kw, g).astype(BF16)
        vw_ref[:, sl] = _dup(vw, g).astype(BF16)
    gates = jax.nn.sigmoid(gk_ref[...])
    gt_ref[:, 0:LANES] = gates
    gt_ref[:, LANES:2 * LANES] = pltpu.roll(gates, LANES - 12, 1)


def _nsa_prep(u2, cos, sin, q_gain, k_gain, seq):
    m_rows = u2.shape[0]
    tm = 512
    per_seq = seq // tm
    rowmap = lambda i: (i, 0)
    tab = lambda i: (i % per_seq, 0)
    const = lambda i: (0, 0)
    outs = [
        jax.ShapeDtypeStruct((m_rows, GROUP_WIDTH), BF16),
        jax.ShapeDtypeStruct((m_rows, LANES), F32),
        jax.ShapeDtypeStruct((m_rows, 2 * LANES), BF16),
        jax.ShapeDtypeStruct((m_rows, 2 * LANES), BF16),
        jax.ShapeDtypeStruct((m_rows, 2 * LANES), BF16),
        jax.ShapeDtypeStruct((m_rows, 2 * LANES), BF16),
        jax.ShapeDtypeStruct((m_rows, 2 * LANES), F32),
    ]
    return pl.pallas_call(
        _nsa_prep_kernel,
        grid=(m_rows // tm,),
        in_specs=[
            pl.BlockSpec((tm, GROUP_WIDTH), lambda i: (i, U_NQ // GROUP_WIDTH)),
            pl.BlockSpec((tm, 6 * LANES), lambda i: (i, U_NKV // (6 * LANES))),
            pl.BlockSpec((tm, LANES), lambda i: (i, (U_MLA + MLA_Q_RANK + MLA_KV_RANK) // LANES)),
            pl.BlockSpec((tm, LANES), tab),
            pl.BlockSpec((tm, LANES), tab),
            pl.BlockSpec((1, LANES), const),
            pl.BlockSpec((3, LANES), const),
        ],
        out_specs=[
            pl.BlockSpec((tm, GROUP_WIDTH), rowmap),
            pl.BlockSpec((tm, LANES), rowmap),
            pl.BlockSpec((tm, 2 * LANES), rowmap),
            pl.BlockSpec((tm, 2 * LANES), rowmap),
            pl.BlockSpec((tm, 2 * LANES), rowmap),
            pl.BlockSpec((tm, 2 * LANES), rowmap),
            pl.BlockSpec((tm, 2 * LANES), rowmap),
        ],
        out_shape=outs,
        compiler_params=_cparams("parallel"),
        name="nsa_prep",
    )(u2, u2, u2, cos, sin, q_gain, k_gain)


def _compress_kernel(xk_ref, xv_ref, pe_ref, wk_ref, wv_ref, kc_ref, vc_ref):
    nrow = xk_ref.shape[2]

    def comp(x, pe_a, pe_b, w_ref):
        a = _dot((x + pe_a).astype(BF16), w_ref[0])
        bm = _dot((x + pe_b).astype(BF16), w_ref[1])
        return a + pltpu.roll(bm, nrow - 1, 0)

    kc_ref[0, 0] = comp(xk_ref[0, 0], pe_ref[0:1, :], pe_ref[1:2, :], wk_ref).astype(BF16)
    vc_ref[0, 0] = comp(xv_ref[0, 0], pe_ref[2:3, :], pe_ref[3:4, :], wv_ref).astype(BF16)


def _compress(xk, xv, pe, wk, wv):
    b, g, nrow, wide = xk.shape
    xmap = lambda bb, gg: (bb, gg, 0, 0)
    out = jax.ShapeDtypeStruct((b, g, nrow, LANES), BF16)
    return pl.pallas_call(
        _compress_kernel,
        grid=(b, g),
        in_specs=[
            pl.BlockSpec((1, 1, nrow, wide), xmap),
            pl.BlockSpec((1, 1, nrow, wide), xmap),
            pl.BlockSpec((4, wide), lambda bb, gg: (0, 0)),
            pl.BlockSpec((2, wide, LANES), lambda bb, gg: (0, 0, 0)),
            pl.BlockSpec((2, wide, LANES), lambda bb, gg: (0, 0, 0)),
        ],
        out_specs=[pl.BlockSpec((1, 1, nrow, LANES), xmap), pl.BlockSpec((1, 1, nrow, LANES), xmap)],
        out_shape=[out, out],
        compiler_params=_cparams("parallel", "parallel"),
        name="nsa_compress",
    )(xk, xv, pe, wk, wv)


def _softmax_tile(q, k, v, mask, m_ref, l_ref, acc_ref, scale=None):
    s = _dot_nt(q, k)
    if scale is not None:
        s = s * scale
    s = jnp.where(mask, s, NEG)
    m_prev = m_ref[...]
    m_new = jnp.maximum(m_prev, jnp.max(s, axis=-1, keepdims=True))
    alpha = jnp.exp(m_prev - m_new)
    p = jnp.where(mask, jnp.exp(s - m_new), 0.0)
    l_ref[...] = alpha * l_ref[...] + jnp.sum(p, axis=-1, keepdims=True)
    acc_ref[...] = alpha * acc_ref[...] + _dot(p.astype(BF16), v)
    m_ref[...] = m_new


def _nsa_kernel(q_ref, kc_ref, vc_ref, ks_ref, vs_ref, kw_ref, vw_ref, gt_ref, ov_ref,
                o_ref, m_ref, l_ref, acc_ref, *, nkb, n_cmp, n_sblk, n_sel):
    i = pl.program_id(2)
    lane = _lane_iota()
    row = i * TQ + lax.broadcasted_iota(jnp.int32, (TQ, 1), 0)
    rep = 4
    qs = []
    for p in range(2):
        q2 = q_ref[0, :, p * LANES:(p + 1) * LANES]
        qs.append(jnp.where(lane < HEAD_DIM, q2, jnp.zeros_like(q2)))
        qs.append(jnp.where(lane >= HEAD_DIM, q2, jnp.zeros_like(q2)))
    m_ref[...] = jnp.full_like(m_ref, NEG)
    l_ref[...] = jnp.zeros_like(l_ref)
    acc_ref[...] = jnp.zeros_like(acc_ref)

    kc = kc_ref[0, 0]
    vc = vc_ref[0, 0]
    cmask = ((lane * NSA_CMP_STRIDE + (NSA_CMP_LEN - 1)) <= row) & (lane < n_cmp)
    psum = jnp.zeros((TQ, LANES), F32)
    o_cmp = []
    for r in range(rep):
        s = jnp.where(cmask, _dot_nt(qs[r], kc), NEG)
        mx = jnp.max(s, axis=-1, keepdims=True)
        p = jnp.where(cmask, jnp.exp(s - mx), 0.0)
        p = p / jnp.maximum(jnp.sum(p, axis=-1, keepdims=True), 1e-30)
        o_cmp.append(_dot(p.astype(BF16), vc))
        psum = psum + p
    imp = _split_dot(psum, ov_ref[...])
    cur = row >> SEL_SHIFT
    eligible = lane <= cur
    forced = (lane == 0) | (lane == cur) | (lane == cur - 1)
    score = jnp.where(eligible, imp + jnp.where(forced, NSA_FORCE_BONUS, 0.0), NEG)
    rank = jnp.zeros((TQ, LANES), F32)
    for sp in range(n_sblk):
        col = score[:, sp:sp + 1]
        beats = (col > score) | ((col == score) & (lane > sp))
        rank = rank + jnp.where(beats, 1.0, 0.0)
    sel = jnp.where(eligible & (rank < n_sel), 1.0, 0.0).astype(BF16)

    for c in range(nkb):
        @pl.when(c <= i)
        def _():
            kpos = c * TK + _lane_iota(TK)
            expand = (lax.broadcasted_iota(jnp.int32, (LANES, TK), 0)
                      == (c * TK + lax.broadcasted_iota(jnp.int32, (LANES, TK), 1)) >> SEL_SHIFT)
            chosen = _dot(sel, expand.astype(BF16)) > 0.5
            mask = chosen & (kpos <= row)
            kt = ks_ref[0, c * TK:(c + 1) * TK, :]
            vt = vs_ref[0, c * TK:(c + 1) * TK, :]
            for r in range(rep):
                _softmax_tile(qs[r], kt, vt, mask, m_ref.at[r], l_ref.at[r], acc_ref.at[r])

    wblocks = NSA_WINDOW // TK
    for c in range(nkb):
        @pl.when((c <= i) & (c >= i - wblocks))
        def _():
            diff = row - (c * TK + _lane_iota(TK))
            mask = (diff >= 0) & (diff < NSA_WINDOW)
            kt = kw_ref[0, c * TK:(c + 1) * TK, :]
            vt = vw_ref[0, c * TK:(c + 1) * TK, :]
            for r in range(rep):
                _softmax_tile(qs[r], kt, vt, mask, m_ref.at[rep + r], l_ref.at[rep + r], acc_ref.at[rep + r])

    gt = gt_ref[0]
    outs = []
    for r in range(rep):
        o_slc = acc_ref[r] / jnp.maximum(l_ref[r], 1e-30)
        o_win = acc_ref[rep + r] / jnp.maximum(l_ref[rep + r], 1e-30)
        outs.append(gt[:, 3 * r:3 * r + 1] * o_cmp[r] + gt[:, 3 * r + 1:3 * r + 2] * o_slc
                    + gt[:, 3 * r + 2:3 * r + 3] * o_win)
    o_ref[0, :, 0:LANES] = jnp.where(lane < HEAD_DIM, outs[0], outs[1])
    o_ref[0, :, LANES:2 * LANES] = jnp.where(lane < HEAD_DIM, outs[2], outs[3])


def _nsa_attention(qn, kcmp, vcmp, ksd, vsd, kwd, vwd, gates, overlap):
    b, t, _ = qn.shape
    nq = t // TQ
    n_cmp = (t - NSA_CMP_LEN) // NSA_CMP_STRIDE + 1
    n_sblk = t // NSA_SEL_LEN
    nrow = kcmp.shape[2]
    kv = lambda bb, g, i: (bb, 0, g)
    cm = lambda bb, g, i: (bb, g, 0, 0)
    return pl.pallas_call(
        functools.partial(_nsa_kernel, nkb=t // TK, n_cmp=n_cmp, n_sblk=n_sblk,
                          n_sel=min(NSA_N_SEL, n_sblk)),
        grid=(b, 2, nq),
        in_specs=[
            pl.BlockSpec((1, TQ, 2 * LANES), lambda bb, g, i: (bb, i, g)),
            pl.BlockSpec((1, 1, nrow, LANES), cm),
            pl.BlockSpec((1, 1, nrow, LANES), cm),
            pl.BlockSpec((1, t, LANES), kv),
            pl.BlockSpec((1, t, LANES), kv),
            pl.BlockSpec((1, t, LANES), kv),
            pl.BlockSpec((1, t, LANES), kv),
            pl.BlockSpec((1, TQ, LANES), lambda bb, g, i: (bb, i, g)),
            pl.BlockSpec((LANES, LANES), lambda bb, g, i: (0, 0)),
        ],
        out_specs=pl.BlockSpec((1, TQ, 2 * LANES), lambda bb, g, i: (bb, i, g)),
        out_shape=jax.ShapeDtypeStruct((b, t, GROUP_WIDTH), F32),
        scratch_shapes=[pltpu.VMEM((8, TQ, 1), F32), pltpu.VMEM((8, TQ, 1), F32),
                        pltpu.VMEM((8, TQ, LANES), F32)],
        compiler_params=_cparams("parallel", "parallel", "arbitrary"),
        name="nsa_attention",
    )(qn, kcmp, vcmp, ksd, vsd, kwd, vwd, gates, overlap)


def _mla_prep_kernel(u_ref, cos_ref, sin_ref, qlg_ref, kvlg_ref, wq_ref, wk_ref, wv_ref, qg_ref, kg_ref,
                     q_ref, k_ref, v_ref):
    lane = _lane_iota()
    cos = cos_ref[...]
    sin = sin_ref[...]
    half = MLA_ROPE // 2

    def lat_norm(x, g):
        ms = jnp.mean(x * x, axis=-1, keepdims=True)
        return (x * lax.rsqrt(ms + EPS) * g).astype(BF16)

    def head_norm_rope(x, g):
        ms = jnp.sum(x * x, axis=-1, keepdims=True) * (1.0 / MLA_QK)
        y = x * lax.rsqrt(ms + EPS) * g
        partner = jnp.where(lane < MLA_NOPE + half, pltpu.roll(y, LANES - half, 1), pltpu.roll(y, half, 1))
        return y * cos + partner * sin

    q = _dot(lat_norm(u_ref[:, 0:MLA_Q_RANK], qlg_ref[...]), wq_ref[...])
    kvn = lat_norm(u_ref[:, MLA_Q_RANK:MLA_Q_RANK + MLA_KV_RANK], kvlg_ref[...])
    kn = _dot(kvn, wk_ref[...])
    v_ref[...] = _dot(kvn, wv_ref[...]).astype(BF16)
    gk = u_ref[:, MLA_Q_RANK + MLA_KV_RANK:MLA_Q_RANK + MLA_KV_RANK + LANES]
    k_rope = jnp.where((lane >= MLA_NOPE) & (lane < MLA_QK), gk, 0.0)
    for h in range(MLA_HEADS):
        sl = slice(h * LANES, (h + 1) * LANES)
        q_ref[:, sl] = head_norm_rope(q[:, sl], qg_ref[...]).astype(BF16)
        k_ref[:, sl] = head_norm_rope(kn[:, sl] + k_rope, kg_ref[...]).astype(BF16)


def _mla_prep(u2, cos, sin, q_lat_gain, kv_lat_gain, wq, wk, wv, q_gain, k_gain, seq):
    m_rows = u2.shape[0]
    tm = 512
    per_seq = seq // tm
    width = MLA_Q_RANK + MLA_KV_RANK + LANES
    rowmap = lambda i: (i, 0)
    tab = lambda i: (i % per_seq, 0)
    const = lambda i: (0, 0)
    return pl.pallas_call(
        _mla_prep_kernel,
        grid=(m_rows // tm,),
        in_specs=[
            pl.BlockSpec((tm, width), lambda i: (i, U_MLA // width)),
            pl.BlockSpec((tm, LANES), tab),
            pl.BlockSpec((tm, LANES), tab),
            pl.BlockSpec((1, MLA_Q_RANK), const),
            pl.BlockSpec((1, MLA_KV_RANK), const),
            pl.BlockSpec(wq.shape, const),
            pl.BlockSpec(wk.shape, const),
            pl.BlockSpec(wv.shape, const),
            pl.BlockSpec((1, LANES), const),
            pl.BlockSpec((1, LANES), const),
        ],
        out_specs=[
            pl.BlockSpec((tm, MLA_HEADS * LANES), rowmap),
            pl.BlockSpec((tm, MLA_HEADS * LANES), rowmap),
            pl.BlockSpec((tm, GROUP_WIDTH), rowmap),
        ],
        out_shape=[
            jax.ShapeDtypeStruct((m_rows, MLA_HEADS * LANES), BF16),
            jax.ShapeDtypeStruct((m_rows, MLA_HEADS * LANES), BF16),
            jax.ShapeDtypeStruct((m_rows, GROUP_WIDTH), BF16),
        ],
        compiler_params=_cparams("parallel"),
        name="mla_prep",
    )(u2, cos, sin, q_lat_gain, kv_lat_gain, wq, wk, wv, q_gain, k_gain)


def _mla_kernel(q_ref, k_ref, v_ref, o_ref, m_ref, l_ref, acc_ref, *, nkb):
    i = pl.program_id(2)
    lane = _lane_iota()
    row = i * TQ + lax.broadcasted_iota(jnp.int32, (TQ, 1), 0)
    m_ref[...] = jnp.full_like(m_ref, NEG)
    l_ref[...] = jnp.zeros_like(l_ref)
    acc_ref[...] = jnp.zeros_like(acc_ref)
    scale = MLA_QK ** -0.5
    for c in range(nkb):
        @pl.when(c <= i)
        def _():
            mask = (c * TK + _lane_iota(TK)) <= row
            vt = v_ref[0, c * TK:(c + 1) * TK, :]
            for hd in range(2):
                sl = slice(hd * LANES, (hd + 1) * LANES)
                _softmax_tile(q_ref[0, :, sl], k_ref[0, c * TK:(c + 1) * TK, sl], vt, mask,
                              m_ref.at[hd], l_ref.at[hd], acc_ref.at[hd], scale=scale)
    o0 = acc_ref[0] / jnp.maximum(l_ref[0], 1e-30)
    o1 = acc_ref[1] / jnp.maximum(l_ref[1], 1e-30)
    o_ref[0] = jnp.where(lane < HEAD_DIM, o0, o1)


def _mla_attention(qm, km, vm):
    b, t, _ = qm.shape
    nq = t // TQ
    return pl.pallas_call(
        functools.partial(_mla_kernel, nkb=t // TK),
        grid=(b, MLA_HEADS // 2, nq),
        in_specs=[
            pl.BlockSpec((1, TQ, 2 * LANES), lambda bb, p, i: (bb, i, p)),
            pl.BlockSpec((1, t, 2 * LANES), lambda bb, p, i: (bb, 0, p)),
            pl.BlockSpec((1, t, LANES), lambda bb, p, i: (bb, 0, p)),
        ],
        out_specs=pl.BlockSpec((1, TQ, LANES), lambda bb, p, i: (bb, i, p)),
        out_shape=jax.ShapeDtypeStruct((b, t, GROUP_WIDTH), F32),
        scratch_shapes=[pltpu.VMEM((2, TQ, 1), F32), pltpu.VMEM((2, TQ, 1), F32),
                        pltpu.VMEM((2, TQ, LANES), F32)],
        compiler_params=_cparams("parallel", "parallel", "arbitrary"),
        name="mla_attention",
    )(qm, km, vm)


def _out_kernel(a_ref, b_ref, c_ref, d_ref, gn_ref, w_ref, x_ref, mod_ref, o_ref, h_ref):
    @pl.when(pl.program_id(1) == 0)
    def _():
        for gi, r in enumerate((a_ref, b_ref, c_ref, d_ref)):
            y = r[...]
            ms = jnp.mean(y * y, axis=-1, keepdims=True)
            sl = slice(gi * GROUP_WIDTH, (gi + 1) * GROUP_WIDTH)
            h_ref[:, sl] = (y * lax.rsqrt(ms + EPS) * gn_ref[:, sl]).astype(BF16)

    o_ref[...] = x_ref[...] + mod_ref[0, 2:3, :] * _dot(h_ref[...], w_ref[...])


def _out_proj(parts, gn, w, x2, mod, seq):
    m_rows, d = x2.shape
    tm, tn = 512, 512
    per_seq = seq // tm
    part = pl.BlockSpec((tm, GROUP_WIDTH), lambda i, j: (i, 0))
    return pl.pallas_call(
        _out_kernel,
        grid=(m_rows // tm, d // tn),
        in_specs=[
            part, part, part, part,
            pl.BlockSpec((1, 4 * GROUP_WIDTH), lambda i, j: (0, 0)),
            pl.BlockSpec((4 * GROUP_WIDTH, tn), lambda i, j: (0, j)),
            pl.BlockSpec((tm, tn), lambda i, j: (i, j)),
            pl.BlockSpec((1, 6, tn), lambda i, j: (i // per_seq, 0, j)),
        ],
        out_specs=pl.BlockSpec((tm, tn), lambda i, j: (i, j)),
        out_shape=jax.ShapeDtypeStruct((m_rows, d), F32),
        scratch_shapes=[pltpu.VMEM((tm, 4 * GROUP_WIDTH), BF16)],
        compiler_params=_cparams("parallel", "arbitrary"),
        name="out_proj",
    )(*parts, gn, w, x2, mod)


FFN_HALO = 16


def _ffn_kernel(x_ref, xh_ref, mod_ref, g_ref, wa_ref, wg_ref, cwa_ref, cwg_ref, cba_ref, cbg_ref, wd_ref,
                o_ref, h_ref, acc_ref, *, per_seq):
    i = pl.program_id(0)
    j = pl.program_id(1)
    m = mod_ref[0]

    @pl.when(j == 0)
    def _():
        h_ref[FFN_HALO:, :] = _modulated_norm(x_ref[...], g_ref[...], m[3:4], m[4:5]).astype(BF16)
        halo = _modulated_norm(xh_ref[...], g_ref[...], m[3:4], m[4:5])
        h_ref[0:FFN_HALO, :] = jnp.where(i % per_seq == 0, 0.0, halo).astype(BF16)
        acc_ref[...] = jnp.zeros_like(acc_ref)

    h = h_ref[...]

    def conv(u, cw_ref, cb_ref):
        y = cw_ref[2:3, :] * u + cw_ref[1:2, :] * pltpu.roll(u, 1, 0) + cw_ref[0:1, :] * pltpu.roll(u, 2, 0)
        return y[FFN_HALO:, :] + cb_ref[...]

    ya = conv(_dot(h, wa_ref[...]), cwa_ref, cba_ref)
    yg = conv(_dot(h, wg_ref[...]), cwg_ref, cbg_ref)
    act = (yg * jax.nn.sigmoid(yg)) * ya
    acc_ref[...] += _dot(act.astype(BF16), wd_ref[...])

    @pl.when(j == pl.num_programs(1) - 1)
    def _():
        o_ref[...] = x_ref[...] + m[5:6] * acc_ref[...]


def _ffn(x2, mod, g, w_up, conv_w, conv_b, w_down, seq):
    m_rows, d = x2.shape
    d_ff = w_down.shape[0]
    tm, tn = 512, 512
    per_seq = seq // tm
    nj = d_ff // tn
    hb = tm // FFN_HALO
    cb = conv_b.reshape(1, 2 * d_ff)
    return pl.pallas_call(
        functools.partial(_ffn_kernel, per_seq=per_seq),
        grid=(m_rows // tm, nj),
        in_specs=[
            pl.BlockSpec((tm, d), lambda i, j: (i, 0)),
            pl.BlockSpec((FFN_HALO, d), lambda i, j: (jnp.maximum(i * hb - 1, 0), 0)),
            pl.BlockSpec((1, 6, d), lambda i, j: (i // per_seq, 0, 0)),
            pl.BlockSpec((1, d), lambda i, j: (0, 0)),
            pl.BlockSpec((d, tn), lambda i, j: (0, j)),
            pl.BlockSpec((d, tn), lambda i, j: (0, nj + j)),
            pl.BlockSpec((3, tn), lambda i, j: (0, j)),
            pl.BlockSpec((3, tn), lambda i, j: (0, nj + j)),
            pl.BlockSpec((1, tn), lambda i, j: (0, j)),
            pl.BlockSpec((1, tn), lambda i, j: (0, nj + j)),
            pl.BlockSpec((tn, d), lambda i, j: (j, 0)),
        ],
        out_specs=pl.BlockSpec((tm, d), lambda i, j: (i, 0)),
        out_shape=jax.ShapeDtypeStruct((m_rows, d), F32),
        scratch_shapes=[pltpu.VMEM((tm + FFN_HALO, d), BF16), pltpu.VMEM((tm, d), F32)],
        compiler_params=_cparams("parallel", "arbitrary"),
        name="conv_glu_ffn",
    )(x2, x2, mod, g, w_up, w_up, conv_w, conv_w, cb, cb, w_down)


def _reorder_w_in(w):
    d = w.shape[0]
    sb, conv, nsa = 1536, 1024, 1304
    off_conv, off_nsa, off_mla = sb, sb + conv, sb + conv + nsa
    z = lambda n: jnp.zeros((d, n), w.dtype)
    nsa_q = w[:, off_nsa:off_nsa + 512]
    nsa_kv = w[:, off_nsa + 512:off_nsa + 1280]
    nsa_g = w[:, off_nsa + 1280:off_nsa + 1304]
    mla_lat = w[:, off_mla:off_mla + MLA_Q_RANK + MLA_KV_RANK]
    mla_kr = w[:, off_mla + MLA_Q_RANK + MLA_KV_RANK:off_mla + MLA_Q_RANK + MLA_KV_RANK + MLA_ROPE]
    gk = jnp.concatenate([nsa_g, z(MLA_NOPE - 24), mla_kr, z(LANES - MLA_QK)], axis=1)
    out = jnp.concatenate([w[:, :sb], nsa_kv, mla_lat, gk, nsa_q, w[:, off_conv:off_conv + conv]], axis=1)
    assert out.shape[1] == U_COLS
    return out.astype(BF16)


def _rope_tables(seq, dim, lane_cos, lane_sin):
    inv = ROPE_THETA ** (-jnp.arange(0, dim, 2, dtype=F32) / dim)
    ang = jnp.arange(seq).astype(F32)[:, None] * inv[None, :]
    cos, sin = jnp.cos(ang), jnp.sin(ang)
    return lane_cos(cos), lane_sin(sin)


def _nsa_tables(seq):
    return _rope_tables(seq, HEAD_DIM,
                        lambda c: jnp.concatenate([c, c, c, c], axis=1),
                        lambda s: jnp.concatenate([-s, s, -s, s], axis=1))


def _mla_tables(seq):
    ones = jnp.ones((seq, MLA_NOPE), F32)
    zeros = jnp.zeros((seq, MLA_NOPE), F32)
    pad1 = jnp.ones((seq, LANES - MLA_QK), F32)
    pad0 = jnp.zeros((seq, LANES - MLA_QK), F32)
    return _rope_tables(seq, MLA_ROPE,
                        lambda c: jnp.concatenate([ones, c, c, pad1], axis=1),
                        lambda s: jnp.concatenate([zeros, -s, s, pad0], axis=1))


def _overlap_matrix(seq):
    n_cmp = (seq - NSA_CMP_LEN) // NSA_CMP_STRIDE + 1
    n_sblk = seq // NSA_SEL_LEN
    starts = np.arange(n_cmp) * NSA_CMP_STRIDE
    sel_start = np.arange(n_sblk) * NSA_SEL_LEN
    ov = np.clip(np.minimum(starts[:, None] + NSA_CMP_LEN, sel_start[None, :] + NSA_SEL_LEN)
                 - np.maximum(starts[:, None], sel_start[None, :]), 0, None) / NSA_CMP_LEN
    full = np.zeros((LANES, LANES), np.float32)
    full[:n_cmp, :n_sblk] = ov
    return jnp.asarray(full, BF16)


def _pad_lanes(v, n):
    return jnp.pad(v, (0, n - v.shape[0]))


def _mixer(x2, mod, batch, seq, p):
    m_rows = x2.shape[0]
    u2 = _in_proj(x2, mod, p["norm_mix"].reshape(1, -1), _reorder_w_in(p["w_in"]), seq)
    u3 = u2.reshape(batch, seq, U_COLS)

    o_sb = _sb_attention(u3)
    o_conv = _conformer(u3, p["conv_dw_w"], p["conv_dw_b"], p["conv_ln_g"], p["conv_ln_b"],
                        p["conv_pw_w"], p["conv_pw_b"])

    cos_n, sin_n = _nsa_tables(seq)
    q_gain = jnp.tile(p["nsa_q_norm"], 2).reshape(1, LANES)
    k_gain = jnp.tile(p["nsa_k_norm"], (1, 2))
    qn, kc, ksd, vsd, kwd, vwd, gates = _nsa_prep(u2, cos_n, sin_n, q_gain, k_gain, seq)
    nrow = seq // NSA_CMP_STRIDE

    def blocks(t2):
        t5 = t2.reshape(batch, nrow, NSA_CMP_STRIDE, 2, HEAD_DIM)
        return t5.transpose(0, 3, 1, 2, 4).reshape(batch, 2, nrow, NSA_CMP_STRIDE * HEAD_DIM)

    vc_raw = u2[:, U_NKV + LANES:U_NKV + 2 * LANES]
    half = NSA_CMP_STRIDE * HEAD_DIM
    pe = p["nsa_cmp_pe"].reshape(2, 2, half).reshape(4, half)
    wdup = jnp.concatenate([p["nsa_cmp_w"], p["nsa_cmp_w"]], axis=-1).astype(BF16)
    wk = wdup[0].reshape(2, half, LANES)
    wv = wdup[1].reshape(2, half, LANES)
    kcmp, vcmp = _compress(blocks(kc), blocks(vc_raw), pe, wk, wv)
    o_nsa = _nsa_attention(qn.reshape(batch, seq, -1), kcmp, vcmp,
                           ksd.reshape(batch, seq, -1), vsd.reshape(batch, seq, -1),
                           kwd.reshape(batch, seq, -1), vwd.reshape(batch, seq, -1),
                           gates.reshape(batch, seq, -1), _overlap_matrix(seq))

    cos_m, sin_m = _mla_tables(seq)
    w_uq = p["mla_w_uq"].reshape(MLA_Q_RANK, MLA_HEADS, MLA_QK)
    wq = jnp.pad(w_uq, ((0, 0), (0, 0), (0, LANES - MLA_QK))).reshape(MLA_Q_RANK, MLA_HEADS * LANES).astype(BF16)
    w_ukv = p["mla_w_ukv"].reshape(MLA_KV_RANK, MLA_HEADS, 2 * HEAD_DIM)
    wk_m = jnp.pad(w_ukv[:, :, :MLA_NOPE], ((0, 0), (0, 0), (0, LANES - MLA_NOPE)))
    wk_m = wk_m.reshape(MLA_KV_RANK, MLA_HEADS * LANES).astype(BF16)
    wv_m = w_ukv[:, :, MLA_NOPE:].reshape(MLA_KV_RANK, GROUP_WIDTH).astype(BF16)
    qm, km, vm = _mla_prep(u2, cos_m, sin_m, p["mla_q_lat_norm"].reshape(1, -1),
                           p["mla_kv_lat_norm"].reshape(1, -1), wq, wk_m, wv_m,
                           _pad_lanes(p["mla_q_norm"], LANES).reshape(1, LANES),
                           _pad_lanes(p["mla_k_norm"], LANES).reshape(1, LANES), seq)
    o_mla = _mla_attention(qm.reshape(batch, seq, -1), km.reshape(batch, seq, -1), vm.reshape(batch, seq, -1))

    parts = [o.reshape(m_rows, GROUP_WIDTH) for o in (o_sb, o_conv, o_nsa, o_mla)]
    return _out_proj(parts, p["group_norm"].reshape(1, -1), p["w_o"].astype(BF16), x2, mod, seq)


def kernel(x, c, ada_w, ada_b, norm_mix, norm_ffn, w_in, conv_dw_w, conv_dw_b, conv_ln_g, conv_ln_b, conv_pw_w, conv_pw_b, nsa_q_norm, nsa_k_norm, nsa_cmp_pe, nsa_cmp_w, mla_q_lat_norm, mla_kv_lat_norm, mla_w_uq, mla_w_ukv, mla_q_norm, mla_k_norm, group_norm, w_o, ffn_up, ffn_conv_w, ffn_conv_b, ffn_down):
    batch, seq, d = x.shape
    depth = ada_w.shape[0]
    per_layer = dict(
        norm_mix=norm_mix, w_in=w_in, conv_dw_w=conv_dw_w, conv_dw_b=conv_dw_b, conv_ln_g=conv_ln_g,
        conv_ln_b=conv_ln_b, conv_pw_w=conv_pw_w, conv_pw_b=conv_pw_b, nsa_q_norm=nsa_q_norm,
        nsa_k_norm=nsa_k_norm, nsa_cmp_pe=nsa_cmp_pe, nsa_cmp_w=nsa_cmp_w, mla_q_lat_norm=mla_q_lat_norm,
        mla_kv_lat_norm=mla_kv_lat_norm, mla_w_uq=mla_w_uq, mla_w_ukv=mla_w_ukv, mla_q_norm=mla_q_norm,
        mla_k_norm=mla_k_norm, group_norm=group_norm, w_o=w_o)
    mods = _ada(c, ada_w, ada_b)
    x2 = x.reshape(batch * seq, d)
    for l in range(depth):
        p = {k: v[l] for k, v in per_layer.items()}
        x2 = _mixer(x2, mods[l], batch, seq, p)
        x2 = _ffn(x2, mods[l], norm_ffn[l].reshape(1, -1), ffn_up[l].astype(BF16), ffn_conv_w[l],
                  ffn_conv_b[l], ffn_down[l].astype(BF16), seq)
    return x2.reshape(batch, seq, d)
```

```python
import functools

import numpy as np
import jax
import jax.numpy as jnp
from jax import lax
from jax.experimental import pallas as pl
from jax.experimental.pallas import tpu as pltpu

F32 = jnp.float32
BF16 = jnp.bfloat16

LANES = 128
VMEM_LIMIT = 56 * 1024 * 1024

HEAD_DIM = 64
HEAD_SHIFT = 6
ROPE_THETA = 10000.0
EPS = 1e-6
GROUP_WIDTH = 512
CONV_WIDTH = 31
CONV_LN_EPS = 1e-5
NSA_CMP_LEN = 32
NSA_CMP_STRIDE = 16
NSA_SEL_LEN = 64
SEL_SHIFT = 6
NSA_N_SEL = 16
NSA_WINDOW = 512
NSA_FORCE_BONUS = 1e3
MLA_Q_RANK = 384
MLA_KV_RANK = 256
MLA_NOPE = 64
MLA_ROPE = 32
MLA_QK = MLA_NOPE + MLA_ROPE
MLA_HEADS = 8

U_SBQ, U_SBK, U_SBV = 0, 512, 1024
U_SB_COLS = 1536
U_NKV = 0
U_MLA = 768
U_NQ = 1536
U_CA, U_CG = 2048, 2560
U_COLS = 4608

TQ = 256
TK = 256
NEG = -1e30


def _cparams(*sem):
    return pltpu.CompilerParams(dimension_semantics=sem, vmem_limit_bytes=VMEM_LIMIT)


def _dot(a, b):
    return jnp.dot(a, b, preferred_element_type=F32)


def _dot_nt(a, b):
    return lax.dot_general(a, b, (((1,), (1,)), ((), ())), preferred_element_type=F32)


def _split_dot(x, w):
    hi = x.astype(BF16)
    lo = (x - hi.astype(F32)).astype(BF16)
    return _dot(hi, w) + _dot(lo, w)


def _lane_iota(n=LANES):
    return lax.broadcasted_iota(jnp.int32, (1, n), 1)


def _ada_kernel(c_ref, w_ref, b_ref, o_ref):
    c = c_ref[...]
    cond = (c * jax.nn.sigmoid(c)).astype(BF16)
    o_ref[0] = _dot(cond, w_ref[0].astype(BF16)) + b_ref[0]


def _ada(c, ada_w, ada_b):
    n_layers, d, n = ada_w.shape
    b = c.shape[0]
    rows = 8
    tn = 1024
    cpad = jnp.pad(c, ((0, rows - b), (0, 0)))
    out = pl.pallas_call(
        _ada_kernel,
        grid=(n_layers, n // tn),
        in_specs=[
            pl.BlockSpec((rows, d), lambda l, j: (0, 0)),
            pl.BlockSpec((1, d, tn), lambda l, j: (l, 0, j)),
            pl.BlockSpec((1, 1, tn), lambda l, j: (l, 0, j)),
        ],
        out_specs=pl.BlockSpec((1, rows, tn), lambda l, j: (l, 0, j)),
        out_shape=jax.ShapeDtypeStruct((n_layers, rows, n), F32),
        compiler_params=_cparams("parallel", "parallel"),
        name="ada_mod",
    )(cpad, ada_w, ada_b.reshape(n_layers, 1, n))
    return out[:, :b].reshape(n_layers, b, 6, d)


def _modulated_norm(x, g, shift, scale):
    ms = jnp.mean(x * x, axis=-1, keepdims=True)
    return x * lax.rsqrt(ms + EPS) * g * (1.0 + scale) + shift


def _in_kernel(x_ref, mod_ref, g_ref, w_ref, sb_ref, o_ref, h_ref, *, n_sb):
    j = pl.program_id(1)

    @pl.when(j == 0)
    def _():
        m = mod_ref[0]
        h_ref[...] = _modulated_norm(x_ref[...], g_ref[...], m[0:1], m[1:2]).astype(BF16)

    @pl.when(j < n_sb)
    def _():
        sb_ref[...] = _dot(h_ref[...], w_ref[...]).astype(BF16)

    @pl.when(j >= n_sb)
    def _():
        o_ref[...] = _dot(h_ref[...], w_ref[...])


def _in_proj(x2, mod, g, w, seq):
    m_rows, d = x2.shape
    n = w.shape[1]
    tm, tn = 512, 512
    per_seq = seq // tm
    n_sb = U_SB_COLS // tn
    return pl.pallas_call(
        functools.partial(_in_kernel, n_sb=n_sb),
        grid=(m_rows // tm, n // tn),
        in_specs=[
            pl.BlockSpec((tm, d), lambda i, j: (i, 0)),
            pl.BlockSpec((1, 6, d), lambda i, j: (i // per_seq, 0, 0)),
            pl.BlockSpec((1, d), lambda i, j: (0, 0)),
            pl.BlockSpec((d, tn), lambda i, j: (0, j)),
        ],
        out_specs=[
            pl.BlockSpec((tm, tn), lambda i, j: (i, jnp.minimum(j, n_sb - 1))),
            pl.BlockSpec((tm, tn), lambda i, j: (i, jnp.maximum(j - n_sb, 0))),
        ],
        out_shape=[jax.ShapeDtypeStruct((m_rows, U_SB_COLS), BF16),
                   jax.ShapeDtypeStruct((m_rows, n - U_SB_COLS), F32)],
        scratch_shapes=[pltpu.VMEM((tm, d), BF16)],
        compiler_params=_cparams("parallel", "arbitrary"),
        name="in_proj",
    )(x2, mod, g, w)


def _diag_masks():
    r = lax.broadcasted_iota(jnp.int32, (TQ, TK), 0)
    c = lax.broadcasted_iota(jnp.int32, (TQ, TK), 1)
    return c <= r, c < r


def _sb_kernel(q_ref, k_ref, v_ref, o_ref, z_ref, a_ref, *, nq):
    i = pl.program_id(2)
    lane = _lane_iota()
    q2 = q_ref[0] * (HEAD_DIM ** -0.5)
    qs = (jnp.where(lane < HEAD_DIM, q2, jnp.zeros_like(q2)),
          jnp.where(lane >= HEAD_DIM, q2, jnp.zeros_like(q2)))
    tri = (lax.broadcasted_iota(jnp.int32, (TK, TK), 0)
           > lax.broadcasted_iota(jnp.int32, (TK, TK), 1)).astype(BF16)
    _, strict = _diag_masks()

    for c in range(nq):
        @pl.when(i == c)
        def _():
            kv_len = (c + 1) * TK
            outs = []
            for hd in range(2):
                z_ref[hd, :, :kv_len] = _dot_nt(qs[hd], k_ref[0, :kv_len, :])
                carry = jnp.zeros((TQ, 1), F32)
                for j in reversed(range(c + 1)):
                    sl = slice(j * TK, (j + 1) * TK)
                    z = z_ref[hd, :, sl]
                    sp = jnp.log(1.0 + jnp.exp(-jnp.abs(z)))
                    log_beta = jnp.minimum(z, 0.0) - sp
                    log_1m = jnp.minimum(-z, 0.0) - sp
                    if j == c:
                        log_1m = jnp.where(strict, log_1m, 0.0)
                    after = _split_dot(log_1m, tri) + carry
                    a = jnp.exp(log_beta + after)
                    if j == c:
                        a = jnp.where(strict, a, 0.0)
                    a_ref[hd, :, sl] = a.astype(BF16)
                    carry = carry + jnp.sum(log_1m, axis=-1, keepdims=True)
                outs.append(_dot(a_ref[hd, :, :kv_len], v_ref[0, :kv_len, :]))
            o_ref[0] = jnp.where(lane < HEAD_DIM, outs[0], outs[1])


def _sb_attention(usb3):
    b, t, _ = usb3.shape
    nq = t // TQ
    pairs = GROUP_WIDTH // LANES
    return pl.pallas_call(
        functools.partial(_sb_kernel, nq=nq),
        grid=(b, pairs, nq),
        in_specs=[
            pl.BlockSpec((1, TQ, LANES), lambda bb, p, i: (bb, i, U_SBQ // LANES + p)),
            pl.BlockSpec((1, t, LANES), lambda bb, p, i: (bb, 0, U_SBK // LANES + p)),
            pl.BlockSpec((1, t, LANES), lambda bb, p, i: (bb, 0, U_SBV // LANES + p)),
        ],
        out_specs=pl.BlockSpec((1, TQ, LANES), lambda bb, p, i: (bb, i, p)),
        out_shape=jax.ShapeDtypeStruct((b, t, GROUP_WIDTH), F32),
        scratch_shapes=[pltpu.VMEM((2, TQ, t), F32), pltpu.VMEM((2, TQ, t), BF16)],
        compiler_params=_cparams("parallel", "parallel", "arbitrary"),
        name="sb_attention",
    )(usb3, usb3, usb3)


CONV_HALO = 32


def _conv_kernel(a_ref, g_ref, ah_ref, gh_ref, dww_ref, dwb_ref, lng_ref, lnb_ref, pw_ref, pwb_ref,
                 o_ref, h_ref, *, tm):
    i = pl.program_id(1)
    h_ref[CONV_HALO:, :] = a_ref[0] * jax.nn.sigmoid(g_ref[0])
    halo = ah_ref[0] * jax.nn.sigmoid(gh_ref[0])
    h_ref[0:CONV_HALO, :] = jnp.where(i == 0, 0.0, halo)
    acc = jnp.zeros((tm, GROUP_WIDTH), F32) + dwb_ref[...]
    base = CONV_HALO - (CONV_WIDTH - 1)
    for k in range(CONV_WIDTH):
        acc = acc + dww_ref[k:k + 1, :] * h_ref[base + k:base + k + tm, :]
    mu = jnp.mean(acc, axis=-1, keepdims=True)
    cen = acc - mu
    var = jnp.mean(cen * cen, axis=-1, keepdims=True)
    hn = cen * lax.rsqrt(var + CONV_LN_EPS) * lng_ref[...] + lnb_ref[...]
    act = hn * jax.nn.sigmoid(hn)
    o_ref[0] = _dot(act.astype(BF16), pw_ref[...]) + pwb_ref[...]


def _conformer(u3, dw_w, dw_b, ln_g, ln_b, pw_w, pw_b):
    b, t, _ = u3.shape
    tm = 512
    cw = GROUP_WIDTH
    hb = tm // CONV_HALO

    def halo_map(col):
        return lambda bb, i: (bb, jnp.maximum(i * hb - 1, 0), col)

    row = lambda bb, i: (0, 0)
    return pl.pallas_call(
        functools.partial(_conv_kernel, tm=tm),
        grid=(b, t // tm),
        in_specs=[
            pl.BlockSpec((1, tm, cw), lambda bb, i: (bb, i, U_CA // cw)),
            pl.BlockSpec((1, tm, cw), lambda bb, i: (bb, i, U_CG // cw)),
            pl.BlockSpec((1, CONV_HALO, cw), halo_map(U_CA // cw)),
            pl.BlockSpec((1, CONV_HALO, cw), halo_map(U_CG // cw)),
            pl.BlockSpec((CONV_WIDTH, cw), row),
            pl.BlockSpec((1, cw), row),
            pl.BlockSpec((1, cw), row),
            pl.BlockSpec((1, cw), row),
            pl.BlockSpec((cw, cw), row),
            pl.BlockSpec((1, cw), row),
        ],
        out_specs=pl.BlockSpec((1, tm, cw), lambda bb, i: (bb, i, 0)),
        out_shape=jax.ShapeDtypeStruct((b, t, cw), F32),
        scratch_shapes=[pltpu.VMEM((tm + CONV_HALO, cw), F32)],
        compiler_params=_cparams("parallel", "parallel"),
        name="conformer_conv",
    )(u3, u3, u3, u3, dw_w, dw_b.reshape(1, cw), ln_g.reshape(1, cw), ln_b.reshape(1, cw),
      pw_w.astype(BF16), pw_b.reshape(1, cw))


def _head_norm_rope(x, gain, cos, sin_signed, seg_mean):
    ms = _split_dot(x * x, seg_mean)
    y = x * lax.rsqrt(ms + EPS) * gain
    lane = _lane_iota()
    half = HEAD_DIM // 2
    first = (lane & (HEAD_DIM - 1)) < half
    partner = jnp.where(first, pltpu.roll(y, LANES - half, 1), pltpu.roll(y, half, 1))
    return y * cos + partner * sin_signed


def _dup(x, g):
    lane = _lane_iota()
    sw = pltpu.roll(x, HEAD_DIM, 1)
    if g == 0:
        return jnp.where(lane < HEAD_DIM, x, sw)
    return jnp.where(lane < HEAD_DIM, sw, x)


def _nsa_prep_kernel(q_ref, kv_ref, gk_ref, cos_ref, sin_ref, qg_ref, kg_ref,
                     qn_ref, kc_ref, kse_ref, kso_ref, vs_ref, kw_ref, vw_ref, gt_ref, *, per_seq):
    cos = cos_ref[...]
    sin = sin_ref[...]
    seg = lax.broadcasted_iota(jnp.int32, (LANES, LANES), 0) >> HEAD_SHIFT
    seg_mean = jnp.where(seg == lax.broadcasted_iota(jnp.int32, (LANES, LANES), 1) >> HEAD_SHIFT,
                         1.0 / HEAD_DIM, 0.0).astype(BF16)
    scale = HEAD_DIM ** -0.5
    for p in range(GROUP_WIDTH // LANES):
        x = q_ref[:, p * LANES:(p + 1) * LANES]
        qn_ref[:, p * LANES:(p + 1) * LANES] = (
            _head_norm_rope(x, qg_ref[...], cos, sin, seg_mean) * scale).astype(BF16)

    def blk(n):
        return kv_ref[:, n * LANES:(n + 1) * LANES]

    kc_ref[...] = _head_norm_rope(blk(0), kg_ref[0:1, :], cos, sin, seg_mean)
    ks = _head_norm_rope(blk(2), kg_ref[1:2, :], cos, sin, seg_mean)
    kw = _head_norm_rope(blk(4), kg_ref[2:3, :], cos, sin, seg_mean)
    vs = blk(3)
    vw = blk(5)
    tm = ks.shape[0]
    lane = _lane_iota()
    t = (pl.program_id(0) % per_seq) * tm + lax.broadcasted_iota(jnp.int32, (tm, 1), 0)
    sblk = t >> SEL_SHIFT
    hot_lo = jnp.where(lane == sblk, 1.0, 0.0)
    hot_hi = jnp.where(lane - HEAD_DIM == sblk, 1.0, 0.0)
    ks_sw = pltpu.roll(ks, HEAD_DIM, 1)
    for g in range(2):
        sl = slice(g * LANES, (g + 1) * LANES)
        kse_ref[:, sl] = jnp.where(lane < HEAD_DIM, ks if g == 0 else ks_sw, hot_hi).astype(BF16)
        kso_ref[:, sl] = jnp.where(lane >= HEAD_DIM, ks_sw if g == 0 else ks, hot_lo).astype(BF16)
        vs_ref[:, sl] = _dup(vs, g).astype(BF16)
        kw_ref[:, sl] = _dup(kw, g).astype(BF16)
        vw_ref[:, sl] = _dup(vw, g).astype(BF16)
    gates = jax.nn.sigmoid(gk_ref[...])
    gt_ref[:, 0:LANES] = gates
    gt_ref[:, LANES:2 * LANES] = pltpu.roll(gates, LANES - 12, 1)


def _nsa_prep(u2, cos, sin, q_gain, k_gain, seq):
    m_rows = u2.shape[0]
    tm = 512
    per_seq = seq // tm
    rowmap = lambda i: (i, 0)
    tab = lambda i: (i % per_seq, 0)
    const = lambda i: (0, 0)
    outs = [
        jax.ShapeDtypeStruct((m_rows, GROUP_WIDTH), BF16),
        jax.ShapeDtypeStruct((m_rows, LANES), F32),
        jax.ShapeDtypeStruct((m_rows, 2 * LANES), BF16),
        jax.ShapeDtypeStruct((m_rows, 2 * LANES), BF16),
        jax.ShapeDtypeStruct((m_rows, 2 * LANES), BF16),
        jax.ShapeDtypeStruct((m_rows, 2 * LANES), BF16),
        jax.ShapeDtypeStruct((m_rows, 2 * LANES), BF16),
        jax.ShapeDtypeStruct((m_rows, 2 * LANES), F32),
    ]
    return pl.pallas_call(
        functools.partial(_nsa_prep_kernel, per_seq=per_seq),
        grid=(m_rows // tm,),
        in_specs=[
            pl.BlockSpec((tm, GROUP_WIDTH), lambda i: (i, U_NQ // GROUP_WIDTH)),
            pl.BlockSpec((tm, 6 * LANES), lambda i: (i, U_NKV // (6 * LANES))),
            pl.BlockSpec((tm, LANES), lambda i: (i, (U_MLA + MLA_Q_RANK + MLA_KV_RANK) // LANES)),
            pl.BlockSpec((tm, LANES), tab),
            pl.BlockSpec((tm, LANES), tab),
            pl.BlockSpec((1, LANES), const),
            pl.BlockSpec((3, LANES), const),
        ],
        out_specs=[
            pl.BlockSpec((tm, GROUP_WIDTH), rowmap),
            pl.BlockSpec((tm, LANES), rowmap),
            pl.BlockSpec((tm, 2 * LANES), rowmap),
            pl.BlockSpec((tm, 2 * LANES), rowmap),
            pl.BlockSpec((tm, 2 * LANES), rowmap),
            pl.BlockSpec((tm, 2 * LANES), rowmap),
            pl.BlockSpec((tm, 2 * LANES), rowmap),
            pl.BlockSpec((tm, 2 * LANES), rowmap),
        ],
        out_shape=outs,
        compiler_params=_cparams("parallel"),
        name="nsa_prep",
    )(u2, u2, u2, cos, sin, q_gain, k_gain)


def _compress_kernel(xk_ref, xv_ref, pe_ref, wk_ref, wv_ref, kc_ref, vc_ref):
    nrow = xk_ref.shape[2]

    def comp(x, pe_a, pe_b, w_ref):
        a = _dot((x + pe_a).astype(BF16), w_ref[0])
        bm = _dot((x + pe_b).astype(BF16), w_ref[1])
        return a + pltpu.roll(bm, nrow - 1, 0)

    kc_ref[0, 0] = comp(xk_ref[0, 0], pe_ref[0:1, :], pe_ref[1:2, :], wk_ref).astype(BF16)
    vc_ref[0, 0] = comp(xv_ref[0, 0], pe_ref[2:3, :], pe_ref[3:4, :], wv_ref).astype(BF16)


def _compress(xk, xv, pe, wk, wv):
    b, g, nrow, wide = xk.shape
    xmap = lambda bb, gg: (bb, gg, 0, 0)
    out = jax.ShapeDtypeStruct((b, g, nrow, LANES), BF16)
    return pl.pallas_call(
        _compress_kernel,
        grid=(b, g),
        in_specs=[
            pl.BlockSpec((1, 1, nrow, wide), xmap),
            pl.BlockSpec((1, 1, nrow, wide), xmap),
            pl.BlockSpec((4, wide), lambda bb, gg: (0, 0)),
            pl.BlockSpec((2, wide, LANES), lambda bb, gg: (0, 0, 0)),
            pl.BlockSpec((2, wide, LANES), lambda bb, gg: (0, 0, 0)),
        ],
        out_specs=[pl.BlockSpec((1, 1, nrow, LANES), xmap), pl.BlockSpec((1, 1, nrow, LANES), xmap)],
        out_shape=[out, out],
        compiler_params=_cparams("parallel", "parallel"),
        name="nsa_compress",
    )(xk, xv, pe, wk, wv)


def _attend(q, k, v, s_ref, p_ref, masks, scale=None):
    n = k.shape[0] // TK
    s_ref[:, :n * TK] = _dot_nt(q, k)
    mfold = None
    for j in range(n):
        sl = slice(j * TK, (j + 1) * TK)
        s = s_ref[:, sl]
        if j in masks:
            s = jnp.where(masks[j], s, NEG)
            s_ref[:, sl] = s
        f = jnp.maximum(s[:, :LANES], s[:, LANES:])
        mfold = f if mfold is None else jnp.maximum(mfold, f)
    m = jnp.max(mfold, axis=-1, keepdims=True)
    lfold = None
    for j in range(n):
        sl = slice(j * TK, (j + 1) * TK)
        x = s_ref[:, sl] - m
        if scale is not None:
            x = x * scale
        p = jnp.exp(x)
        f = p[:, :LANES] + p[:, LANES:]
        lfold = f if lfold is None else lfold + f
        p_ref[:, sl] = p.astype(BF16)
    l = jnp.sum(lfold, axis=-1, keepdims=True)
    return _dot(p_ref[:, :n * TK], v) / l


def _nsa_kernel(q_ref, kc_ref, vc_ref, kse_ref, kso_ref, vs_ref, kw_ref, vw_ref, gt_ref, ov_ref,
                o_ref, s_ref, p_ref, sw_ref, pw_ref, *, nq, n_cmp, n_sblk, n_sel):
    i = pl.program_id(2)
    lane = _lane_iota()
    row = i * TQ + lax.broadcasted_iota(jnp.int32, (TQ, 1), 0)
    rep = 4
    q2s = [q_ref[0, :, p * LANES:(p + 1) * LANES] for p in range(2)]
    qs = []
    for q2 in q2s:
        qs.append(jnp.where(lane < HEAD_DIM, q2, jnp.zeros_like(q2)))
        qs.append(jnp.where(lane >= HEAD_DIM, q2, jnp.zeros_like(q2)))

    kc = kc_ref[0, 0]
    vc = vc_ref[0, 0]
    cmask = ((lane * NSA_CMP_STRIDE + (NSA_CMP_LEN - 1)) <= row) & (lane < n_cmp)
    psum = jnp.zeros((TQ, LANES), F32)
    o_cmp = []
    for r in range(rep):
        s = jnp.where(cmask, _dot_nt(qs[r], kc), NEG)
        mx = jnp.max(s, axis=-1, keepdims=True)
        p = jnp.where(cmask, jnp.exp(s - mx), 0.0)
        p = p / jnp.maximum(jnp.sum(p, axis=-1, keepdims=True), 1e-30)
        o_cmp.append(_dot(p.astype(BF16), vc))
        psum = psum + p
    imp = _split_dot(psum, ov_ref[...])
    cur = row >> SEL_SHIFT
    eligible = lane <= cur
    forced = (lane == 0) | (lane == cur) | (lane == cur - 1)
    score = jnp.where(eligible, imp + jnp.where(forced, NSA_FORCE_BONUS, 0.0), NEG)
    rank = jnp.zeros((TQ, LANES), F32)
    for sp in range(n_sblk):
        col = score[:, sp:sp + 1]
        beats = (col > score) | ((col == score) & (lane > sp))
        rank = rank + jnp.where(beats, 1.0, 0.0)
    chosen = eligible & (rank < n_sel)

    bias_lo = jnp.where(chosen, 0.0, jnp.where(lane < n_sblk, NEG, 0.0))
    bias_hi = pltpu.roll(bias_lo, HEAD_DIM, 1).astype(BF16)
    bias_lo = bias_lo.astype(BF16)
    q_aug = []
    for q2 in q2s:
        q_aug.append(jnp.where(lane < HEAD_DIM, q2, bias_hi))
        q_aug.append(jnp.where(lane >= HEAD_DIM, q2, bias_lo))
    causal, strict = _diag_masks()
    wtiles = NSA_WINDOW // TK
    gt = gt_ref[0]

    for c in range(nq):
        @pl.when(i == c)
        def _():
            kv_len = (c + 1) * TK
            lo = max(c - wtiles, 0) * TK
            wmasks = {c - max(c - wtiles, 0): causal}
            if c >= wtiles:
                wmasks[0] = jnp.logical_not(causal)
            outs = []
            for r in range(rep):
                ks_ref = kse_ref if r % 2 == 0 else kso_ref
                o_slc = _attend(q_aug[r], ks_ref[0, :kv_len, :], vs_ref[0, :kv_len, :],
                                s_ref.at[r], p_ref.at[r], {c: causal})
                o_win = _attend(qs[r], kw_ref[0, lo:kv_len, :], vw_ref[0, lo:kv_len, :],
                                sw_ref.at[r], pw_ref.at[r], wmasks)
                outs.append(gt[:, 3 * r:3 * r + 1] * o_cmp[r] + gt[:, 3 * r + 1:3 * r + 2] * o_slc
                            + gt[:, 3 * r + 2:3 * r + 3] * o_win)
            o_ref[0, :, 0:LANES] = jnp.where(lane < HEAD_DIM, outs[0], outs[1])
            o_ref[0, :, LANES:2 * LANES] = jnp.where(lane < HEAD_DIM, outs[2], outs[3])


def _nsa_attention(qn, kcmp, vcmp, kse, kso, vsd, kwd, vwd, gates, overlap):
    b, t, _ = qn.shape
    nq = t // TQ
    n_cmp = (t - NSA_CMP_LEN) // NSA_CMP_STRIDE + 1
    n_sblk = t // NSA_SEL_LEN
    nrow = kcmp.shape[2]
    wlen = NSA_WINDOW + TK
    kv = lambda bb, g, i: (bb, 0, g)
    cm = lambda bb, g, i: (bb, g, 0, 0)
    return pl.pallas_call(
        functools.partial(_nsa_kernel, nq=nq, n_cmp=n_cmp, n_sblk=n_sblk, n_sel=min(NSA_N_SEL, n_sblk)),
        grid=(b, 2, nq),
        in_specs=[
            pl.BlockSpec((1, TQ, 2 * LANES), lambda bb, g, i: (bb, i, g)),
            pl.BlockSpec((1, 1, nrow, LANES), cm),
            pl.BlockSpec((1, 1, nrow, LANES), cm),
            pl.BlockSpec((1, t, LANES), kv),
            pl.BlockSpec((1, t, LANES), kv),
            pl.BlockSpec((1, t, LANES), kv),
            pl.BlockSpec((1, t, LANES), kv),
            pl.BlockSpec((1, t, LANES), kv),
            pl.BlockSpec((1, TQ, LANES), lambda bb, g, i: (bb, i, g)),
            pl.BlockSpec((LANES, LANES), lambda bb, g, i: (0, 0)),
        ],
        out_specs=pl.BlockSpec((1, TQ, 2 * LANES), lambda bb, g, i: (bb, i, g)),
        out_shape=jax.ShapeDtypeStruct((b, t, GROUP_WIDTH), F32),
        scratch_shapes=[pltpu.VMEM((4, TQ, t), F32), pltpu.VMEM((4, TQ, t), BF16),
                        pltpu.VMEM((4, TQ, wlen), F32), pltpu.VMEM((4, TQ, wlen), BF16)],
        compiler_params=_cparams("parallel", "parallel", "arbitrary"),
        name="nsa_attention",
    )(qn, kcmp, vcmp, kse, kso, vsd, kwd, vwd, gates, overlap)


def _mla_prep_kernel(u_ref, cos_ref, sin_ref, qlg_ref, kvlg_ref, wq_ref, wk_ref, wv_ref, qg_ref, kg_ref,
                     q_ref, k_ref, v_ref):
    lane = _lane_iota()
    cos = cos_ref[...]
    sin = sin_ref[...]
    half = MLA_ROPE // 2

    def lat_norm(x, g):
        ms = jnp.mean(x * x, axis=-1, keepdims=True)
        return (x * lax.rsqrt(ms + EPS) * g).astype(BF16)

    def head_norm_rope(x, g):
        ms = jnp.sum(x * x, axis=-1, keepdims=True) * (1.0 / MLA_QK)
        y = x * lax.rsqrt(ms + EPS) * g
        partner = jnp.where(lane < MLA_NOPE + half, pltpu.roll(y, LANES - half, 1), pltpu.roll(y, half, 1))
        return y * cos + partner * sin

    q = _dot(lat_norm(u_ref[:, 0:MLA_Q_RANK], qlg_ref[...]), wq_ref[...])
    kvn = lat_norm(u_ref[:, MLA_Q_RANK:MLA_Q_RANK + MLA_KV_RANK], kvlg_ref[...])
    kn = _dot(kvn, wk_ref[...])
    v_ref[...] = _dot(kvn, wv_ref[...]).astype(BF16)
    gk = u_ref[:, MLA_Q_RANK + MLA_KV_RANK:MLA_Q_RANK + MLA_KV_RANK + LANES]
    k_rope = jnp.where((lane >= MLA_NOPE) & (lane < MLA_QK), gk, 0.0)
    for h in range(MLA_HEADS):
        sl = slice(h * LANES, (h + 1) * LANES)
        q_ref[:, sl] = head_norm_rope(q[:, sl], qg_ref[...]).astype(BF16)
        k_ref[:, sl] = head_norm_rope(kn[:, sl] + k_rope, kg_ref[...]).astype(BF16)


def _mla_prep(u2, cos, sin, q_lat_gain, kv_lat_gain, wq, wk, wv, q_gain, k_gain, seq):
    m_rows = u2.shape[0]
    tm = 512
    per_seq = seq // tm
    width = MLA_Q_RANK + MLA_KV_RANK + LANES
    rowmap = lambda i: (i, 0)
    tab = lambda i: (i % per_seq, 0)
    const = lambda i: (0, 0)
    return pl.pallas_call(
        _mla_prep_kernel,
        grid=(m_rows // tm,),
        in_specs=[
            pl.BlockSpec((tm, width), lambda i: (i, U_MLA // width)),
            pl.BlockSpec((tm, LANES), tab),
            pl.BlockSpec((tm, LANES), tab),
            pl.BlockSpec((1, MLA_Q_RANK), const),
            pl.BlockSpec((1, MLA_KV_RANK), const),
            pl.BlockSpec(wq.shape, const),
            pl.BlockSpec(wk.shape, const),
            pl.BlockSpec(wv.shape, const),
            pl.BlockSpec((1, LANES), const),
            pl.BlockSpec((1, LANES), const),
        ],
        out_specs=[
            pl.BlockSpec((tm, MLA_HEADS * LANES), rowmap),
            pl.BlockSpec((tm, MLA_HEADS * LANES), rowmap),
            pl.BlockSpec((tm, GROUP_WIDTH), rowmap),
        ],
        out_shape=[
            jax.ShapeDtypeStruct((m_rows, MLA_HEADS * LANES), BF16),
            jax.ShapeDtypeStruct((m_rows, MLA_HEADS * LANES), BF16),
            jax.ShapeDtypeStruct((m_rows, GROUP_WIDTH), BF16),
        ],
        compiler_params=_cparams("parallel"),
        name="mla_prep",
    )(u2, cos, sin, q_lat_gain, kv_lat_gain, wq, wk, wv, q_gain, k_gain)


def _mla_kernel(q_ref, k_ref, v_ref, o_ref, s_ref, p_ref, *, nq):
    i = pl.program_id(2)
    lane = _lane_iota()
    causal, _ = _diag_masks()
    scale = MLA_QK ** -0.5
    for c in range(nq):
        @pl.when(i == c)
        def _():
            kv_len = (c + 1) * TK
            outs = []
            for hd in range(2):
                sl = slice(hd * LANES, (hd + 1) * LANES)
                outs.append(_attend(q_ref[0, :, sl], k_ref[0, :kv_len, sl], v_ref[0, :kv_len, :],
                                    s_ref.at[hd], p_ref.at[hd], {c: causal}, scale=scale))
            o_ref[0] = jnp.where(lane < HEAD_DIM, outs[0], outs[1])


def _mla_attention(qm, km, vm):
    b, t, _ = qm.shape
    nq = t // TQ
    return pl.pallas_call(
        functools.partial(_mla_kernel, nq=nq),
        grid=(b, MLA_HEADS // 2, nq),
        in_specs=[
            pl.BlockSpec((1, TQ, 2 * LANES), lambda bb, p, i: (bb, i, p)),
            pl.BlockSpec((1, t, 2 * LANES), lambda bb, p, i: (bb, 0, p)),
            pl.BlockSpec((1, t, LANES), lambda bb, p, i: (bb, 0, p)),
        ],
        out_specs=pl.BlockSpec((1, TQ, LANES), lambda bb, p, i: (bb, i, p)),
        out_shape=jax.ShapeDtypeStruct((b, t, GROUP_WIDTH), F32),
        scratch_shapes=[pltpu.VMEM((2, TQ, t), F32), pltpu.VMEM((2, TQ, t), BF16)],
        compiler_params=_cparams("parallel", "parallel", "arbitrary"),
        name="mla_attention",
    )(qm, km, vm)


def _out_kernel(a_ref, b_ref, c_ref, d_ref, gn_ref, w_ref, x_ref, mod_ref, o_ref, h_ref):
    @pl.when(pl.program_id(1) == 0)
    def _():
        for gi, r in enumerate((a_ref, b_ref, c_ref, d_ref)):
            y = r[...]
            ms = jnp.mean(y * y, axis=-1, keepdims=True)
            sl = slice(gi * GROUP_WIDTH, (gi + 1) * GROUP_WIDTH)
            h_ref[:, sl] = (y * lax.rsqrt(ms + EPS) * gn_ref[:, sl]).astype(BF16)

    o_ref[...] = x_ref[...] + mod_ref[0, 2:3, :] * _dot(h_ref[...], w_ref[...])


def _out_proj(parts, gn, w, x2, mod, seq):
    m_rows, d = x2.shape
    tm, tn = 512, 512
    per_seq = seq // tm
    part = pl.BlockSpec((tm, GROUP_WIDTH), lambda i, j: (i, 0))
    return pl.pallas_call(
        _out_kernel,
        grid=(m_rows // tm, d // tn),
        in_specs=[
            part, part, part, part,
            pl.BlockSpec((1, 4 * GROUP_WIDTH), lambda i, j: (0, 0)),
            pl.BlockSpec((4 * GROUP_WIDTH, tn), lambda i, j: (0, j)),
            pl.BlockSpec((tm, tn), lambda i, j: (i, j)),
            pl.BlockSpec((1, 6, tn), lambda i, j: (i // per_seq, 0, j)),
        ],
        out_specs=pl.BlockSpec((tm, tn), lambda i, j: (i, j)),
        out_shape=jax.ShapeDtypeStruct((m_rows, d), F32),
        scratch_shapes=[pltpu.VMEM((tm, 4 * GROUP_WIDTH), BF16)],
        compiler_params=_cparams("parallel", "arbitrary"),
        name="out_proj",
    )(*parts, gn, w, x2, mod)


FFN_HALO = 16


def _ffn_kernel(x_ref, xh_ref, mod_ref, g_ref, wa_ref, wg_ref, cwa_ref, cwg_ref, cba_ref, cbg_ref, wd_ref,
                o_ref, h_ref, acc_ref, *, per_seq):
    i = pl.program_id(0)
    j = pl.program_id(1)
    m = mod_ref[0]

    @pl.when(j == 0)
    def _():
        h_ref[FFN_HALO:, :] = _modulated_norm(x_ref[...], g_ref[...], m[3:4], m[4:5]).astype(BF16)
        halo = _modulated_norm(xh_ref[...], g_ref[...], m[3:4], m[4:5])
        h_ref[0:FFN_HALO, :] = jnp.where(i % per_seq == 0, 0.0, halo).astype(BF16)
        acc_ref[...] = jnp.zeros_like(acc_ref)

    h = h_ref[...]

    def conv(u, cw_ref, cb_ref):
        y = cw_ref[2:3, :] * u + cw_ref[1:2, :] * pltpu.roll(u, 1, 0) + cw_ref[0:1, :] * pltpu.roll(u, 2, 0)
        return y[FFN_HALO:, :] + cb_ref[...]

    ya = conv(_dot(h, wa_ref[...]), cwa_ref, cba_ref)
    yg = conv(_dot(h, wg_ref[...]), cwg_ref, cbg_ref)
    act = (yg * jax.nn.sigmoid(yg)) * ya
    acc_ref[...] += _dot(act.astype(BF16), wd_ref[...])

    @pl.when(j == pl.num_programs(1) - 1)
    def _():
        o_ref[...] = x_ref[...] + m[5:6] * acc_ref[...]


def _ffn(x2, mod, g, w_up, conv_w, conv_b, w_down, seq):
    m_rows, d = x2.shape
    d_ff = w_down.shape[0]
    tm, tn = 512, 512
    per_seq = seq // tm
    nj = d_ff // tn
    hb = tm // FFN_HALO
    cb = conv_b.reshape(1, 2 * d_ff)
    return pl.pallas_call(
        functools.partial(_ffn_kernel, per_seq=per_seq),
        grid=(m_rows // tm, nj),
        in_specs=[
            pl.BlockSpec((tm, d), lambda i, j: (i, 0)),
            pl.BlockSpec((FFN_HALO, d), lambda i, j: (jnp.maximum(i * hb - 1, 0), 0)),
            pl.BlockSpec((1, 6, d), lambda i, j: (i // per_seq, 0, 0)),
            pl.BlockSpec((1, d), lambda i, j: (0, 0)),
            pl.BlockSpec((d, tn), lambda i, j: (0, j)),
            pl.BlockSpec((d, tn), lambda i, j: (0, nj + j)),
            pl.BlockSpec((3, tn), lambda i, j: (0, j)),
            pl.BlockSpec((3, tn), lambda i, j: (0, nj + j)),
            pl.BlockSpec((1, tn), lambda i, j: (0, j)),
            pl.BlockSpec((1, tn), lambda i, j: (0, nj + j)),
            pl.BlockSpec((tn, d), lambda i, j: (j, 0)),
        ],
        out_specs=pl.BlockSpec((tm, d), lambda i, j: (i, 0)),
        out_shape=jax.ShapeDtypeStruct((m_rows, d), F32),
        scratch_shapes=[pltpu.VMEM((tm + FFN_HALO, d), BF16), pltpu.VMEM((tm, d), F32)],
        compiler_params=_cparams("parallel", "arbitrary"),
        name="conv_glu_ffn",
    )(x2, x2, mod, g, w_up, w_up, conv_w, conv_w, cb, cb, w_down)


def _reorder_w_in(w):
    d = w.shape[0]
    sb, conv, nsa = 1536, 1024, 1304
    off_conv, off_nsa, off_mla = sb, sb + conv, sb + conv + nsa
    z = lambda n: jnp.zeros((d, n), w.dtype)
    nsa_q = w[:, off_nsa:off_nsa + 512]
    nsa_kv = w[:, off_nsa + 512:off_nsa + 1280]
    nsa_g = w[:, off_nsa + 1280:off_nsa + 1304]
    mla_lat = w[:, off_mla:off_mla + MLA_Q_RANK + MLA_KV_RANK]
    mla_kr = w[:, off_mla + MLA_Q_RANK + MLA_KV_RANK:off_mla + MLA_Q_RANK + MLA_KV_RANK + MLA_ROPE]
    gk = jnp.concatenate([nsa_g, z(MLA_NOPE - 24), mla_kr, z(LANES - MLA_QK)], axis=1)
    out = jnp.concatenate([w[:, :sb], nsa_kv, mla_lat, gk, nsa_q, w[:, off_conv:off_conv + conv]], axis=1)
    assert out.shape[1] == U_COLS
    return out.astype(BF16)


def _rope_tables(seq, dim, lane_cos, lane_sin):
    inv = ROPE_THETA ** (-jnp.arange(0, dim, 2, dtype=F32) / dim)
    ang = jnp.arange(seq).astype(F32)[:, None] * inv[None, :]
    cos, sin = jnp.cos(ang), jnp.sin(ang)
    return lane_cos(cos), lane_sin(sin)


def _nsa_tables(seq):
    return _rope_tables(seq, HEAD_DIM,
                        lambda c: jnp.concatenate([c, c, c, c], axis=1),
                        lambda s: jnp.concatenate([-s, s, -s, s], axis=1))


def _mla_tables(seq):
    ones = jnp.ones((seq, MLA_NOPE), F32)
    zeros = jnp.zeros((seq, MLA_NOPE), F32)
    pad1 = jnp.ones((seq, LANES - MLA_QK), F32)
    pad0 = jnp.zeros((seq, LANES - MLA_QK), F32)
    return _rope_tables(seq, MLA_ROPE,
                        lambda c: jnp.concatenate([ones, c, c, pad1], axis=1),
                        lambda s: jnp.concatenate([zeros, -s, s, pad0], axis=1))


def _overlap_matrix(seq):
    n_cmp = (seq - NSA_CMP_LEN) // NSA_CMP_STRIDE + 1
    n_sblk = seq // NSA_SEL_LEN
    starts = np.arange(n_cmp) * NSA_CMP_STRIDE
    sel_start = np.arange(n_sblk) * NSA_SEL_LEN
    ov = np.clip(np.minimum(starts[:, None] + NSA_CMP_LEN, sel_start[None, :] + NSA_SEL_LEN)
                 - np.maximum(starts[:, None], sel_start[None, :]), 0, None) / NSA_CMP_LEN
    full = np.zeros((LANES, LANES), np.float32)
    full[:n_cmp, :n_sblk] = ov
    return jnp.asarray(full, BF16)


def _pad_lanes(v, n):
    return jnp.pad(v, (0, n - v.shape[0]))


def _mixer(x2, mod, batch, seq, p):
    m_rows = x2.shape[0]
    usb, u2 = _in_proj(x2, mod, p["norm_mix"].reshape(1, -1), _reorder_w_in(p["w_in"]), seq)
    u3 = u2.reshape(batch, seq, U_COLS - U_SB_COLS)

    o_sb = _sb_attention(usb.reshape(batch, seq, U_SB_COLS))
    o_conv = _conformer(u3, p["conv_dw_w"], p["conv_dw_b"], p["conv_ln_g"], p["conv_ln_b"],
                        p["conv_pw_w"], p["conv_pw_b"])

    cos_n, sin_n = _nsa_tables(seq)
    q_gain = jnp.tile(p["nsa_q_norm"], 2).reshape(1, LANES)
    k_gain = jnp.tile(p["nsa_k_norm"], (1, 2))
    qn, kc, kse, kso, vsd, kwd, vwd, gates = _nsa_prep(u2, cos_n, sin_n, q_gain, k_gain, seq)
    nrow = seq // NSA_CMP_STRIDE

    def blocks(t2):
        t5 = t2.reshape(batch, nrow, NSA_CMP_STRIDE, 2, HEAD_DIM)
        return t5.transpose(0, 3, 1, 2, 4).reshape(batch, 2, nrow, NSA_CMP_STRIDE * HEAD_DIM)

    vc_raw = u2[:, U_NKV + LANES:U_NKV + 2 * LANES]
    half = NSA_CMP_STRIDE * HEAD_DIM
    pe = p["nsa_cmp_pe"].reshape(2, 2, half).reshape(4, half)
    wdup = jnp.concatenate([p["nsa_cmp_w"], p["nsa_cmp_w"]], axis=-1).astype(BF16)
    wk = wdup[0].reshape(2, half, LANES)
    wv = wdup[1].reshape(2, half, LANES)
    kcmp, vcmp = _compress(blocks(kc), blocks(vc_raw), pe, wk, wv)
    o_nsa = _nsa_attention(qn.reshape(batch, seq, -1), kcmp, vcmp,
                           kse.reshape(batch, seq, -1), kso.reshape(batch, seq, -1), vsd.reshape(batch, seq, -1),
                           kwd.reshape(batch, seq, -1), vwd.reshape(batch, seq, -1),
                           gates.reshape(batch, seq, -1), _overlap_matrix(seq))

    cos_m, sin_m = _mla_tables(seq)
    w_uq = p["mla_w_uq"].reshape(MLA_Q_RANK, MLA_HEADS, MLA_QK)
    wq = jnp.pad(w_uq, ((0, 0), (0, 0), (0, LANES - MLA_QK))).reshape(MLA_Q_RANK, MLA_HEADS * LANES).astype(BF16)
    w_ukv = p["mla_w_ukv"].reshape(MLA_KV_RANK, MLA_HEADS, 2 * HEAD_DIM)
    wk_m = jnp.pad(w_ukv[:, :, :MLA_NOPE], ((0, 0), (0, 0), (0, LANES - MLA_NOPE)))
    wk_m = wk_m.reshape(MLA_KV_RANK, MLA_HEADS * LANES).astype(BF16)
    wv_m = w_ukv[:, :, MLA_NOPE:].reshape(MLA_KV_RANK, GROUP_WIDTH).astype(BF16)
    qm, km, vm = _mla_prep(u2, cos_m, sin_m, p["mla_q_lat_norm"].reshape(1, -1),
                           p["mla_kv_lat_norm"].reshape(1, -1), wq, wk_m, wv_m,
                           _pad_lanes(p["mla_q_norm"], LANES).reshape(1, LANES),
                           _pad_lanes(p["mla_k_norm"], LANES).reshape(1, LANES), seq)
    o_mla = _mla_attention(qm.reshape(batch, seq, -1), km.reshape(batch, seq, -1), vm.reshape(batch, seq, -1))

    parts = [o.reshape(m_rows, GROUP_WIDTH) for o in (o_sb, o_conv, o_nsa, o_mla)]
    return _out_proj(parts, p["group_norm"].reshape(1, -1), p["w_o"].astype(BF16), x2, mod, seq)


def kernel(x, c, ada_w, ada_b, norm_mix, norm_ffn, w_in, conv_dw_w, conv_dw_b, conv_ln_g, conv_ln_b, conv_pw_w, conv_pw_b, nsa_q_norm, nsa_k_norm, nsa_cmp_pe, nsa_cmp_w, mla_q_lat_norm, mla_kv_lat_norm, mla_w_uq, mla_w_ukv, mla_q_norm, mla_k_norm, group_norm, w_o, ffn_up, ffn_conv_w, ffn_conv_b, ffn_down):
    batch, seq, d = x.shape
    depth = ada_w.shape[0]
    per_layer = dict(
        norm_mix=norm_mix, w_in=w_in, conv_dw_w=conv_dw_w, conv_dw_b=conv_dw_b, conv_ln_g=conv_ln_g,
        conv_ln_b=conv_ln_b, conv_pw_w=conv_pw_w, conv_pw_b=conv_pw_b, nsa_q_norm=nsa_q_norm,
        nsa_k_norm=nsa_k_norm, nsa_cmp_pe=nsa_cmp_pe, nsa_cmp_w=nsa_cmp_w, mla_q_lat_norm=mla_q_lat_norm,
        mla_kv_lat_norm=mla_kv_lat_norm, mla_w_uq=mla_w_uq, mla_w_ukv=mla_w_ukv, mla_q_norm=mla_q_norm,
        mla_k_norm=mla_k_norm, group_norm=group_norm, w_o=w_o)
    mods = _ada(c, ada_w, ada_b)
    x2 = x.reshape(batch * seq, d)
    for l in range(depth):
        p = {k: v[l] for k, v in per_layer.items()}
        x2 = _mixer(x2, mods[l], batch, seq, p)
        x2 = _ffn(x2, mods[l], norm_ffn[l].reshape(1, -1), ffn_up[l].astype(BF16), ffn_conv_w[l],
                  ffn_conv_b[l], ffn_down[l].astype(BF16), seq)
    return x2.reshape(batch, seq, d)
```

```python
import functools

import numpy as np
import jax
import jax.numpy as jnp
from jax import lax
from jax.experimental import pallas as pl
from jax.experimental.pallas import tpu as pltpu

F32 = jnp.float32
BF16 = jnp.bfloat16

LANES = 128
VMEM_LIMIT = 56 * 1024 * 1024

HEAD_DIM = 64
HEAD_SHIFT = 6
ROPE_THETA = 10000.0
EPS = 1e-6
GROUP_WIDTH = 512
CONV_WIDTH = 31
CONV_LN_EPS = 1e-5
NSA_CMP_LEN = 32
NSA_CMP_STRIDE = 16
NSA_SEL_LEN = 64
SEL_SHIFT = 6
NSA_N_SEL = 16
NSA_WINDOW = 512
NSA_FORCE_BONUS = 1e3
MLA_Q_RANK = 384
MLA_KV_RANK = 256
MLA_NOPE = 64
MLA_ROPE = 32
MLA_QK = MLA_NOPE + MLA_ROPE
MLA_HEADS = 8

U_SBQ, U_SBK, U_SBV = 0, 512, 1024
U_SB_COLS = 1536
U_CA, U_CG = 0, 512
U_NQ = 1024
U_NKV = 1536
U_MLA = 2304
U_COLS = 4608
REF_ALIGNED_COLS = 3840

TQ = 256
TK = 256
NEG = -1e30


def _cparams(*sem):
    return pltpu.CompilerParams(dimension_semantics=sem, vmem_limit_bytes=VMEM_LIMIT)


def _dot(a, b):
    return jnp.dot(a, b, preferred_element_type=F32)


def _dot_nt(a, b):
    return lax.dot_general(a, b, (((1,), (1,)), ((), ())), preferred_element_type=F32)


def _split_dot(x, w):
    hi = x.astype(BF16)
    lo = (x - hi.astype(F32)).astype(BF16)
    return _dot(hi, w) + _dot(lo, w)


def _lane_iota(n=LANES):
    return lax.broadcasted_iota(jnp.int32, (1, n), 1)


def _ada_kernel(c_ref, w_ref, b_ref, o_ref):
    c = c_ref[...]
    cond = (c * jax.nn.sigmoid(c)).astype(BF16)
    o_ref[0] = _dot(cond, w_ref[0].astype(BF16)) + b_ref[0]


def _ada(c, ada_w, ada_b):
    n_layers, d, n = ada_w.shape
    b = c.shape[0]
    rows = 8
    tn = 1024
    cpad = jnp.pad(c, ((0, rows - b), (0, 0)))
    out = pl.pallas_call(
        _ada_kernel,
        grid=(n_layers, n // tn),
        in_specs=[
            pl.BlockSpec((rows, d), lambda l, j: (0, 0)),
            pl.BlockSpec((1, d, tn), lambda l, j: (l, 0, j)),
            pl.BlockSpec((1, 1, tn), lambda l, j: (l, 0, j)),
        ],
        out_specs=pl.BlockSpec((1, rows, tn), lambda l, j: (l, 0, j)),
        out_shape=jax.ShapeDtypeStruct((n_layers, rows, n), F32),
        compiler_params=_cparams("parallel", "parallel"),
        name="ada_mod",
    )(cpad, ada_w, ada_b.reshape(n_layers, 1, n))
    return out[:, :b].reshape(n_layers, b, 6, d)


def _modulated_norm(x, g, shift, scale):
    ms = jnp.mean(x * x, axis=-1, keepdims=True)
    return x * lax.rsqrt(ms + EPS) * g * (1.0 + scale) + shift


def _in_kernel(x_ref, mod_ref, g_ref, w_ref, sb_ref, o_ref, h_ref, *, n_sb):
    j = pl.program_id(1)

    @pl.when(j == 0)
    def _():
        m = mod_ref[0]
        h_ref[...] = _modulated_norm(x_ref[...], g_ref[...], m[0:1], m[1:2]).astype(BF16)

    @pl.when(j < n_sb)
    def _():
        sb_ref[...] = _dot(h_ref[...], w_ref[...]).astype(BF16)

    @pl.when(j >= n_sb)
    def _():
        o_ref[...] = _dot(h_ref[...], w_ref[...])


def _in_proj(x2, mod, g, w, seq):
    m_rows, d = x2.shape
    n = w.shape[1]
    tm, tn = 1024, 768
    per_seq = seq // tm
    n_sb = U_SB_COLS // tn
    return pl.pallas_call(
        functools.partial(_in_kernel, n_sb=n_sb),
        grid=(m_rows // tm, n // tn),
        in_specs=[
            pl.BlockSpec((tm, d), lambda i, j: (i, 0)),
            pl.BlockSpec((1, 6, d), lambda i, j: (i // per_seq, 0, 0)),
            pl.BlockSpec((1, d), lambda i, j: (0, 0)),
            pl.BlockSpec((d, tn), lambda i, j: (0, j)),
        ],
        out_specs=[
            pl.BlockSpec((tm, tn), lambda i, j: (i, jnp.minimum(j, n_sb - 1))),
            pl.BlockSpec((tm, tn), lambda i, j: (i, jnp.maximum(j - n_sb, 0))),
        ],
        out_shape=[jax.ShapeDtypeStruct((m_rows, U_SB_COLS), BF16),
                   jax.ShapeDtypeStruct((m_rows, n - U_SB_COLS), F32)],
        scratch_shapes=[pltpu.VMEM((tm, d), BF16)],
        compiler_params=_cparams("parallel", "arbitrary"),
        name="in_proj",
    )(x2, mod, g, w)


def _diag_masks():
    r = lax.broadcasted_iota(jnp.int32, (TQ, TK), 0)
    c = lax.broadcasted_iota(jnp.int32, (TQ, TK), 1)
    return c <= r, c < r


def _sb_kernel(q_ref, k_ref, v_ref, o_ref, z_ref, a_ref, *, nq):
    i = pl.program_id(2)
    lane = _lane_iota()
    q2 = q_ref[0] * (HEAD_DIM ** -0.5)
    qs = (jnp.where(lane < HEAD_DIM, q2, jnp.zeros_like(q2)),
          jnp.where(lane >= HEAD_DIM, q2, jnp.zeros_like(q2)))
    tri = (lax.broadcasted_iota(jnp.int32, (TK, TK), 0)
           > lax.broadcasted_iota(jnp.int32, (TK, TK), 1)).astype(BF16)
    _, strict = _diag_masks()

    for c in range(nq):
        @pl.when(i == c)
        def _():
            kv_len = (c + 1) * TK
            outs = []
            for hd in range(2):
                z_ref[hd, :, :kv_len] = _dot_nt(qs[hd], k_ref[0, :kv_len, :])
                carry = jnp.zeros((TQ, 1), F32)
                for j in reversed(range(c + 1)):
                    sl = slice(j * TK, (j + 1) * TK)
                    z = z_ref[hd, :, sl]
                    sp = jnp.log(1.0 + jnp.exp(-jnp.abs(z)))
                    log_beta = jnp.minimum(z, 0.0) - sp
                    log_1m = jnp.minimum(-z, 0.0) - sp
                    if j == c:
                        log_1m = jnp.where(strict, log_1m, 0.0)
                    after = _split_dot(log_1m, tri) + carry
                    a = jnp.exp(log_beta + after)
                    if j == c:
                        a = jnp.where(strict, a, 0.0)
                    a_ref[hd, :, sl] = a.astype(BF16)
                    carry = carry + jnp.sum(log_1m, axis=-1, keepdims=True)
                outs.append(_dot(a_ref[hd, :, :kv_len], v_ref[0, :kv_len, :]))
            o_ref[0] = jnp.where(lane < HEAD_DIM, outs[0], outs[1])


def _sb_attention(usb3):
    b, t, _ = usb3.shape
    nq = t // TQ
    pairs = GROUP_WIDTH // LANES
    return pl.pallas_call(
        functools.partial(_sb_kernel, nq=nq),
        grid=(b, pairs, nq),
        in_specs=[
            pl.BlockSpec((1, TQ, LANES), lambda bb, p, i: (bb, i, U_SBQ // LANES + p)),
            pl.BlockSpec((1, t, LANES), lambda bb, p, i: (bb, 0, U_SBK // LANES + p)),
            pl.BlockSpec((1, t, LANES), lambda bb, p, i: (bb, 0, U_SBV // LANES + p)),
        ],
        out_specs=pl.BlockSpec((1, TQ, LANES), lambda bb, p, i: (bb, i, p)),
        out_shape=jax.ShapeDtypeStruct((b, t, GROUP_WIDTH), F32),
        scratch_shapes=[pltpu.VMEM((2, TQ, t), F32), pltpu.VMEM((2, TQ, t), BF16)],
        compiler_params=_cparams("parallel", "parallel", "arbitrary"),
        name="sb_attention",
    )(usb3, usb3, usb3)


CONV_HALO = 32


def _conv_kernel(a_ref, g_ref, ah_ref, gh_ref, dww_ref, dwb_ref, lng_ref, lnb_ref, pw_ref, pwb_ref,
                 o_ref, h_ref, *, tm):
    i = pl.program_id(1)
    h_ref[CONV_HALO:, :] = a_ref[0] * jax.nn.sigmoid(g_ref[0])
    halo = ah_ref[0] * jax.nn.sigmoid(gh_ref[0])
    h_ref[0:CONV_HALO, :] = jnp.where(i == 0, 0.0, halo)
    acc = jnp.zeros((tm, GROUP_WIDTH), F32) + dwb_ref[...]
    base = CONV_HALO - (CONV_WIDTH - 1)
    sub = 8
    for r in range(sub):
        taps = [k for k in range(CONV_WIDTH) if (base + k) % sub == r]
        if not taps:
            continue
        lo = base + taps[0]
        shifted = h_ref[lo:lo + (taps[-1] - taps[0]) + tm, :]
        for k in taps:
            acc = acc + dww_ref[k:k + 1, :] * shifted[k - taps[0]:k - taps[0] + tm, :]
    mu = jnp.mean(acc, axis=-1, keepdims=True)
    cen = acc - mu
    var = jnp.mean(cen * cen, axis=-1, keepdims=True)
    hn = cen * lax.rsqrt(var + CONV_LN_EPS) * lng_ref[...] + lnb_ref[...]
    act = hn * jax.nn.sigmoid(hn)
    o_ref[0] = _dot(act.astype(BF16), pw_ref[...]) + pwb_ref[...]


def _conformer(u3, dw_w, dw_b, ln_g, ln_b, pw_w, pw_b):
    b, t, _ = u3.shape
    tm = 512
    cw = GROUP_WIDTH
    hb = tm // CONV_HALO

    def halo_map(col):
        return lambda bb, i: (bb, jnp.maximum(i * hb - 1, 0), col)

    row = lambda bb, i: (0, 0)
    return pl.pallas_call(
        functools.partial(_conv_kernel, tm=tm),
        grid=(b, t // tm),
        in_specs=[
            pl.BlockSpec((1, tm, cw), lambda bb, i: (bb, i, U_CA // cw)),
            pl.BlockSpec((1, tm, cw), lambda bb, i: (bb, i, U_CG // cw)),
            pl.BlockSpec((1, CONV_HALO, cw), halo_map(U_CA // cw)),
            pl.BlockSpec((1, CONV_HALO, cw), halo_map(U_CG // cw)),
            pl.BlockSpec((CONV_WIDTH, cw), row),
            pl.BlockSpec((1, cw), row),
            pl.BlockSpec((1, cw), row),
            pl.BlockSpec((1, cw), row),
            pl.BlockSpec((cw, cw), row),
            pl.BlockSpec((1, cw), row),
        ],
        out_specs=pl.BlockSpec((1, tm, cw), lambda bb, i: (bb, i, 0)),
        out_shape=jax.ShapeDtypeStruct((b, t, cw), F32),
        scratch_shapes=[pltpu.VMEM((tm + CONV_HALO, cw), F32)],
        compiler_params=_cparams("parallel", "parallel"),
        name="conformer_conv",
    )(u3, u3, u3, u3, dw_w, dw_b.reshape(1, cw), ln_g.reshape(1, cw), ln_b.reshape(1, cw),
      pw_w.astype(BF16), pw_b.reshape(1, cw))


def _head_norm_rope(x, gain, cos, sin_signed, seg_mean):
    ms = _split_dot(x * x, seg_mean)
    y = x * lax.rsqrt(ms + EPS) * gain
    lane = _lane_iota()
    half = HEAD_DIM // 2
    first = (lane & (HEAD_DIM - 1)) < half
    partner = jnp.where(first, pltpu.roll(y, LANES - half, 1), pltpu.roll(y, half, 1))
    return y * cos + partner * sin_signed


def _dup(x, g):
    lane = _lane_iota()
    sw = pltpu.roll(x, HEAD_DIM, 1)
    if g == 0:
        return jnp.where(lane < HEAD_DIM, x, sw)
    return jnp.where(lane < HEAD_DIM, sw, x)


def _nsa_prep_kernel(q_ref, kv_ref, gk_ref, cos_ref, sin_ref, qg_ref, kg_ref,
                     qn_ref, kc_ref, kse_ref, kso_ref, vs_ref, kw_ref, vw_ref, gt_ref, *, per_seq):
    cos = cos_ref[...]
    sin = sin_ref[...]
    seg = lax.broadcasted_iota(jnp.int32, (LANES, LANES), 0) >> HEAD_SHIFT
    seg_mean = jnp.where(seg == lax.broadcasted_iota(jnp.int32, (LANES, LANES), 1) >> HEAD_SHIFT,
                         1.0 / HEAD_DIM, 0.0).astype(BF16)
    scale = HEAD_DIM ** -0.5
    for p in range(GROUP_WIDTH // LANES):
        x = q_ref[:, p * LANES:(p + 1) * LANES]
        qn_ref[:, p * LANES:(p + 1) * LANES] = (
            _head_norm_rope(x, qg_ref[...], cos, sin, seg_mean) * scale).astype(BF16)

    def blk(n):
        return kv_ref[:, n * LANES:(n + 1) * LANES]

    kc_ref[...] = _head_norm_rope(blk(0), kg_ref[0:1, :], cos, sin, seg_mean)
    ks = _head_norm_rope(blk(2), kg_ref[1:2, :], cos, sin, seg_mean)
    kw = _head_norm_rope(blk(4), kg_ref[2:3, :], cos, sin, seg_mean)
    vs = blk(3)
    vw = blk(5)
    tm = ks.shape[0]
    lane = _lane_iota()
    t = (pl.program_id(0) % per_seq) * tm + lax.broadcasted_iota(jnp.int32, (tm, 1), 0)
    sblk = t >> SEL_SHIFT
    hot_lo = jnp.where(lane == sblk, 1.0, 0.0)
    hot_hi = jnp.where(lane - HEAD_DIM == sblk, 1.0, 0.0)
    ks_sw = pltpu.roll(ks, HEAD_DIM, 1)
    for g in range(2):
        sl = slice(g * LANES, (g + 1) * LANES)
        kse_ref[:, sl] = jnp.where(lane < HEAD_DIM, ks if g == 0 else ks_sw, hot_hi).astype(BF16)
        kso_ref[:, sl] = jnp.where(lane >= HEAD_DIM, ks_sw if g == 0 else ks, hot_lo).astype(BF16)
        vs_ref[:, sl] = _dup(vs, g).astype(BF16)
        kw_ref[:, sl] = _dup(kw, g).astype(BF16)
        vw_ref[:, sl] = _dup(vw, g).astype(BF16)
    gates = jax.nn.sigmoid(gk_ref[...])
    gt_ref[:, 0:LANES] = gates
    gt_ref[:, LANES:2 * LANES] = pltpu.roll(gates, LANES - 12, 1)


def _nsa_prep(u2, cos, sin, q_gain, k_gain, seq):
    m_rows = u2.shape[0]
    tm = 512
    per_seq = seq // tm
    rowmap = lambda i: (i, 0)
    tab = lambda i: (i % per_seq, 0)
    const = lambda i: (0, 0)
    outs = [
        jax.ShapeDtypeStruct((m_rows, GROUP_WIDTH), BF16),
        jax.ShapeDtypeStruct((m_rows, LANES), F32),
        jax.ShapeDtypeStruct((m_rows, 2 * LANES), BF16),
        jax.ShapeDtypeStruct((m_rows, 2 * LANES), BF16),
        jax.ShapeDtypeStruct((m_rows, 2 * LANES), BF16),
        jax.ShapeDtypeStruct((m_rows, 2 * LANES), BF16),
        jax.ShapeDtypeStruct((m_rows, 2 * LANES), BF16),
        jax.ShapeDtypeStruct((m_rows, 2 * LANES), F32),
    ]
    return pl.pallas_call(
        functools.partial(_nsa_prep_kernel, per_seq=per_seq),
        grid=(m_rows // tm,),
        in_specs=[
            pl.BlockSpec((tm, GROUP_WIDTH), lambda i: (i, U_NQ // GROUP_WIDTH)),
            pl.BlockSpec((tm, 6 * LANES), lambda i: (i, U_NKV // (6 * LANES))),
            pl.BlockSpec((tm, LANES), lambda i: (i, (U_MLA + MLA_Q_RANK + MLA_KV_RANK) // LANES)),
            pl.BlockSpec((tm, LANES), tab),
            pl.BlockSpec((tm, LANES), tab),
            pl.BlockSpec((1, LANES), const),
            pl.BlockSpec((3, LANES), const),
        ],
        out_specs=[
            pl.BlockSpec((tm, GROUP_WIDTH), rowmap),
            pl.BlockSpec((tm, LANES), rowmap),
            pl.BlockSpec((tm, 2 * LANES), rowmap),
            pl.BlockSpec((tm, 2 * LANES), rowmap),
            pl.BlockSpec((tm, 2 * LANES), rowmap),
            pl.BlockSpec((tm, 2 * LANES), rowmap),
            pl.BlockSpec((tm, 2 * LANES), rowmap),
            pl.BlockSpec((tm, 2 * LANES), rowmap),
        ],
        out_shape=outs,
        compiler_params=_cparams("parallel"),
        name="nsa_prep",
    )(u2, u2, u2, cos, sin, q_gain, k_gain)


def _compress_kernel(xk_ref, xv_ref, pe_ref, wk_ref, wv_ref, kc_ref, vc_ref):
    nrow = xk_ref.shape[2]

    def comp(x, pe_a, pe_b, w_ref):
        a = _dot((x + pe_a).astype(BF16), w_ref[0])
        bm = _dot((x + pe_b).astype(BF16), w_ref[1])
        return a + pltpu.roll(bm, nrow - 1, 0)

    kc_ref[0, 0] = comp(xk_ref[0, 0], pe_ref[0:1, :], pe_ref[1:2, :], wk_ref).astype(BF16)
    vc_ref[0, 0] = comp(xv_ref[0, 0], pe_ref[2:3, :], pe_ref[3:4, :], wv_ref).astype(BF16)


def _compress(xk, xv, pe, wk, wv):
    b, g, nrow, wide = xk.shape
    xmap = lambda bb, gg: (bb, gg, 0, 0)
    out = jax.ShapeDtypeStruct((b, g, nrow, LANES), BF16)
    return pl.pallas_call(
        _compress_kernel,
        grid=(b, g),
        in_specs=[
            pl.BlockSpec((1, 1, nrow, wide), xmap),
            pl.BlockSpec((1, 1, nrow, wide), xmap),
            pl.BlockSpec((4, wide), lambda bb, gg: (0, 0)),
            pl.BlockSpec((2, wide, LANES), lambda bb, gg: (0, 0, 0)),
            pl.BlockSpec((2, wide, LANES), lambda bb, gg: (0, 0, 0)),
        ],
        out_specs=[pl.BlockSpec((1, 1, nrow, LANES), xmap), pl.BlockSpec((1, 1, nrow, LANES), xmap)],
        out_shape=[out, out],
        compiler_params=_cparams("parallel", "parallel"),
        name="nsa_compress",
    )(xk, xv, pe, wk, wv)


def _attend(q, k, v, s_ref, p_ref, masks, scale=None):
    n = k.shape[0] // TK
    s_ref[:, :n * TK] = _dot_nt(q, k)
    mfold = None
    for j in range(n):
        sl = slice(j * TK, (j + 1) * TK)
        s = s_ref[:, sl]
        if j in masks:
            s = jnp.where(masks[j], s, NEG)
            s_ref[:, sl] = s
        f = jnp.maximum(s[:, :LANES], s[:, LANES:])
        mfold = f if mfold is None else jnp.maximum(mfold, f)
    m = jnp.max(mfold, axis=-1, keepdims=True)
    lfold = None
    for j in range(n):
        sl = slice(j * TK, (j + 1) * TK)
        x = s_ref[:, sl] - m
        if scale is not None:
            x = x * scale
        p = jnp.exp(x)
        f = p[:, :LANES] + p[:, LANES:]
        lfold = f if lfold is None else lfold + f
        p_ref[:, sl] = p.astype(BF16)
    l = jnp.sum(lfold, axis=-1, keepdims=True)
    return _dot(p_ref[:, :n * TK], v) / l


def _nsa_kernel(q_ref, kc_ref, vc_ref, kse_ref, kso_ref, vs_ref, kw_ref, vw_ref, gt_ref, ov_ref,
                o_ref, s_ref, p_ref, sw_ref, pw_ref, *, nq, n_cmp, n_sblk, n_sel):
    i = pl.program_id(2)
    lane = _lane_iota()
    row = i * TQ + lax.broadcasted_iota(jnp.int32, (TQ, 1), 0)
    rep = 4
    q2s = [q_ref[0, :, p * LANES:(p + 1) * LANES] for p in range(2)]
    qs = []
    for q2 in q2s:
        qs.append(jnp.where(lane < HEAD_DIM, q2, jnp.zeros_like(q2)))
        qs.append(jnp.where(lane >= HEAD_DIM, q2, jnp.zeros_like(q2)))

    kc = kc_ref[0, 0]
    vc = vc_ref[0, 0]
    cmask = ((lane * NSA_CMP_STRIDE + (NSA_CMP_LEN - 1)) <= row) & (lane < n_cmp)
    psum = jnp.zeros((TQ, LANES), F32)
    o_cmp = []
    for r in range(rep):
        s = jnp.where(cmask, _dot_nt(qs[r], kc), NEG)
        mx = jnp.max(s, axis=-1, keepdims=True)
        p = jnp.where(cmask, jnp.exp(s - mx), 0.0)
        p = p / jnp.maximum(jnp.sum(p, axis=-1, keepdims=True), 1e-30)
        o_cmp.append(_dot(p.astype(BF16), vc))
        psum = psum + p
    ov_t = ov_ref[0:n_sblk, :]
    p_hi = psum.astype(BF16)
    p_lo = (psum - p_hi.astype(F32)).astype(BF16)
    imp = _dot_nt(ov_t, p_hi) + _dot_nt(ov_t, p_lo)
    cur = (i * TQ + _lane_iota(TQ)) >> SEL_SHIFT
    blk = lax.broadcasted_iota(jnp.int32, (n_sblk, 1), 0)
    eligible = blk <= cur
    forced = (blk == 0) | (blk == cur) | (blk == cur - 1)
    score = jnp.where(eligible, imp + jnp.where(forced, NSA_FORCE_BONUS, 0.0), NEG)
    rank = jnp.zeros((n_sblk, TQ), F32)
    for sp in range(n_sblk):
        other = score[sp:sp + 1, :]
        beats = (other > score) | ((other == score) & (blk > sp))
        rank = rank + jnp.where(beats, 1.0, 0.0)
    bias_t = jnp.where(eligible & (rank < n_sel), 0.0, NEG)

    bias_lo = jnp.concatenate([bias_t, jnp.zeros((LANES - n_sblk, TQ), F32)], axis=0).T
    bias_hi = pltpu.roll(bias_lo, HEAD_DIM, 1).astype(BF16)
    bias_lo = bias_lo.astype(BF16)
    q_aug = []
    for q2 in q2s:
        q_aug.append(jnp.where(lane < HEAD_DIM, q2, bias_hi))
        q_aug.append(jnp.where(lane >= HEAD_DIM, q2, bias_lo))
    causal, strict = _diag_masks()
    wtiles = NSA_WINDOW // TK
    gt = gt_ref[0]

    for c in range(nq):
        @pl.when(i == c)
        def _():
            kv_len = (c + 1) * TK
            lo = max(c - wtiles, 0) * TK
            wmasks = {c - max(c - wtiles, 0): causal}
            if c >= wtiles:
                wmasks[0] = jnp.logical_not(causal)
            outs = []
            for r in range(rep):
                ks_ref = kse_ref if r % 2 == 0 else kso_ref
                o_slc = _attend(q_aug[r], ks_ref[0, :kv_len, :], vs_ref[0, :kv_len, :],
                                s_ref.at[r], p_ref.at[r], {c: causal})
                o_win = _attend(qs[r], kw_ref[0, lo:kv_len, :], vw_ref[0, lo:kv_len, :],
                                sw_ref.at[r], pw_ref.at[r], wmasks)
                outs.append(gt[:, 3 * r:3 * r + 1] * o_cmp[r] + gt[:, 3 * r + 1:3 * r + 2] * o_slc
                            + gt[:, 3 * r + 2:3 * r + 3] * o_win)
            o_ref[0, :, 0:LANES] = jnp.where(lane < HEAD_DIM, outs[0], outs[1])
            o_ref[0, :, LANES:2 * LANES] = jnp.where(lane < HEAD_DIM, outs[2], outs[3])


def _nsa_attention(qn, kcmp, vcmp, kse, kso, vsd, kwd, vwd, gates, overlap):
    b, t, _ = qn.shape
    nq = t // TQ
    n_cmp = (t - NSA_CMP_LEN) // NSA_CMP_STRIDE + 1
    n_sblk = t // NSA_SEL_LEN
    nrow = kcmp.shape[2]
    wlen = NSA_WINDOW + TK
    kv = lambda bb, g, i: (bb, 0, g)
    cm = lambda bb, g, i: (bb, g, 0, 0)
    return pl.pallas_call(
        functools.partial(_nsa_kernel, nq=nq, n_cmp=n_cmp, n_sblk=n_sblk, n_sel=min(NSA_N_SEL, n_sblk)),
        grid=(b, 2, nq),
        in_specs=[
            pl.BlockSpec((1, TQ, 2 * LANES), lambda bb, g, i: (bb, i, g)),
            pl.BlockSpec((1, 1, nrow, LANES), cm),
            pl.BlockSpec((1, 1, nrow, LANES), cm),
            pl.BlockSpec((1, t, LANES), kv),
            pl.BlockSpec((1, t, LANES), kv),
            pl.BlockSpec((1, t, LANES), kv),
            pl.BlockSpec((1, t, LANES), kv),
            pl.BlockSpec((1, t, LANES), kv),
            pl.BlockSpec((1, TQ, LANES), lambda bb, g, i: (bb, i, g)),
            pl.BlockSpec((LANES, LANES), lambda bb, g, i: (0, 0)),
        ],
        out_specs=pl.BlockSpec((1, TQ, 2 * LANES), lambda bb, g, i: (bb, i, g)),
        out_shape=jax.ShapeDtypeStruct((b, t, GROUP_WIDTH), F32),
        scratch_shapes=[pltpu.VMEM((4, TQ, t), F32), pltpu.VMEM((4, TQ, t), BF16),
                        pltpu.VMEM((4, TQ, wlen), F32), pltpu.VMEM((4, TQ, wlen), BF16)],
        compiler_params=_cparams("parallel", "parallel", "arbitrary"),
        name="nsa_attention",
    )(qn, kcmp, vcmp, kse, kso, vsd, kwd, vwd, gates, overlap)


def _mla_prep_kernel(u_ref, cos_ref, sin_ref, qlg_ref, kvlg_ref, wq_ref, wk_ref, wv_ref, qg_ref, kg_ref,
                     q_ref, k_ref, v_ref):
    lane = _lane_iota()
    cos = cos_ref[...]
    sin = sin_ref[...]
    half = MLA_ROPE // 2

    def lat_norm(x, g):
        ms = jnp.mean(x * x, axis=-1, keepdims=True)
        return (x * lax.rsqrt(ms + EPS) * g).astype(BF16)

    def head_norm_rope(x, g):
        ms = jnp.sum(x * x, axis=-1, keepdims=True) * (1.0 / MLA_QK)
        y = x * lax.rsqrt(ms + EPS) * g
        partner = jnp.where(lane < MLA_NOPE + half, pltpu.roll(y, LANES - half, 1), pltpu.roll(y, half, 1))
        return y * cos + partner * sin

    q = _dot(lat_norm(u_ref[:, 0:MLA_Q_RANK], qlg_ref[...]), wq_ref[...])
    kvn = lat_norm(u_ref[:, MLA_Q_RANK:MLA_Q_RANK + MLA_KV_RANK], kvlg_ref[...])
    kn = _dot(kvn, wk_ref[...])
    v_ref[...] = _dot(kvn, wv_ref[...]).astype(BF16)
    gk = u_ref[:, MLA_Q_RANK + MLA_KV_RANK:MLA_Q_RANK + MLA_KV_RANK + LANES]
    k_rope = jnp.where((lane >= MLA_NOPE) & (lane < MLA_QK), gk, 0.0)
    for h in range(MLA_HEADS):
        sl = slice(h * LANES, (h + 1) * LANES)
        q_ref[:, sl] = head_norm_rope(q[:, sl], qg_ref[...]).astype(BF16)
        k_ref[:, sl] = head_norm_rope(kn[:, sl] + k_rope, kg_ref[...]).astype(BF16)


def _mla_prep(u2, cos, sin, q_lat_gain, kv_lat_gain, wq, wk, wv, q_gain, k_gain, seq):
    m_rows = u2.shape[0]
    tm = 512
    per_seq = seq // tm
    width = MLA_Q_RANK + MLA_KV_RANK + LANES
    rowmap = lambda i: (i, 0)
    tab = lambda i: (i % per_seq, 0)
    const = lambda i: (0, 0)
    return pl.pallas_call(
        _mla_prep_kernel,
        grid=(m_rows // tm,),
        in_specs=[
            pl.BlockSpec((tm, width), lambda i: (i, U_MLA // width)),
            pl.BlockSpec((tm, LANES), tab),
            pl.BlockSpec((tm, LANES), tab),
            pl.BlockSpec((1, MLA_Q_RANK), const),
            pl.BlockSpec((1, MLA_KV_RANK), const),
            pl.BlockSpec(wq.shape, const),
            pl.BlockSpec(wk.shape, const),
            pl.BlockSpec(wv.shape, const),
            pl.BlockSpec((1, LANES), const),
            pl.BlockSpec((1, LANES), const),
        ],
        out_specs=[
            pl.BlockSpec((tm, MLA_HEADS * LANES), rowmap),
            pl.BlockSpec((tm, MLA_HEADS * LANES), rowmap),
            pl.BlockSpec((tm, GROUP_WIDTH), rowmap),
        ],
        out_shape=[
            jax.ShapeDtypeStruct((m_rows, MLA_HEADS * LANES), BF16),
            jax.ShapeDtypeStruct((m_rows, MLA_HEADS * LANES), BF16),
            jax.ShapeDtypeStruct((m_rows, GROUP_WIDTH), BF16),
        ],
        compiler_params=_cparams("parallel"),
        name="mla_prep",
    )(u2, cos, sin, q_lat_gain, kv_lat_gain, wq, wk, wv, q_gain, k_gain)


def _mla_kernel(q_ref, k_ref, v_ref, o_ref, s_ref, p_ref, *, nq):
    i = pl.program_id(2)
    lane = _lane_iota()
    causal, _ = _diag_masks()
    scale = MLA_QK ** -0.5
    for c in range(nq):
        @pl.when(i == c)
        def _():
            kv_len = (c + 1) * TK
            outs = []
            for hd in range(2):
                sl = slice(hd * LANES, (hd + 1) * LANES)
                outs.append(_attend(q_ref[0, :, sl], k_ref[0, :kv_len, sl], v_ref[0, :kv_len, :],
                                    s_ref.at[hd], p_ref.at[hd], {c: causal}, scale=scale))
            o_ref[0] = jnp.where(lane < HEAD_DIM, outs[0], outs[1])


def _mla_attention(qm, km, vm):
    b, t, _ = qm.shape
    nq = t // TQ
    return pl.pallas_call(
        functools.partial(_mla_kernel, nq=nq),
        grid=(b, MLA_HEADS // 2, nq),
        in_specs=[
            pl.BlockSpec((1, TQ, 2 * LANES), lambda bb, p, i: (bb, i, p)),
            pl.BlockSpec((1, t, 2 * LANES), lambda bb, p, i: (bb, 0, p)),
            pl.BlockSpec((1, t, LANES), lambda bb, p, i: (bb, 0, p)),
        ],
        out_specs=pl.BlockSpec((1, TQ, LANES), lambda bb, p, i: (bb, i, p)),
        out_shape=jax.ShapeDtypeStruct((b, t, GROUP_WIDTH), F32),
        scratch_shapes=[pltpu.VMEM((2, TQ, t), F32), pltpu.VMEM((2, TQ, t), BF16)],
        compiler_params=_cparams("parallel", "parallel", "arbitrary"),
        name="mla_attention",
    )(qm, km, vm)


def _out_kernel(a_ref, b_ref, c_ref, d_ref, gn_ref, w_ref, x_ref, mod_ref, o_ref):
    acc = None
    for gi, r in enumerate((a_ref, b_ref, c_ref, d_ref)):
        y = r[...]
        ms = jnp.mean(y * y, axis=-1, keepdims=True)
        sl = slice(gi * GROUP_WIDTH, (gi + 1) * GROUP_WIDTH)
        part = _dot((y * lax.rsqrt(ms + EPS) * gn_ref[:, sl]).astype(BF16), w_ref[sl, :])
        acc = part if acc is None else acc + part
    o_ref[...] = x_ref[...] + mod_ref[0, 2:3, :] * acc


def _out_proj(parts, gn, w, x2, mod, seq):
    m_rows, d = x2.shape
    tm = 512
    per_seq = seq // tm
    part = pl.BlockSpec((tm, GROUP_WIDTH), lambda i: (i, 0))
    return pl.pallas_call(
        _out_kernel,
        grid=(m_rows // tm,),
        in_specs=[
            part, part, part, part,
            pl.BlockSpec((1, 4 * GROUP_WIDTH), lambda i: (0, 0)),
            pl.BlockSpec((4 * GROUP_WIDTH, d), lambda i: (0, 0)),
            pl.BlockSpec((tm, d), lambda i: (i, 0)),
            pl.BlockSpec((1, 6, d), lambda i: (i // per_seq, 0, 0)),
        ],
        out_specs=pl.BlockSpec((tm, d), lambda i: (i, 0)),
        out_shape=jax.ShapeDtypeStruct((m_rows, d), F32),
        compiler_params=_cparams("parallel"),
        name="out_proj",
    )(*parts, gn, w, x2, mod)


FFN_HALO = 16


def _ffn_kernel(x_ref, xh_ref, mod_ref, g_ref, wa_ref, wg_ref, cwa_ref, cwg_ref, cba_ref, cbg_ref, wd_ref,
                o_ref, h_ref, acc_ref, *, per_seq):
    i = pl.program_id(0)
    j = pl.program_id(1)
    m = mod_ref[0]

    @pl.when(j == 0)
    def _():
        h_ref[FFN_HALO:, :] = _modulated_norm(x_ref[...], g_ref[...], m[3:4], m[4:5]).astype(BF16)
        halo = _modulated_norm(xh_ref[...], g_ref[...], m[3:4], m[4:5])
        h_ref[0:FFN_HALO, :] = jnp.where(i % per_seq == 0, 0.0, halo).astype(BF16)
        acc_ref[...] = jnp.zeros_like(acc_ref)

    h = h_ref[...]

    def conv(u, cw_ref, cb_ref):
        y = cw_ref[2:3, :] * u + cw_ref[1:2, :] * pltpu.roll(u, 1, 0) + cw_ref[0:1, :] * pltpu.roll(u, 2, 0)
        return y[FFN_HALO:, :] + cb_ref[...]

    ya = conv(_dot(h, wa_ref[...]), cwa_ref, cba_ref)
    yg = conv(_dot(h, wg_ref[...]), cwg_ref, cbg_ref)
    act = (yg * jax.nn.sigmoid(yg)) * ya
    acc_ref[...] += _dot(act.astype(BF16), wd_ref[...])

    @pl.when(j == pl.num_programs(1) - 1)
    def _():
        o_ref[...] = x_ref[...] + m[5:6] * acc_ref[...]


def _ffn(x2, mod, g, w_up, conv_w, conv_b, w_down, seq):
    m_rows, d = x2.shape
    d_ff = w_down.shape[0]
    tm, tn = 512, 512
    per_seq = seq // tm
    nj = d_ff // tn
    hb = tm // FFN_HALO
    cb = conv_b.reshape(1, 2 * d_ff)
    return pl.pallas_call(
        functools.partial(_ffn_kernel, per_seq=per_seq),
        grid=(m_rows // tm, nj),
        in_specs=[
            pl.BlockSpec((tm, d), lambda i, j: (i, 0)),
            pl.BlockSpec((FFN_HALO, d), lambda i, j: (jnp.maximum(i * hb - 1, 0), 0)),
            pl.BlockSpec((1, 6, d), lambda i, j: (i // per_seq, 0, 0)),
            pl.BlockSpec((1, d), lambda i, j: (0, 0)),
            pl.BlockSpec((d, tn), lambda i, j: (0, j)),
            pl.BlockSpec((d, tn), lambda i, j: (0, nj + j)),
            pl.BlockSpec((3, tn), lambda i, j: (0, j)),
            pl.BlockSpec((3, tn), lambda i, j: (0, nj + j)),
            pl.BlockSpec((1, tn), lambda i, j: (0, j)),
            pl.BlockSpec((1, tn), lambda i, j: (0, nj + j)),
            pl.BlockSpec((tn, d), lambda i, j: (j, 0)),
        ],
        out_specs=pl.BlockSpec((tm, d), lambda i, j: (i, 0)),
        out_shape=jax.ShapeDtypeStruct((m_rows, d), F32),
        scratch_shapes=[pltpu.VMEM((tm + FFN_HALO, d), BF16), pltpu.VMEM((tm, d), F32)],
        compiler_params=_cparams("parallel", "arbitrary"),
        name="conv_glu_ffn",
    )(x2, x2, mod, g, w_up, w_up, conv_w, conv_w, cb, cb, w_down)


def _reorder_w_in(w):
    d = w.shape[0]
    n_gate = 24
    off_mla = REF_ALIGNED_COLS + n_gate
    lat = MLA_Q_RANK + MLA_KV_RANK
    z = lambda n: jnp.zeros((d, n), w.dtype)
    out = jnp.concatenate([
        w[:, :REF_ALIGNED_COLS], w[:, off_mla:off_mla + lat],
        w[:, REF_ALIGNED_COLS:off_mla], z(MLA_NOPE - n_gate),
        w[:, off_mla + lat:off_mla + lat + MLA_ROPE], z(LANES - MLA_QK)], axis=1)
    assert out.shape[1] == U_COLS
    return out.astype(BF16)


def _rope_tables(seq, dim, lane_cos, lane_sin):
    inv = ROPE_THETA ** (-jnp.arange(0, dim, 2, dtype=F32) / dim)
    ang = jnp.arange(seq).astype(F32)[:, None] * inv[None, :]
    cos, sin = jnp.cos(ang), jnp.sin(ang)
    return lane_cos(cos), lane_sin(sin)


def _nsa_tables(seq):
    return _rope_tables(seq, HEAD_DIM,
                        lambda c: jnp.concatenate([c, c, c, c], axis=1),
                        lambda s: jnp.concatenate([-s, s, -s, s], axis=1))


def _mla_tables(seq):
    ones = jnp.ones((seq, MLA_NOPE), F32)
    zeros = jnp.zeros((seq, MLA_NOPE), F32)
    pad1 = jnp.ones((seq, LANES - MLA_QK), F32)
    pad0 = jnp.zeros((seq, LANES - MLA_QK), F32)
    return _rope_tables(seq, MLA_ROPE,
                        lambda c: jnp.concatenate([ones, c, c, pad1], axis=1),
                        lambda s: jnp.concatenate([zeros, -s, s, pad0], axis=1))


def _overlap_matrix(seq):
    n_cmp = (seq - NSA_CMP_LEN) // NSA_CMP_STRIDE + 1
    n_sblk = seq // NSA_SEL_LEN
    starts = np.arange(n_cmp) * NSA_CMP_STRIDE
    sel_start = np.arange(n_sblk) * NSA_SEL_LEN
    ov = np.clip(np.minimum(starts[:, None] + NSA_CMP_LEN, sel_start[None, :] + NSA_SEL_LEN)
                 - np.maximum(starts[:, None], sel_start[None, :]), 0, None) / NSA_CMP_LEN
    full = np.zeros((LANES, LANES), np.float32)
    full[:n_sblk, :n_cmp] = ov.T
    return jnp.asarray(full, BF16)


def _pad_lanes(v, n):
    return jnp.pad(v, (0, n - v.shape[0]))


def _mixer(x2, mod, batch, seq, p):
    m_rows = x2.shape[0]
    usb, u2 = _in_proj(x2, mod, p["norm_mix"].reshape(1, -1), _reorder_w_in(p["w_in"]), seq)
    u3 = u2.reshape(batch, seq, U_COLS - U_SB_COLS)

    o_sb = _sb_attention(usb.reshape(batch, seq, U_SB_COLS))
    o_conv = _conformer(u3, p["conv_dw_w"], p["conv_dw_b"], p["conv_ln_g"], p["conv_ln_b"],
                        p["conv_pw_w"], p["conv_pw_b"])

    cos_n, sin_n = _nsa_tables(seq)
    q_gain = jnp.tile(p["nsa_q_norm"], 2).reshape(1, LANES)
    k_gain = jnp.tile(p["nsa_k_norm"], (1, 2))
    qn, kc, kse, kso, vsd, kwd, vwd, gates = _nsa_prep(u2, cos_n, sin_n, q_gain, k_gain, seq)
    nrow = seq // NSA_CMP_STRIDE

    def blocks(t2):
        t5 = t2.reshape(batch, nrow, NSA_CMP_STRIDE, 2, HEAD_DIM)
        return t5.transpose(0, 3, 1, 2, 4).reshape(batch, 2, nrow, NSA_CMP_STRIDE * HEAD_DIM)

    vc_raw = u2[:, U_NKV + LANES:U_NKV + 2 * LANES]
    half = NSA_CMP_STRIDE * HEAD_DIM
    pe = p["nsa_cmp_pe"].reshape(2, 2, half).reshape(4, half)
    wdup = jnp.concatenate([p["nsa_cmp_w"], p["nsa_cmp_w"]], axis=-1).astype(BF16)
    wk = wdup[0].reshape(2, half, LANES)
    wv = wdup[1].reshape(2, half, LANES)
    kcmp, vcmp = _compress(blocks(kc), blocks(vc_raw), pe, wk, wv)
    o_nsa = _nsa_attention(qn.reshape(batch, seq, -1), kcmp, vcmp,
                           kse.reshape(batch, seq, -1), kso.reshape(batch, seq, -1), vsd.reshape(batch, seq, -1),
                           kwd.reshape(batch, seq, -1), vwd.reshape(batch, seq, -1),
                           gates.reshape(batch, seq, -1), _overlap_matrix(seq))

    cos_m, sin_m = _mla_tables(seq)
    w_uq = p["mla_w_uq"].reshape(MLA_Q_RANK, MLA_HEADS, MLA_QK)
    wq = jnp.pad(w_uq, ((0, 0), (0, 0), (0, LANES - MLA_QK))).reshape(MLA_Q_RANK, MLA_HEADS * LANES).astype(BF16)
    w_ukv = p["mla_w_ukv"].reshape(MLA_KV_RANK, MLA_HEADS, 2 * HEAD_DIM)
    wk_m = jnp.pad(w_ukv[:, :, :MLA_NOPE], ((0, 0), (0, 0), (0, LANES - MLA_NOPE)))
    wk_m = wk_m.reshape(MLA_KV_RANK, MLA_HEADS * LANES).astype(BF16)
    wv_m = w_ukv[:, :, MLA_NOPE:].reshape(MLA_KV_RANK, GROUP_WIDTH).astype(BF16)
    qm, km, vm = _mla_prep(u2, cos_m, sin_m, p["mla_q_lat_norm"].reshape(1, -1),
                           p["mla_kv_lat_norm"].reshape(1, -1), wq, wk_m, wv_m,
                           _pad_lanes(p["mla_q_norm"], LANES).reshape(1, LANES),
                           _pad_lanes(p["mla_k_norm"], LANES).reshape(1, LANES), seq)
    o_mla = _mla_attention(qm.reshape(batch, seq, -1), km.reshape(batch, seq, -1), vm.reshape(batch, seq, -1))

    parts = [o.reshape(m_rows, GROUP_WIDTH) for o in (o_sb, o_conv, o_nsa, o_mla)]
    return _out_proj(parts, p["group_norm"].reshape(1, -1), p["w_o"].astype(BF16), x2, mod, seq)


def kernel(x, c, ada_w, ada_b, norm_mix, norm_ffn, w_in, conv_dw_w, conv_dw_b, conv_ln_g, conv_ln_b, conv_pw_w, conv_pw_b, nsa_q_norm, nsa_k_norm, nsa_cmp_pe, nsa_cmp_w, mla_q_lat_norm, mla_kv_lat_norm, mla_w_uq, mla_w_ukv, mla_q_norm, mla_k_norm, group_norm, w_o, ffn_up, ffn_conv_w, ffn_conv_b, ffn_down):
    batch, seq, d = x.shape
    depth = ada_w.shape[0]
    per_layer = dict(
        norm_mix=norm_mix, w_in=w_in, conv_dw_w=conv_dw_w, conv_dw_b=conv_dw_b, conv_ln_g=conv_ln_g,
        conv_ln_b=conv_ln_b, conv_pw_w=conv_pw_w, conv_pw_b=conv_pw_b, nsa_q_norm=nsa_q_norm,
        nsa_k_norm=nsa_k_norm, nsa_cmp_pe=nsa_cmp_pe, nsa_cmp_w=nsa_cmp_w, mla_q_lat_norm=mla_q_lat_norm,
        mla_kv_lat_norm=mla_kv_lat_norm, mla_w_uq=mla_w_uq, mla_w_ukv=mla_w_ukv, mla_q_norm=mla_q_norm,
        mla_k_norm=mla_k_norm, group_norm=group_norm, w_o=w_o)
    mods = _ada(c, ada_w, ada_b)
    x2 = x.reshape(batch * seq, d)
    for l in range(depth):
        p = {k: v[l] for k, v in per_layer.items()}
        x2 = _mixer(x2, mods[l], batch, seq, p)
        x2 = _ffn(x2, mods[l], norm_ffn[l].reshape(1, -1), ffn_up[l].astype(BF16), ffn_conv_w[l],
                  ffn_conv_b[l], ffn_down[l].astype(BF16), seq)
    return x2.reshape(batch, seq, d)
```

```python
import functools

import numpy as np
import jax
import jax.numpy as jnp
from jax import lax
from jax.experimental import pallas as pl
from jax.experimental.pallas import tpu as pltpu

F32 = jnp.float32
BF16 = jnp.bfloat16

LANES = 128
VMEM_LIMIT = 56 * 1024 * 1024

HEAD_DIM = 64
HEAD_SHIFT = 6
ROPE_THETA = 10000.0
EPS = 1e-6
GROUP_WIDTH = 512
CONV_WIDTH = 31
CONV_LN_EPS = 1e-5
NSA_CMP_LEN = 32
NSA_CMP_STRIDE = 16
NSA_SEL_LEN = 64
SEL_SHIFT = 6
NSA_N_SEL = 16
NSA_WINDOW = 512
NSA_FORCE_BONUS = 1e3
MLA_Q_RANK = 384
MLA_KV_RANK = 256
MLA_NOPE = 64
MLA_ROPE = 32
MLA_QK = MLA_NOPE + MLA_ROPE
MLA_HEADS = 8

U_SBQ, U_SBK, U_SBV = 0, 512, 1024
U_SB_COLS = 1536
U_CA, U_CG = 0, 512
U_NQ = 1024
U_NKV = 1536
U_MLA = 2304
U_COLS = 4608
REF_ALIGNED_COLS = 3840

TQ = 256
TK = 256
NEG = -1e30
LOG2E = 1.4426950408889634


def _cparams(*sem):
    return pltpu.CompilerParams(dimension_semantics=sem, vmem_limit_bytes=VMEM_LIMIT)


def _dot(a, b):
    return jnp.dot(a, b, preferred_element_type=F32)


def _dot_nt(a, b):
    return lax.dot_general(a, b, (((1,), (1,)), ((), ())), preferred_element_type=F32)


def _split_dot(x, w):
    hi = x.astype(BF16)
    lo = (x - hi.astype(F32)).astype(BF16)
    return _dot(hi, w) + _dot(lo, w)


def _lane_iota(n=LANES):
    return lax.broadcasted_iota(jnp.int32, (1, n), 1)


def _ada_kernel(c_ref, w_ref, b_ref, o_ref):
    c = c_ref[...]
    cond = (c * jax.nn.sigmoid(c)).astype(BF16)
    o_ref[0] = _dot(cond, w_ref[0].astype(BF16)) + b_ref[0]


def _ada(c, ada_w, ada_b):
    n_layers, d, n = ada_w.shape
    b = c.shape[0]
    rows = 8
    tn = 1024
    cpad = jnp.pad(c, ((0, rows - b), (0, 0)))
    out = pl.pallas_call(
        _ada_kernel,
        grid=(n_layers, n // tn),
        in_specs=[
            pl.BlockSpec((rows, d), lambda l, j: (0, 0)),
            pl.BlockSpec((1, d, tn), lambda l, j: (l, 0, j)),
            pl.BlockSpec((1, 1, tn), lambda l, j: (l, 0, j)),
        ],
        out_specs=pl.BlockSpec((1, rows, tn), lambda l, j: (l, 0, j)),
        out_shape=jax.ShapeDtypeStruct((n_layers, rows, n), F32),
        compiler_params=_cparams("parallel", "parallel"),
        name="ada_mod",
    )(cpad, ada_w, ada_b.reshape(n_layers, 1, n))
    return out[:, :b].reshape(n_layers, b, 6, d)


def _modulated_norm(x, g, shift, scale):
    ms = jnp.mean(x * x, axis=-1, keepdims=True)
    return x * lax.rsqrt(ms + EPS) * g * (1.0 + scale) + shift


def _in_kernel(x_ref, mod_ref, g_ref, w_ref, sb_ref, o_ref, h_ref, *, n_sb):
    j = pl.program_id(1)

    @pl.when(j == 0)
    def _():
        m = mod_ref[0]
        h_ref[...] = _modulated_norm(x_ref[...], g_ref[...], m[0:1], m[1:2]).astype(BF16)

    @pl.when(j < n_sb)
    def _():
        sb_ref[...] = _dot(h_ref[...], w_ref[...]).astype(BF16)

    @pl.when(j >= n_sb)
    def _():
        o_ref[...] = _dot(h_ref[...], w_ref[...])


def _in_proj(x2, mod, g, w, layer, seq):
    m_rows, d = x2.shape
    n = w.shape[2]
    tm, tn = 1024, 768
    per_seq = seq // tm
    n_sb = U_SB_COLS // tn
    return pl.pallas_call(
        functools.partial(_in_kernel, n_sb=n_sb),
        grid=(m_rows // tm, n // tn),
        in_specs=[
            pl.BlockSpec((tm, d), lambda i, j: (i, 0)),
            pl.BlockSpec((1, 6, d), lambda i, j: (i // per_seq, 0, 0)),
            pl.BlockSpec((1, d), lambda i, j: (0, 0)),
            pl.BlockSpec((None, d, tn), lambda i, j: (layer, 0, j)),
        ],
        out_specs=[
            pl.BlockSpec((tm, tn), lambda i, j: (i, jnp.minimum(j, n_sb - 1))),
            pl.BlockSpec((tm, tn), lambda i, j: (i, jnp.maximum(j - n_sb, 0))),
        ],
        out_shape=[jax.ShapeDtypeStruct((m_rows, U_SB_COLS), BF16),
                   jax.ShapeDtypeStruct((m_rows, n - U_SB_COLS), F32)],
        scratch_shapes=[pltpu.VMEM((tm, d), BF16)],
        compiler_params=_cparams("parallel", "arbitrary"),
        name="in_proj",
    )(x2, mod, g, w)


def _diag_masks():
    r = lax.broadcasted_iota(jnp.int32, (TQ, TK), 0)
    c = lax.broadcasted_iota(jnp.int32, (TQ, TK), 1)
    return c <= r, c < r


def _sb_kernel(q_ref, k_ref, v_ref, o_ref, z_ref, a_ref, *, nq):
    i = pl.program_id(2)
    lane = _lane_iota()
    q2 = q_ref[0] * (HEAD_DIM ** -0.5)
    qs = (jnp.where(lane < HEAD_DIM, q2, jnp.zeros_like(q2)),
          jnp.where(lane >= HEAD_DIM, q2, jnp.zeros_like(q2)))
    tri = (lax.broadcasted_iota(jnp.int32, (TK, TK), 0)
           > lax.broadcasted_iota(jnp.int32, (TK, TK), 1)).astype(BF16)
    tri2 = jnp.concatenate([tri, tri], axis=0)
    _, strict = _diag_masks()

    for c in range(nq):
        @pl.when(i == c)
        def _():
            kv_len = (c + 1) * TK
            outs = []
            for hd in range(2):
                z_ref[hd, :, :kv_len] = _dot_nt(qs[hd], k_ref[0, :kv_len, :])
                carry = jnp.zeros((TQ, 1), F32)
                for j in reversed(range(c + 1)):
                    sl = slice(j * TK, (j + 1) * TK)
                    z = z_ref[hd, :, sl]
                    sp = jnp.log(1.0 + jnp.exp(-jnp.abs(z)))
                    log_beta = jnp.minimum(z, 0.0) - sp
                    neg_1m = jnp.maximum(z, 0.0) + sp
                    if j == c:
                        neg_1m = jnp.where(strict, neg_1m, 0.0)
                    hi = neg_1m.astype(BF16)
                    pieces = jnp.concatenate([hi, (neg_1m - hi.astype(F32)).astype(BF16)], axis=1)
                    later = _dot(pieces, tri2)
                    a = jnp.exp(log_beta - later - carry)
                    if j == c:
                        a = jnp.where(strict, a, 0.0)
                    a_ref[hd, :, sl] = a.astype(BF16)
                    carry = carry + jnp.sum(neg_1m, axis=-1, keepdims=True)
                outs.append(_dot(a_ref[hd, :, :kv_len], v_ref[0, :kv_len, :]))
            o_ref[0] = jnp.where(lane < HEAD_DIM, outs[0], outs[1])


def _sb_attention(usb3):
    b, t, _ = usb3.shape
    nq = t // TQ
    pairs = GROUP_WIDTH // LANES
    return pl.pallas_call(
        functools.partial(_sb_kernel, nq=nq),
        grid=(b, pairs, nq),
        in_specs=[
            pl.BlockSpec((1, TQ, LANES), lambda bb, p, i: (bb, i, U_SBQ // LANES + p)),
            pl.BlockSpec((1, t, LANES), lambda bb, p, i: (bb, 0, U_SBK // LANES + p)),
            pl.BlockSpec((1, t, LANES), lambda bb, p, i: (bb, 0, U_SBV // LANES + p)),
        ],
        out_specs=pl.BlockSpec((1, TQ, LANES), lambda bb, p, i: (bb, i, p)),
        out_shape=jax.ShapeDtypeStruct((b, t, GROUP_WIDTH), F32),
        scratch_shapes=[pltpu.VMEM((2, TQ, t), F32), pltpu.VMEM((2, TQ, t), BF16)],
        compiler_params=_cparams("parallel", "parallel", "arbitrary"),
        name="sb_attention",
    )(usb3, usb3, usb3)


CONV_HALO = 32


def _conv_kernel(a_ref, g_ref, ah_ref, gh_ref, dww_ref, dwb_ref, lng_ref, lnb_ref, pw_ref, pwb_ref,
                 o_ref, h_ref, s_ref, *, tm):
    i = pl.program_id(1)
    h_ref[CONV_HALO:, :] = a_ref[0] * jax.nn.sigmoid(g_ref[0])
    halo = ah_ref[0] * jax.nn.sigmoid(gh_ref[0])
    h_ref[0:CONV_HALO, :] = jnp.where(i == 0, 0.0, halo)
    acc = jnp.zeros((tm, GROUP_WIDTH), F32) + dwb_ref[...]
    base = CONV_HALO - (CONV_WIDTH - 1)
    sub = 8
    for r in range(sub):
        taps = [k for k in range(CONV_WIDTH) if (base + k) % sub == r]
        if not taps:
            continue
        lo = base + taps[0]
        span = (taps[-1] - taps[0]) + tm
        s_ref[r, 0:span, :] = h_ref[lo:lo + span, :]
        for k in taps:
            acc = acc + dww_ref[k:k + 1, :] * s_ref[r, k - taps[0]:k - taps[0] + tm, :]
    mu = jnp.mean(acc, axis=-1, keepdims=True)
    cen = acc - mu
    var = jnp.mean(cen * cen, axis=-1, keepdims=True)
    hn = cen * lax.rsqrt(var + CONV_LN_EPS) * lng_ref[...] + lnb_ref[...]
    act = hn * jax.nn.sigmoid(hn)
    o_ref[0] = _dot(act.astype(BF16), pw_ref[...]) + pwb_ref[...]


def _conformer(u3, dw_w, dw_b, ln_g, ln_b, pw_w, pw_b):
    b, t, _ = u3.shape
    tm = 512
    cw = GROUP_WIDTH
    hb = tm // CONV_HALO

    def halo_map(col):
        return lambda bb, i: (bb, jnp.maximum(i * hb - 1, 0), col)

    row = lambda bb, i: (0, 0)
    return pl.pallas_call(
        functools.partial(_conv_kernel, tm=tm),
        grid=(b, t // tm),
        in_specs=[
            pl.BlockSpec((1, tm, cw), lambda bb, i: (bb, i, U_CA // cw)),
            pl.BlockSpec((1, tm, cw), lambda bb, i: (bb, i, U_CG // cw)),
            pl.BlockSpec((1, CONV_HALO, cw), halo_map(U_CA // cw)),
            pl.BlockSpec((1, CONV_HALO, cw), halo_map(U_CG // cw)),
            pl.BlockSpec((CONV_WIDTH, cw), row),
            pl.BlockSpec((1, cw), row),
            pl.BlockSpec((1, cw), row),
            pl.BlockSpec((1, cw), row),
            pl.BlockSpec((cw, cw), row),
            pl.BlockSpec((1, cw), row),
        ],
        out_specs=pl.BlockSpec((1, tm, cw), lambda bb, i: (bb, i, 0)),
        out_shape=jax.ShapeDtypeStruct((b, t, cw), F32),
        scratch_shapes=[pltpu.VMEM((tm + CONV_HALO, cw), F32), pltpu.VMEM((8, tm + CONV_HALO, cw), F32)],
        compiler_params=_cparams("parallel", "parallel"),
        name="conformer_conv",
    )(u3, u3, u3, u3, dw_w, dw_b.reshape(1, cw), ln_g.reshape(1, cw), ln_b.reshape(1, cw),
      pw_w.astype(BF16), pw_b.reshape(1, cw))


def _head_norm_rope(x, gain, cos, sin_signed, seg_mean):
    ms = _split_dot(x * x, seg_mean)
    y = x * lax.rsqrt(ms + EPS) * gain
    lane = _lane_iota()
    half = HEAD_DIM // 2
    first = (lane & (HEAD_DIM - 1)) < half
    partner = jnp.where(first, pltpu.roll(y, LANES - half, 1), pltpu.roll(y, half, 1))
    return y * cos + partner * sin_signed


def _dup(x, g):
    lane = _lane_iota()
    sw = pltpu.roll(x, HEAD_DIM, 1)
    if g == 0:
        return jnp.where(lane < HEAD_DIM, x, sw)
    return jnp.where(lane < HEAD_DIM, sw, x)


def _nsa_prep_kernel(q_ref, kv_ref, gk_ref, cos_ref, sin_ref, qg_ref, kg_ref,
                     qn_ref, kc_ref, kse_ref, kso_ref, vs_ref, kw_ref, vw_ref, gt_ref, *, per_seq):
    cos = cos_ref[...]
    sin = sin_ref[...]
    seg = lax.broadcasted_iota(jnp.int32, (LANES, LANES), 0) >> HEAD_SHIFT
    seg_mean = jnp.where(seg == lax.broadcasted_iota(jnp.int32, (LANES, LANES), 1) >> HEAD_SHIFT,
                         1.0 / HEAD_DIM, 0.0).astype(BF16)
    scale = HEAD_DIM ** -0.5
    for p in range(GROUP_WIDTH // LANES):
        x = q_ref[:, p * LANES:(p + 1) * LANES]
        qn_ref[:, p * LANES:(p + 1) * LANES] = (
            _head_norm_rope(x, qg_ref[...], cos, sin, seg_mean) * scale).astype(BF16)

    def blk(n):
        return kv_ref[:, n * LANES:(n + 1) * LANES]

    kc_ref[...] = _head_norm_rope(blk(0), kg_ref[0:1, :], cos, sin, seg_mean)
    ks = _head_norm_rope(blk(2), kg_ref[1:2, :], cos, sin, seg_mean)
    kw = _head_norm_rope(blk(4), kg_ref[2:3, :], cos, sin, seg_mean)
    vs = blk(3)
    vw = blk(5)
    tm = ks.shape[0]
    lane = _lane_iota()
    t = (pl.program_id(0) % per_seq) * tm + lax.broadcasted_iota(jnp.int32, (tm, 1), 0)
    sblk = t >> SEL_SHIFT
    hot_lo = jnp.where(lane == sblk, 1.0, 0.0)
    hot_hi = jnp.where(lane - HEAD_DIM == sblk, 1.0, 0.0)
    ks_sw = pltpu.roll(ks, HEAD_DIM, 1)
    for g in range(2):
        sl = slice(g * LANES, (g + 1) * LANES)
        kse_ref[:, sl] = jnp.where(lane < HEAD_DIM, ks if g == 0 else ks_sw, hot_hi).astype(BF16)
        kso_ref[:, sl] = jnp.where(lane >= HEAD_DIM, ks_sw if g == 0 else ks, hot_lo).astype(BF16)
        vs_ref[:, sl] = _dup(vs, g).astype(BF16)
        kw_ref[:, sl] = _dup(kw, g).astype(BF16)
        vw_ref[:, sl] = _dup(vw, g).astype(BF16)
    gates = jax.nn.sigmoid(gk_ref[...])
    gt_ref[:, 0:LANES] = gates
    gt_ref[:, LANES:2 * LANES] = pltpu.roll(gates, LANES - 12, 1)


def _nsa_prep(u2, cos, sin, q_gain, k_gain, seq):
    m_rows = u2.shape[0]
    tm = 512
    per_seq = seq // tm
    rowmap = lambda i: (i, 0)
    tab = lambda i: (i % per_seq, 0)
    const = lambda i: (0, 0)
    outs = [
        jax.ShapeDtypeStruct((m_rows, GROUP_WIDTH), BF16),
        jax.ShapeDtypeStruct((m_rows, LANES), F32),
        jax.ShapeDtypeStruct((m_rows, 2 * LANES), BF16),
        jax.ShapeDtypeStruct((m_rows, 2 * LANES), BF16),
        jax.ShapeDtypeStruct((m_rows, 2 * LANES), BF16),
        jax.ShapeDtypeStruct((m_rows, 2 * LANES), BF16),
        jax.ShapeDtypeStruct((m_rows, 2 * LANES), BF16),
        jax.ShapeDtypeStruct((m_rows, 2 * LANES), F32),
    ]
    return pl.pallas_call(
        functools.partial(_nsa_prep_kernel, per_seq=per_seq),
        grid=(m_rows // tm,),
        in_specs=[
            pl.BlockSpec((tm, GROUP_WIDTH), lambda i: (i, U_NQ // GROUP_WIDTH)),
            pl.BlockSpec((tm, 6 * LANES), lambda i: (i, U_NKV // (6 * LANES))),
            pl.BlockSpec((tm, LANES), lambda i: (i, (U_MLA + MLA_Q_RANK + MLA_KV_RANK) // LANES)),
            pl.BlockSpec((tm, LANES), tab),
            pl.BlockSpec((tm, LANES), tab),
            pl.BlockSpec((1, LANES), const),
            pl.BlockSpec((3, LANES), const),
        ],
        out_specs=[
            pl.BlockSpec((tm, GROUP_WIDTH), rowmap),
            pl.BlockSpec((tm, LANES), rowmap),
            pl.BlockSpec((tm, 2 * LANES), rowmap),
            pl.BlockSpec((tm, 2 * LANES), rowmap),
            pl.BlockSpec((tm, 2 * LANES), rowmap),
            pl.BlockSpec((tm, 2 * LANES), rowmap),
            pl.BlockSpec((tm, 2 * LANES), rowmap),
            pl.BlockSpec((tm, 2 * LANES), rowmap),
        ],
        out_shape=outs,
        compiler_params=_cparams("parallel"),
        name="nsa_prep",
    )(u2, u2, u2, cos, sin, q_gain, k_gain)


def _compress_kernel(xk_ref, xv_ref, pe_ref, wk_ref, wv_ref, kc_ref, vc_ref):
    nrow = xk_ref.shape[2]

    def comp(x, pe_a, pe_b, w_ref):
        a = _dot((x + pe_a).astype(BF16), w_ref[0])
        bm = _dot((x + pe_b).astype(BF16), w_ref[1])
        return a + pltpu.roll(bm, nrow - 1, 0)

    kc_ref[0, 0] = comp(xk_ref[0, 0], pe_ref[0:1, :], pe_ref[1:2, :], wk_ref).astype(BF16)
    vc_ref[0, 0] = comp(xv_ref[0, 0], pe_ref[2:3, :], pe_ref[3:4, :], wv_ref).astype(BF16)


def _compress(xk, xv, pe, wk, wv):
    b, g, nrow, wide = xk.shape
    xmap = lambda bb, gg: (bb, gg, 0, 0)
    out = jax.ShapeDtypeStruct((b, g, nrow, LANES), BF16)
    return pl.pallas_call(
        _compress_kernel,
        grid=(b, g),
        in_specs=[
            pl.BlockSpec((1, 1, nrow, wide), xmap),
            pl.BlockSpec((1, 1, nrow, wide), xmap),
            pl.BlockSpec((4, wide), lambda bb, gg: (0, 0)),
            pl.BlockSpec((2, wide, LANES), lambda bb, gg: (0, 0, 0)),
            pl.BlockSpec((2, wide, LANES), lambda bb, gg: (0, 0, 0)),
        ],
        out_specs=[pl.BlockSpec((1, 1, nrow, LANES), xmap), pl.BlockSpec((1, 1, nrow, LANES), xmap)],
        out_shape=[out, out],
        compiler_params=_cparams("parallel", "parallel"),
        name="nsa_compress",
    )(xk, xv, pe, wk, wv)


def _attend(q, k, v, s_ref, p_ref, masks, scale=None):
    n = k.shape[0] // TK
    s_ref[:, :n * TK] = _dot_nt(q, k)
    mfold = None
    for j in range(n):
        sl = slice(j * TK, (j + 1) * TK)
        s = s_ref[:, sl]
        if j in masks:
            s = jnp.where(masks[j], s, NEG)
            s_ref[:, sl] = s
        f = jnp.maximum(s[:, :LANES], s[:, LANES:])
        mfold = f if mfold is None else jnp.maximum(mfold, f)
    m = jnp.max(mfold, axis=-1, keepdims=True)
    lfold = None
    for j in range(n):
        sl = slice(j * TK, (j + 1) * TK)
        p = jnp.exp2((s_ref[:, sl] - m) * (LOG2E if scale is None else scale * LOG2E))
        f = p[:, :LANES] + p[:, LANES:]
        lfold = f if lfold is None else lfold + f
        p_ref[:, sl] = p.astype(BF16)
    l = jnp.sum(lfold, axis=-1, keepdims=True)
    return _dot(p_ref[:, :n * TK], v) / l


def _nsa_kernel(q_ref, kc_ref, vc_ref, kse_ref, kso_ref, vs_ref, kw_ref, vw_ref, gt_ref, ov_ref,
                o_ref, s_ref, p_ref, sw_ref, pw_ref, *, nq, n_cmp, n_sblk, n_sel):
    i = pl.program_id(2)
    lane = _lane_iota()
    row = i * TQ + lax.broadcasted_iota(jnp.int32, (TQ, 1), 0)
    rep = 4
    q2s = [q_ref[0, :, p * LANES:(p + 1) * LANES] for p in range(2)]
    qs = []
    for q2 in q2s:
        qs.append(jnp.where(lane < HEAD_DIM, q2, jnp.zeros_like(q2)))
        qs.append(jnp.where(lane >= HEAD_DIM, q2, jnp.zeros_like(q2)))

    kc = kc_ref[0, 0]
    vc = vc_ref[0, 0]
    cmask = ((lane * NSA_CMP_STRIDE + (NSA_CMP_LEN - 1)) <= row) & (lane < n_cmp)
    psum = jnp.zeros((TQ, LANES), F32)
    o_cmp = []
    for r in range(rep):
        s = jnp.where(cmask, _dot_nt(qs[r], kc), NEG)
        mx = jnp.max(s, axis=-1, keepdims=True)
        p = jnp.where(cmask, jnp.exp(s - mx), 0.0)
        p = p / jnp.maximum(jnp.sum(p, axis=-1, keepdims=True), 1e-30)
        o_cmp.append(_dot(p.astype(BF16), vc))
        psum = psum + p
    ov_t = ov_ref[0:n_sblk, :]
    p_hi = psum.astype(BF16)
    p_lo = (psum - p_hi.astype(F32)).astype(BF16)
    imp = _dot_nt(ov_t, p_hi) + _dot_nt(ov_t, p_lo)
    cur = (i * TQ + _lane_iota(TQ)) >> SEL_SHIFT
    blk = lax.broadcasted_iota(jnp.int32, (n_sblk, 1), 0)
    eligible = blk <= cur
    forced = (blk == 0) | (blk == cur) | (blk == cur - 1)
    score = jnp.where(eligible, imp + jnp.where(forced, NSA_FORCE_BONUS, 0.0), NEG)
    rank = jnp.zeros((n_sblk, TQ), F32)
    for sp in range(n_sblk):
        other = score[sp:sp + 1, :]
        beats = (other > score) | ((other == score) & (blk > sp))
        rank = rank + jnp.where(beats, 1.0, 0.0)
    bias_t = jnp.where(eligible & (rank < n_sel), 0.0, NEG)

    bias_lo = jnp.concatenate([bias_t, jnp.zeros((LANES - n_sblk, TQ), F32)], axis=0).T
    bias_hi = pltpu.roll(bias_lo, HEAD_DIM, 1).astype(BF16)
    bias_lo = bias_lo.astype(BF16)
    q_aug = []
    for q2 in q2s:
        q_aug.append(jnp.where(lane < HEAD_DIM, q2, bias_hi))
        q_aug.append(jnp.where(lane >= HEAD_DIM, q2, bias_lo))
    causal, strict = _diag_masks()
    wtiles = NSA_WINDOW // TK
    gt = gt_ref[0]

    for c in range(nq):
        @pl.when(i == c)
        def _():
            kv_len = (c + 1) * TK
            lo = max(c - wtiles, 0) * TK
            wmasks = {c - max(c - wtiles, 0): causal}
            if c >= wtiles:
                wmasks[0] = jnp.logical_not(causal)
            outs = []
            for r in range(rep):
                ks_ref = kse_ref if r % 2 == 0 else kso_ref
                o_slc = _attend(q_aug[r], ks_ref[0, :kv_len, :], vs_ref[0, :kv_len, :],
                                s_ref.at[r], p_ref.at[r], {c: causal})
                o_win = _attend(qs[r], kw_ref[0, lo:kv_len, :], vw_ref[0, lo:kv_len, :],
                                sw_ref.at[r], pw_ref.at[r], wmasks)
                outs.append(gt[:, 3 * r:3 * r + 1] * o_cmp[r] + gt[:, 3 * r + 1:3 * r + 2] * o_slc
                            + gt[:, 3 * r + 2:3 * r + 3] * o_win)
            o_ref[0, :, 0:LANES] = jnp.where(lane < HEAD_DIM, outs[0], outs[1])
            o_ref[0, :, LANES:2 * LANES] = jnp.where(lane < HEAD_DIM, outs[2], outs[3])


def _nsa_attention(qn, kcmp, vcmp, kse, kso, vsd, kwd, vwd, gates, overlap):
    b, t, _ = qn.shape
    nq = t // TQ
    n_cmp = (t - NSA_CMP_LEN) // NSA_CMP_STRIDE + 1
    n_sblk = t // NSA_SEL_LEN
    nrow = kcmp.shape[2]
    wlen = NSA_WINDOW + TK
    kv = lambda bb, g, i: (bb, 0, g)
    cm = lambda bb, g, i: (bb, g, 0, 0)
    return pl.pallas_call(
        functools.partial(_nsa_kernel, nq=nq, n_cmp=n_cmp, n_sblk=n_sblk, n_sel=min(NSA_N_SEL, n_sblk)),
        grid=(b, 2, nq),
        in_specs=[
            pl.BlockSpec((1, TQ, 2 * LANES), lambda bb, g, i: (bb, i, g)),
            pl.BlockSpec((1, 1, nrow, LANES), cm),
            pl.BlockSpec((1, 1, nrow, LANES), cm),
            pl.BlockSpec((1, t, LANES), kv),
            pl.BlockSpec((1, t, LANES), kv),
            pl.BlockSpec((1, t, LANES), kv),
            pl.BlockSpec((1, t, LANES), kv),
            pl.BlockSpec((1, t, LANES), kv),
            pl.BlockSpec((1, TQ, LANES), lambda bb, g, i: (bb, i, g)),
            pl.BlockSpec((LANES, LANES), lambda bb, g, i: (0, 0)),
        ],
        out_specs=pl.BlockSpec((1, TQ, 2 * LANES), lambda bb, g, i: (bb, i, g)),
        out_shape=jax.ShapeDtypeStruct((b, t, GROUP_WIDTH), F32),
        scratch_shapes=[pltpu.VMEM((4, TQ, t), F32), pltpu.VMEM((4, TQ, t), BF16),
                        pltpu.VMEM((4, TQ, wlen), F32), pltpu.VMEM((4, TQ, wlen), BF16)],
        compiler_params=_cparams("parallel", "parallel", "arbitrary"),
        name="nsa_attention",
    )(qn, kcmp, vcmp, kse, kso, vsd, kwd, vwd, gates, overlap)


def _mla_prep_kernel(u_ref, cos_ref, sin_ref, qlg_ref, kvlg_ref, wq_ref, wk_ref, wv_ref, qg_ref, kg_ref,
                     q_ref, k_ref, v_ref):
    lane = _lane_iota()
    cos = cos_ref[...]
    sin = sin_ref[...]
    half = MLA_ROPE // 2

    def lat_norm(x, g):
        ms = jnp.mean(x * x, axis=-1, keepdims=True)
        return (x * lax.rsqrt(ms + EPS) * g).astype(BF16)

    def head_norm_rope(x, g):
        ms = jnp.sum(x * x, axis=-1, keepdims=True) * (1.0 / MLA_QK)
        y = x * lax.rsqrt(ms + EPS) * g
        partner = jnp.where(lane < MLA_NOPE + half, pltpu.roll(y, LANES - half, 1), pltpu.roll(y, half, 1))
        return y * cos + partner * sin

    q = _dot(lat_norm(u_ref[:, 0:MLA_Q_RANK], qlg_ref[...]), wq_ref[...])
    kvn = lat_norm(u_ref[:, MLA_Q_RANK:MLA_Q_RANK + MLA_KV_RANK], kvlg_ref[...])
    kn = _dot(kvn, wk_ref[...])
    v_ref[...] = _dot(kvn, wv_ref[...]).astype(BF16)
    gk = u_ref[:, MLA_Q_RANK + MLA_KV_RANK:MLA_Q_RANK + MLA_KV_RANK + LANES]
    k_rope = jnp.where((lane >= MLA_NOPE) & (lane < MLA_QK), gk, 0.0)
    for h in range(MLA_HEADS):
        sl = slice(h * LANES, (h + 1) * LANES)
        q_ref[:, sl] = head_norm_rope(q[:, sl], qg_ref[...]).astype(BF16)
        k_ref[:, sl] = head_norm_rope(kn[:, sl] + k_rope, kg_ref[...]).astype(BF16)


def _mla_prep(u2, cos, sin, q_lat_gain, kv_lat_gain, wq, wk, wv, q_gain, k_gain, seq):
    m_rows = u2.shape[0]
    tm = 512
    per_seq = seq // tm
    width = MLA_Q_RANK + MLA_KV_RANK + LANES
    rowmap = lambda i: (i, 0)
    tab = lambda i: (i % per_seq, 0)
    const = lambda i: (0, 0)
    return pl.pallas_call(
        _mla_prep_kernel,
        grid=(m_rows // tm,),
        in_specs=[
            pl.BlockSpec((tm, width), lambda i: (i, U_MLA // width)),
            pl.BlockSpec((tm, LANES), tab),
            pl.BlockSpec((tm, LANES), tab),
            pl.BlockSpec((1, MLA_Q_RANK), const),
            pl.BlockSpec((1, MLA_KV_RANK), const),
            pl.BlockSpec(wq.shape, const),
            pl.BlockSpec(wk.shape, const),
            pl.BlockSpec(wv.shape, const),
            pl.BlockSpec((1, LANES), const),
            pl.BlockSpec((1, LANES), const),
        ],
        out_specs=[
            pl.BlockSpec((tm, MLA_HEADS * LANES), rowmap),
            pl.BlockSpec((tm, MLA_HEADS * LANES), rowmap),
            pl.BlockSpec((tm, GROUP_WIDTH), rowmap),
        ],
        out_shape=[
            jax.ShapeDtypeStruct((m_rows, MLA_HEADS * LANES), BF16),
            jax.ShapeDtypeStruct((m_rows, MLA_HEADS * LANES), BF16),
            jax.ShapeDtypeStruct((m_rows, GROUP_WIDTH), BF16),
        ],
        compiler_params=_cparams("parallel"),
        name="mla_prep",
    )(u2, cos, sin, q_lat_gain, kv_lat_gain, wq, wk, wv, q_gain, k_gain)


def _mla_kernel(q_ref, k_ref, v_ref, o_ref, s_ref, p_ref, *, nq):
    i = pl.program_id(2)
    lane = _lane_iota()
    causal, _ = _diag_masks()
    scale = MLA_QK ** -0.5
    for c in range(nq):
        @pl.when(i == c)
        def _():
            kv_len = (c + 1) * TK
            outs = []
            for hd in range(2):
                sl = slice(hd * LANES, (hd + 1) * LANES)
                outs.append(_attend(q_ref[0, :, sl], k_ref[0, :kv_len, sl], v_ref[0, :kv_len, :],
                                    s_ref.at[hd], p_ref.at[hd], {c: causal}, scale=scale))
            o_ref[0] = jnp.where(lane < HEAD_DIM, outs[0], outs[1])


def _mla_attention(qm, km, vm):
    b, t, _ = qm.shape
    nq = t // TQ
    return pl.pallas_call(
        functools.partial(_mla_kernel, nq=nq),
        grid=(b, MLA_HEADS // 2, nq),
        in_specs=[
            pl.BlockSpec((1, TQ, 2 * LANES), lambda bb, p, i: (bb, i, p)),
            pl.BlockSpec((1, t, 2 * LANES), lambda bb, p, i: (bb, 0, p)),
            pl.BlockSpec((1, t, LANES), lambda bb, p, i: (bb, 0, p)),
        ],
        out_specs=pl.BlockSpec((1, TQ, LANES), lambda bb, p, i: (bb, i, p)),
        out_shape=jax.ShapeDtypeStruct((b, t, GROUP_WIDTH), F32),
        scratch_shapes=[pltpu.VMEM((2, TQ, t), F32), pltpu.VMEM((2, TQ, t), BF16)],
        compiler_params=_cparams("parallel", "parallel", "arbitrary"),
        name="mla_attention",
    )(qm, km, vm)


def _out_kernel(a_ref, b_ref, c_ref, d_ref, gn_ref, w_ref, x_ref, mod_ref, o_ref):
    acc = None
    for gi, r in enumerate((a_ref, b_ref, c_ref, d_ref)):
        y = r[...]
        ms = jnp.mean(y * y, axis=-1, keepdims=True)
        sl = slice(gi * GROUP_WIDTH, (gi + 1) * GROUP_WIDTH)
        part = _dot((y * lax.rsqrt(ms + EPS) * gn_ref[:, sl]).astype(BF16), w_ref[sl, :])
        acc = part if acc is None else acc + part
    o_ref[...] = x_ref[...] + mod_ref[0, 2:3, :] * acc


def _out_proj(parts, gn, w, layer, x2, mod, seq):
    m_rows, d = x2.shape
    tm = 512
    per_seq = seq // tm
    part = pl.BlockSpec((tm, GROUP_WIDTH), lambda i: (i, 0))
    return pl.pallas_call(
        _out_kernel,
        grid=(m_rows // tm,),
        in_specs=[
            part, part, part, part,
            pl.BlockSpec((1, 4 * GROUP_WIDTH), lambda i: (0, 0)),
            pl.BlockSpec((None, 4 * GROUP_WIDTH, d), lambda i: (layer, 0, 0)),
            pl.BlockSpec((tm, d), lambda i: (i, 0)),
            pl.BlockSpec((1, 6, d), lambda i: (i // per_seq, 0, 0)),
        ],
        out_specs=pl.BlockSpec((tm, d), lambda i: (i, 0)),
        out_shape=jax.ShapeDtypeStruct((m_rows, d), F32),
        compiler_params=_cparams("parallel"),
        name="out_proj",
    )(*parts, gn, w, x2, mod)


FFN_HALO = 16


def _ffn_kernel(x_ref, xh_ref, mod_ref, g_ref, wa_ref, wg_ref, cwa_ref, cwg_ref, cba_ref, cbg_ref, wd_ref,
                o_ref, h_ref, acc_ref, *, per_seq):
    i = pl.program_id(0)
    j = pl.program_id(1)
    m = mod_ref[0]

    @pl.when(j == 0)
    def _():
        h_ref[FFN_HALO:, :] = _modulated_norm(x_ref[...], g_ref[...], m[3:4], m[4:5]).astype(BF16)
        halo = _modulated_norm(xh_ref[...], g_ref[...], m[3:4], m[4:5])
        h_ref[0:FFN_HALO, :] = jnp.where(i % per_seq == 0, 0.0, halo).astype(BF16)
        acc_ref[...] = jnp.zeros_like(acc_ref)

    h = h_ref[...]

    def conv(u, cw_ref, cb_ref):
        y = cw_ref[2:3, :] * u + cw_ref[1:2, :] * pltpu.roll(u, 1, 0) + cw_ref[0:1, :] * pltpu.roll(u, 2, 0)
        return y[FFN_HALO:, :] + cb_ref[...]

    ya = conv(_dot(h, wa_ref[...]), cwa_ref, cba_ref)
    yg = conv(_dot(h, wg_ref[...]), cwg_ref, cbg_ref)
    act = (yg * jax.nn.sigmoid(yg)) * ya
    acc_ref[...] += _dot(act.astype(BF16), wd_ref[...])

    @pl.when(j == pl.num_programs(1) - 1)
    def _():
        o_ref[...] = x_ref[...] + m[5:6] * acc_ref[...]


def _ffn(x2, mod, g, w_up, conv_w, conv_b, w_down, layer, seq):
    m_rows, d = x2.shape
    d_ff = w_down.shape[1]
    tm, tn = 512, 512
    per_seq = seq // tm
    nj = d_ff // tn
    hb = tm // FFN_HALO
    cb = conv_b.reshape(1, 2 * d_ff)
    return pl.pallas_call(
        functools.partial(_ffn_kernel, per_seq=per_seq),
        grid=(m_rows // tm, nj),
        in_specs=[
            pl.BlockSpec((tm, d), lambda i, j: (i, 0)),
            pl.BlockSpec((FFN_HALO, d), lambda i, j: (jnp.maximum(i * hb - 1, 0), 0)),
            pl.BlockSpec((1, 6, d), lambda i, j: (i // per_seq, 0, 0)),
            pl.BlockSpec((1, d), lambda i, j: (0, 0)),
            pl.BlockSpec((None, d, tn), lambda i, j: (layer, 0, j)),
            pl.BlockSpec((None, d, tn), lambda i, j: (layer, 0, nj + j)),
            pl.BlockSpec((3, tn), lambda i, j: (0, j)),
            pl.BlockSpec((3, tn), lambda i, j: (0, nj + j)),
            pl.BlockSpec((1, tn), lambda i, j: (0, j)),
            pl.BlockSpec((1, tn), lambda i, j: (0, nj + j)),
            pl.BlockSpec((None, tn, d), lambda i, j: (layer, j, 0)),
        ],
        out_specs=pl.BlockSpec((tm, d), lambda i, j: (i, 0)),
        out_shape=jax.ShapeDtypeStruct((m_rows, d), F32),
        scratch_shapes=[pltpu.VMEM((tm + FFN_HALO, d), BF16), pltpu.VMEM((tm, d), F32)],
        compiler_params=_cparams("parallel", "arbitrary"),
        name="conv_glu_ffn",
    )(x2, x2, mod, g, w_up, w_up, conv_w, conv_w, cb, cb, w_down)


def _reorder_w_in(w):
    n_gate = 24
    off_mla = REF_ALIGNED_COLS + n_gate
    lat = MLA_Q_RANK + MLA_KV_RANK
    z = lambda n: jnp.zeros(w.shape[:-1] + (n,), w.dtype)
    out = jnp.concatenate([
        w[..., :REF_ALIGNED_COLS], w[..., off_mla:off_mla + lat],
        w[..., REF_ALIGNED_COLS:off_mla], z(MLA_NOPE - n_gate),
        w[..., off_mla + lat:off_mla + lat + MLA_ROPE], z(LANES - MLA_QK)], axis=-1)
    assert out.shape[-1] == U_COLS
    return out.astype(BF16)


def _rope_tables(seq, dim, lane_cos, lane_sin):
    inv = ROPE_THETA ** (-jnp.arange(0, dim, 2, dtype=F32) / dim)
    ang = jnp.arange(seq).astype(F32)[:, None] * inv[None, :]
    cos, sin = jnp.cos(ang), jnp.sin(ang)
    return lane_cos(cos), lane_sin(sin)


def _nsa_tables(seq):
    return _rope_tables(seq, HEAD_DIM,
                        lambda c: jnp.concatenate([c, c, c, c], axis=1),
                        lambda s: jnp.concatenate([-s, s, -s, s], axis=1))


def _mla_tables(seq):
    ones = jnp.ones((seq, MLA_NOPE), F32)
    zeros = jnp.zeros((seq, MLA_NOPE), F32)
    pad1 = jnp.ones((seq, LANES - MLA_QK), F32)
    pad0 = jnp.zeros((seq, LANES - MLA_QK), F32)
    return _rope_tables(seq, MLA_ROPE,
                        lambda c: jnp.concatenate([ones, c, c, pad1], axis=1),
                        lambda s: jnp.concatenate([zeros, -s, s, pad0], axis=1))


def _overlap_matrix(seq):
    n_cmp = (seq - NSA_CMP_LEN) // NSA_CMP_STRIDE + 1
    n_sblk = seq // NSA_SEL_LEN
    starts = np.arange(n_cmp) * NSA_CMP_STRIDE
    sel_start = np.arange(n_sblk) * NSA_SEL_LEN
    ov = np.clip(np.minimum(starts[:, None] + NSA_CMP_LEN, sel_start[None, :] + NSA_SEL_LEN)
                 - np.maximum(starts[:, None], sel_start[None, :]), 0, None) / NSA_CMP_LEN
    full = np.zeros((LANES, LANES), np.float32)
    full[:n_sblk, :n_cmp] = ov.T
    return jnp.asarray(full, BF16)


def _pad_lanes(v, n):
    return jnp.pad(v, (0, n - v.shape[0]))


def _mixer(x2, mod, batch, seq, p, w_in_all, w_o_all, layer):
    m_rows = x2.shape[0]
    usb, u2 = _in_proj(x2, mod, p["norm_mix"].reshape(1, -1), w_in_all, layer, seq)
    u3 = u2.reshape(batch, seq, U_COLS - U_SB_COLS)

    o_sb = _sb_attention(usb.reshape(batch, seq, U_SB_COLS))
    o_conv = _conformer(u3, p["conv_dw_w"], p["conv_dw_b"], p["conv_ln_g"], p["conv_ln_b"],
                        p["conv_pw_w"], p["conv_pw_b"])

    cos_n, sin_n = _nsa_tables(seq)
    q_gain = jnp.tile(p["nsa_q_norm"], 2).reshape(1, LANES)
    k_gain = jnp.tile(p["nsa_k_norm"], (1, 2))
    qn, kc, kse, kso, vsd, kwd, vwd, gates = _nsa_prep(u2, cos_n, sin_n, q_gain, k_gain, seq)
    nrow = seq // NSA_CMP_STRIDE

    def blocks(t2):
        t5 = t2.reshape(batch, nrow, NSA_CMP_STRIDE, 2, HEAD_DIM)
        return t5.transpose(0, 3, 1, 2, 4).reshape(batch, 2, nrow, NSA_CMP_STRIDE * HEAD_DIM)

    vc_raw = u2[:, U_NKV + LANES:U_NKV + 2 * LANES]
    half = NSA_CMP_STRIDE * HEAD_DIM
    pe = p["nsa_cmp_pe"].reshape(2, 2, half).reshape(4, half)
    wdup = jnp.concatenate([p["nsa_cmp_w"], p["nsa_cmp_w"]], axis=-1).astype(BF16)
    wk = wdup[0].reshape(2, half, LANES)
    wv = wdup[1].reshape(2, half, LANES)
    kcmp, vcmp = _compress(blocks(kc), blocks(vc_raw), pe, wk, wv)
    o_nsa = _nsa_attention(qn.reshape(batch, seq, -1), kcmp, vcmp,
                           kse.reshape(batch, seq, -1), kso.reshape(batch, seq, -1), vsd.reshape(batch, seq, -1),
                           kwd.reshape(batch, seq, -1), vwd.reshape(batch, seq, -1),
                           gates.reshape(batch, seq, -1), _overlap_matrix(seq))

    cos_m, sin_m = _mla_tables(seq)
    w_uq = p["mla_w_uq"].reshape(MLA_Q_RANK, MLA_HEADS, MLA_QK)
    wq = jnp.pad(w_uq, ((0, 0), (0, 0), (0, LANES - MLA_QK))).reshape(MLA_Q_RANK, MLA_HEADS * LANES).astype(BF16)
    w_ukv = p["mla_w_ukv"].reshape(MLA_KV_RANK, MLA_HEADS, 2 * HEAD_DIM)
    wk_m = jnp.pad(w_ukv[:, :, :MLA_NOPE], ((0, 0), (0, 0), (0, LANES - MLA_NOPE)))
    wk_m = wk_m.reshape(MLA_KV_RANK, MLA_HEADS * LANES).astype(BF16)
    wv_m = w_ukv[:, :, MLA_NOPE:].reshape(MLA_KV_RANK, GROUP_WIDTH).astype(BF16)
    qm, km, vm = _mla_prep(u2, cos_m, sin_m, p["mla_q_lat_norm"].reshape(1, -1),
                           p["mla_kv_lat_norm"].reshape(1, -1), wq, wk_m, wv_m,
                           _pad_lanes(p["mla_q_norm"], LANES).reshape(1, LANES),
                           _pad_lanes(p["mla_k_norm"], LANES).reshape(1, LANES), seq)
    o_mla = _mla_attention(qm.reshape(batch, seq, -1), km.reshape(batch, seq, -1), vm.reshape(batch, seq, -1))

    parts = [o.reshape(m_rows, GROUP_WIDTH) for o in (o_sb, o_conv, o_nsa, o_mla)]
    return _out_proj(parts, p["group_norm"].reshape(1, -1), w_o_all, layer, x2, mod, seq)


def kernel(x, c, ada_w, ada_b, norm_mix, norm_ffn, w_in, conv_dw_w, conv_dw_b, conv_ln_g, conv_ln_b, conv_pw_w, conv_pw_b, nsa_q_norm, nsa_k_norm, nsa_cmp_pe, nsa_cmp_w, mla_q_lat_norm, mla_kv_lat_norm, mla_w_uq, mla_w_ukv, mla_q_norm, mla_k_norm, group_norm, w_o, ffn_up, ffn_conv_w, ffn_conv_b, ffn_down):
    batch, seq, d = x.shape
    depth = ada_w.shape[0]
    per_layer = dict(
        norm_mix=norm_mix, conv_dw_w=conv_dw_w, conv_dw_b=conv_dw_b, conv_ln_g=conv_ln_g,
        conv_ln_b=conv_ln_b, conv_pw_w=conv_pw_w, conv_pw_b=conv_pw_b, nsa_q_norm=nsa_q_norm,
        nsa_k_norm=nsa_k_norm, nsa_cmp_pe=nsa_cmp_pe, nsa_cmp_w=nsa_cmp_w, mla_q_lat_norm=mla_q_lat_norm,
        mla_kv_lat_norm=mla_kv_lat_norm, mla_w_uq=mla_w_uq, mla_w_ukv=mla_w_ukv, mla_q_norm=mla_q_norm,
        mla_k_norm=mla_k_norm, group_norm=group_norm)
    w_in_all = _reorder_w_in(w_in)
    w_o_all = w_o.astype(BF16)
    ffn_up_all = ffn_up.astype(BF16)
    ffn_down_all = ffn_down.astype(BF16)
    mods = _ada(c, ada_w, ada_b)
    x2 = x.reshape(batch * seq, d)
    for l in range(depth):
        p = {k: v[l] for k, v in per_layer.items()}
        x2 = _mixer(x2, mods[l], batch, seq, p, w_in_all, w_o_all, l)
        x2 = _ffn(x2, mods[l], norm_ffn[l].reshape(1, -1), ffn_up_all, ffn_conv_w[l], ffn_conv_b[l],
                  ffn_down_all, l, seq)
    return x2.reshape(batch, seq, d)
```

```python
import functools

import numpy as np
import jax
import jax.numpy as jnp
from jax import lax
from jax.experimental import pallas as pl
from jax.experimental.pallas import tpu as pltpu

F32 = jnp.float32
BF16 = jnp.bfloat16

LANES = 128
VMEM_LIMIT = 56 * 1024 * 1024

HEAD_DIM = 64
HEAD_SHIFT = 6
ROPE_THETA = 10000.0
EPS = 1e-6
GROUP_WIDTH = 512
CONV_WIDTH = 31
CONV_LN_EPS = 1e-5
NSA_CMP_LEN = 32
NSA_CMP_STRIDE = 16
NSA_SEL_LEN = 64
SEL_SHIFT = 6
NSA_N_SEL = 16
NSA_WINDOW = 512
NSA_FORCE_BONUS = 1e3
MLA_Q_RANK = 384
MLA_KV_RANK = 256
MLA_NOPE = 64
MLA_ROPE = 32
MLA_QK = MLA_NOPE + MLA_ROPE
MLA_HEADS = 8
HEADS_PER_STEP = 4

U_SBQ, U_SBK, U_SBV = 0, 512, 1024
U_SB_COLS = 1536
U_CA, U_CG = 0, 512
U_NQ = 1024
U_NKV = 1536
U_MLA = 2304
U_COLS = 4608
REF_ALIGNED_COLS = 3840

TQ = 256
TK = 256
NEG = -1e30
LOG2E = 1.4426950408889634


def _cparams(*sem):
    return pltpu.CompilerParams(dimension_semantics=sem, vmem_limit_bytes=VMEM_LIMIT)


def _dot(a, b):
    return jnp.dot(a, b, preferred_element_type=F32)


def _dot_nt(a, b):
    return lax.dot_general(a, b, (((1,), (1,)), ((), ())), preferred_element_type=F32)


def _split_dot(x, w):
    hi = x.astype(BF16)
    lo = (x - hi.astype(F32)).astype(BF16)
    return _dot(hi, w) + _dot(lo, w)


def _lane_iota(n=LANES):
    return lax.broadcasted_iota(jnp.int32, (1, n), 1)


def _ada_kernel(c_ref, w_ref, b_ref, o_ref):
    c = c_ref[...]
    cond = (c * jax.nn.sigmoid(c)).astype(BF16)
    o_ref[0] = _dot(cond, w_ref[0].astype(BF16)) + b_ref[0]


def _ada(c, ada_w, ada_b):
    n_layers, d, n = ada_w.shape
    b = c.shape[0]
    rows = 8
    tn = 1024
    cpad = jnp.pad(c, ((0, rows - b), (0, 0)))
    out = pl.pallas_call(
        _ada_kernel,
        grid=(n_layers, n // tn),
        in_specs=[
            pl.BlockSpec((rows, d), lambda l, j: (0, 0)),
            pl.BlockSpec((1, d, tn), lambda l, j: (l, 0, j)),
            pl.BlockSpec((1, 1, tn), lambda l, j: (l, 0, j)),
        ],
        out_specs=pl.BlockSpec((1, rows, tn), lambda l, j: (l, 0, j)),
        out_shape=jax.ShapeDtypeStruct((n_layers, rows, n), F32),
        compiler_params=_cparams("parallel", "parallel"),
        name="ada_mod",
    )(cpad, ada_w, ada_b.reshape(n_layers, 1, n))
    return out[:, :b].reshape(n_layers, b, 6, d)


def _modulated_norm(x, g, shift, scale):
    ms = jnp.mean(x * x, axis=-1, keepdims=True)
    return x * lax.rsqrt(ms + EPS) * g * (1.0 + scale) + shift


def _in_kernel(x_ref, mod_ref, g_ref, w_ref, sb_ref, o_ref, h_ref, *, n_sb):
    j = pl.program_id(1)

    @pl.when(j == 0)
    def _():
        m = mod_ref[0]
        h_ref[...] = _modulated_norm(x_ref[...], g_ref[...], m[0:1], m[1:2]).astype(BF16)

    @pl.when(j < n_sb)
    def _():
        sb_ref[...] = _dot(h_ref[...], w_ref[...]).astype(BF16)

    @pl.when(j >= n_sb)
    def _():
        o_ref[...] = _dot(h_ref[...], w_ref[...])


def _in_proj(x2, mod, g, w, layer, seq):
    m_rows, d = x2.shape
    n = w.shape[2]
    tm, tn = 1024, 768
    per_seq = seq // tm
    n_sb = U_SB_COLS // tn
    return pl.pallas_call(
        functools.partial(_in_kernel, n_sb=n_sb),
        grid=(m_rows // tm, n // tn),
        in_specs=[
            pl.BlockSpec((tm, d), lambda i, j: (i, 0)),
            pl.BlockSpec((1, 6, d), lambda i, j: (i // per_seq, 0, 0)),
            pl.BlockSpec((1, d), lambda i, j: (0, 0)),
            pl.BlockSpec((None, d, tn), lambda i, j: (layer, 0, j)),
        ],
        out_specs=[
            pl.BlockSpec((tm, tn), lambda i, j: (i, jnp.minimum(j, n_sb - 1))),
            pl.BlockSpec((tm, tn), lambda i, j: (i, jnp.maximum(j - n_sb, 0))),
        ],
        out_shape=[jax.ShapeDtypeStruct((m_rows, U_SB_COLS), BF16),
                   jax.ShapeDtypeStruct((m_rows, n - U_SB_COLS), F32)],
        scratch_shapes=[pltpu.VMEM((tm, d), BF16)],
        compiler_params=_cparams("parallel", "arbitrary"),
        name="in_proj",
    )(x2, mod, g, w)


def _diag_masks():
    r = lax.broadcasted_iota(jnp.int32, (TQ, TK), 0)
    c = lax.broadcasted_iota(jnp.int32, (TQ, TK), 1)
    return c <= r, c < r


def _sb_kernel(q_ref, k_ref, v_ref, o_ref, z_ref, a_ref, *, nq):
    i = pl.program_id(2)
    lane = _lane_iota()
    heads = range(HEADS_PER_STEP)
    pair = lambda hd: slice((hd // 2) * LANES, (hd // 2 + 1) * LANES)
    qs = []
    for pr in range(HEADS_PER_STEP // 2):
        q2 = q_ref[0, :, pr * LANES:(pr + 1) * LANES] * (HEAD_DIM ** -0.5)
        qs.append(jnp.where(lane < HEAD_DIM, q2, jnp.zeros_like(q2)))
        qs.append(jnp.where(lane >= HEAD_DIM, q2, jnp.zeros_like(q2)))
    tri = (lax.broadcasted_iota(jnp.int32, (TK, TK), 0)
           > lax.broadcasted_iota(jnp.int32, (TK, TK), 1)).astype(BF16)
    tri2 = jnp.concatenate([tri, tri], axis=0)
    _, strict = _diag_masks()

    for c in range(nq):
        @pl.when(i == c)
        def _():
            kv_len = (c + 1) * TK
            for hd in heads:
                z_ref[hd, :, :kv_len] = _dot_nt(qs[hd], k_ref[0, :kv_len, pair(hd)])
            carry = [jnp.zeros((TQ, 1), F32) for _ in heads]
            for j in reversed(range(c + 1)):
                sl = slice(j * TK, (j + 1) * TK)
                for hd in heads:
                    z = z_ref[hd, :, sl]
                    sp = jnp.log(1.0 + jnp.exp(-jnp.abs(z)))
                    log_beta = jnp.minimum(z, 0.0) - sp
                    neg_1m = jnp.maximum(z, 0.0) + sp
                    if j == c:
                        neg_1m = jnp.where(strict, neg_1m, 0.0)
                    hi = neg_1m.astype(BF16)
                    pieces = jnp.concatenate([hi, (neg_1m - hi.astype(F32)).astype(BF16)], axis=1)
                    later = _dot(pieces, tri2)
                    a = jnp.exp(log_beta - later - carry[hd])
                    if j == c:
                        a = jnp.where(strict, a, 0.0)
                    a_ref[hd, :, sl] = a.astype(BF16)
                    carry[hd] = carry[hd] + jnp.sum(neg_1m, axis=-1, keepdims=True)
            outs = [_dot(a_ref[hd, :, :kv_len], v_ref[0, :kv_len, pair(hd)]) for hd in heads]
            for pr in range(HEADS_PER_STEP // 2):
                o_ref[0, :, pr * LANES:(pr + 1) * LANES] = jnp.where(lane < HEAD_DIM, outs[2 * pr], outs[2 * pr + 1])


def _sb_attention(usb3):
    b, t, _ = usb3.shape
    nq = t // TQ
    wide = HEADS_PER_STEP * HEAD_DIM
    return pl.pallas_call(
        functools.partial(_sb_kernel, nq=nq),
        grid=(b, GROUP_WIDTH // wide, nq),
        in_specs=[
            pl.BlockSpec((1, TQ, wide), lambda bb, p, i: (bb, i, U_SBQ // wide + p)),
            pl.BlockSpec((1, t, wide), lambda bb, p, i: (bb, 0, U_SBK // wide + p)),
            pl.BlockSpec((1, t, wide), lambda bb, p, i: (bb, 0, U_SBV // wide + p)),
        ],
        out_specs=pl.BlockSpec((1, TQ, wide), lambda bb, p, i: (bb, i, p)),
        out_shape=jax.ShapeDtypeStruct((b, t, GROUP_WIDTH), F32),
        scratch_shapes=[pltpu.VMEM((HEADS_PER_STEP, TQ, t), F32), pltpu.VMEM((HEADS_PER_STEP, TQ, t), BF16)],
        compiler_params=_cparams("parallel", "parallel", "arbitrary"),
        name="sb_attention",
    )(usb3, usb3, usb3)


CONV_HALO = 32


def _conv_kernel(a_ref, g_ref, ah_ref, gh_ref, dww_ref, dwb_ref, lng_ref, lnb_ref, pw_ref, pwb_ref,
                 o_ref, h_ref, s_ref, *, tm):
    i = pl.program_id(1)
    h_ref[CONV_HALO:, :] = a_ref[0] * jax.nn.sigmoid(g_ref[0])
    halo = ah_ref[0] * jax.nn.sigmoid(gh_ref[0])
    h_ref[0:CONV_HALO, :] = jnp.where(i == 0, 0.0, halo)
    acc = jnp.zeros((tm, GROUP_WIDTH), F32) + dwb_ref[...]
    base = CONV_HALO - (CONV_WIDTH - 1)
    sub = 8
    for r in range(sub):
        taps = [k for k in range(CONV_WIDTH) if (base + k) % sub == r]
        if not taps:
            continue
        lo = base + taps[0]
        span = (taps[-1] - taps[0]) + tm
        s_ref[r, 0:span, :] = h_ref[lo:lo + span, :]
        for k in taps:
            acc = acc + dww_ref[k:k + 1, :] * s_ref[r, k - taps[0]:k - taps[0] + tm, :]
    mu = jnp.mean(acc, axis=-1, keepdims=True)
    cen = acc - mu
    var = jnp.mean(cen * cen, axis=-1, keepdims=True)
    hn = cen * lax.rsqrt(var + CONV_LN_EPS) * lng_ref[...] + lnb_ref[...]
    act = hn * jax.nn.sigmoid(hn)
    o_ref[0] = _dot(act.astype(BF16), pw_ref[...]) + pwb_ref[...]


def _conformer(u3, dw_w, dw_b, ln_g, ln_b, pw_w, pw_b):
    b, t, _ = u3.shape
    tm = 512
    cw = GROUP_WIDTH
    hb = tm // CONV_HALO

    def halo_map(col):
        return lambda bb, i: (bb, jnp.maximum(i * hb - 1, 0), col)

    row = lambda bb, i: (0, 0)
    return pl.pallas_call(
        functools.partial(_conv_kernel, tm=tm),
        grid=(b, t // tm),
        in_specs=[
            pl.BlockSpec((1, tm, cw), lambda bb, i: (bb, i, U_CA // cw)),
            pl.BlockSpec((1, tm, cw), lambda bb, i: (bb, i, U_CG // cw)),
            pl.BlockSpec((1, CONV_HALO, cw), halo_map(U_CA // cw)),
            pl.BlockSpec((1, CONV_HALO, cw), halo_map(U_CG // cw)),
            pl.BlockSpec((CONV_WIDTH, cw), row),
            pl.BlockSpec((1, cw), row),
            pl.BlockSpec((1, cw), row),
            pl.BlockSpec((1, cw), row),
            pl.BlockSpec((cw, cw), row),
            pl.BlockSpec((1, cw), row),
        ],
        out_specs=pl.BlockSpec((1, tm, cw), lambda bb, i: (bb, i, 0)),
        out_shape=jax.ShapeDtypeStruct((b, t, cw), F32),
        scratch_shapes=[pltpu.VMEM((tm + CONV_HALO, cw), F32), pltpu.VMEM((8, tm + CONV_HALO, cw), F32)],
        compiler_params=_cparams("parallel", "parallel"),
        name="conformer_conv",
    )(u3, u3, u3, u3, dw_w, dw_b.reshape(1, cw), ln_g.reshape(1, cw), ln_b.reshape(1, cw),
      pw_w.astype(BF16), pw_b.reshape(1, cw))


def _head_norm_rope(x, gain, cos, sin_signed, seg_mean):
    ms = _split_dot(x * x, seg_mean)
    y = x * lax.rsqrt(ms + EPS) * gain
    lane = _lane_iota()
    half = HEAD_DIM // 2
    first = (lane & (HEAD_DIM - 1)) < half
    partner = jnp.where(first, pltpu.roll(y, LANES - half, 1), pltpu.roll(y, half, 1))
    return y * cos + partner * sin_signed


def _dup(x, g):
    lane = _lane_iota()
    sw = pltpu.roll(x, HEAD_DIM, 1)
    if g == 0:
        return jnp.where(lane < HEAD_DIM, x, sw)
    return jnp.where(lane < HEAD_DIM, sw, x)


def _nsa_prep_kernel(q_ref, kv_ref, gk_ref, cos_ref, sin_ref, qg_ref, kg_ref,
                     qn_ref, kc_ref, kse_ref, kso_ref, vs_ref, kw_ref, vw_ref, gt_ref, *, per_seq):
    cos = cos_ref[...]
    sin = sin_ref[...]
    seg = lax.broadcasted_iota(jnp.int32, (LANES, LANES), 0) >> HEAD_SHIFT
    seg_mean = jnp.where(seg == lax.broadcasted_iota(jnp.int32, (LANES, LANES), 1) >> HEAD_SHIFT,
                         1.0 / HEAD_DIM, 0.0).astype(BF16)
    scale = HEAD_DIM ** -0.5
    for p in range(GROUP_WIDTH // LANES):
        x = q_ref[:, p * LANES:(p + 1) * LANES]
        qn_ref[:, p * LANES:(p + 1) * LANES] = (
            _head_norm_rope(x, qg_ref[...], cos, sin, seg_mean) * scale).astype(BF16)

    def blk(n):
        return kv_ref[:, n * LANES:(n + 1) * LANES]

    kc_ref[...] = _head_norm_rope(blk(0), kg_ref[0:1, :], cos, sin, seg_mean)
    ks = _head_norm_rope(blk(2), kg_ref[1:2, :], cos, sin, seg_mean)
    kw = _head_norm_rope(blk(4), kg_ref[2:3, :], cos, sin, seg_mean)
    vs = blk(3)
    vw = blk(5)
    tm = ks.shape[0]
    lane = _lane_iota()
    t = (pl.program_id(0) % per_seq) * tm + lax.broadcasted_iota(jnp.int32, (tm, 1), 0)
    sblk = t >> SEL_SHIFT
    hot_lo = jnp.where(lane == sblk, 1.0, 0.0)
    hot_hi = jnp.where(lane - HEAD_DIM == sblk, 1.0, 0.0)
    ks_sw = pltpu.roll(ks, HEAD_DIM, 1)
    for g in range(2):
        sl = slice(g * LANES, (g + 1) * LANES)
        kse_ref[:, sl] = jnp.where(lane < HEAD_DIM, ks if g == 0 else ks_sw, hot_hi).astype(BF16)
        kso_ref[:, sl] = jnp.where(lane >= HEAD_DIM, ks_sw if g == 0 else ks, hot_lo).astype(BF16)
        vs_ref[:, sl] = _dup(vs, g).astype(BF16)
        kw_ref[:, sl] = _dup(kw, g).astype(BF16)
        vw_ref[:, sl] = _dup(vw, g).astype(BF16)
    gates = jax.nn.sigmoid(gk_ref[...])
    gt_ref[:, 0:LANES] = gates
    gt_ref[:, LANES:2 * LANES] = pltpu.roll(gates, LANES - 12, 1)


def _nsa_prep(u2, cos, sin, q_gain, k_gain, seq):
    m_rows = u2.shape[0]
    tm = 512
    per_seq = seq // tm
    rowmap = lambda i: (i, 0)
    tab = lambda i: (i % per_seq, 0)
    const = lambda i: (0, 0)
    outs = [
        jax.ShapeDtypeStruct((m_rows, GROUP_WIDTH), BF16),
        jax.ShapeDtypeStruct((m_rows, LANES), F32),
        jax.ShapeDtypeStruct((m_rows, 2 * LANES), BF16),
        jax.ShapeDtypeStruct((m_rows, 2 * LANES), BF16),
        jax.ShapeDtypeStruct((m_rows, 2 * LANES), BF16),
        jax.ShapeDtypeStruct((m_rows, 2 * LANES), BF16),
        jax.ShapeDtypeStruct((m_rows, 2 * LANES), BF16),
        jax.ShapeDtypeStruct((m_rows, 2 * LANES), F32),
    ]
    return pl.pallas_call(
        functools.partial(_nsa_prep_kernel, per_seq=per_seq),
        grid=(m_rows // tm,),
        in_specs=[
            pl.BlockSpec((tm, GROUP_WIDTH), lambda i: (i, U_NQ // GROUP_WIDTH)),
            pl.BlockSpec((tm, 6 * LANES), lambda i: (i, U_NKV // (6 * LANES))),
            pl.BlockSpec((tm, LANES), lambda i: (i, (U_MLA + MLA_Q_RANK + MLA_KV_RANK) // LANES)),
            pl.BlockSpec((tm, LANES), tab),
            pl.BlockSpec((tm, LANES), tab),
            pl.BlockSpec((1, LANES), const),
            pl.BlockSpec((3, LANES), const),
        ],
        out_specs=[
            pl.BlockSpec((tm, GROUP_WIDTH), rowmap),
            pl.BlockSpec((tm, LANES), rowmap),
            pl.BlockSpec((tm, 2 * LANES), rowmap),
            pl.BlockSpec((tm, 2 * LANES), rowmap),
            pl.BlockSpec((tm, 2 * LANES), rowmap),
            pl.BlockSpec((tm, 2 * LANES), rowmap),
            pl.BlockSpec((tm, 2 * LANES), rowmap),
            pl.BlockSpec((tm, 2 * LANES), rowmap),
        ],
        out_shape=outs,
        compiler_params=_cparams("parallel"),
        name="nsa_prep",
    )(u2, u2, u2, cos, sin, q_gain, k_gain)


def _compress_kernel(xk_ref, xv_ref, pek_ref, pev_ref, wk_ref, wv_ref, kc_ref, vc_ref, *, nrow):
    stride = NSA_CMP_STRIDE

    def comp(x_ref, pe_ref, w_ref):
        first = jnp.zeros((nrow, LANES), F32)
        second = jnp.zeros((nrow, LANES), F32)
        for r in range(stride):
            x = x_ref[0, pl.ds(r, nrow, stride=stride), :]
            first = first + _dot((x + pe_ref[r:r + 1, :]).astype(BF16), w_ref[r])
            second = second + _dot((x + pe_ref[stride + r:stride + r + 1, :]).astype(BF16), w_ref[stride + r])
        return first + pltpu.roll(second, nrow - 1, 0)

    ck = comp(xk_ref, pek_ref, wk_ref)
    cv = comp(xv_ref, pev_ref, wv_ref)
    for g in range(2):
        kc_ref[0, g] = _dup(ck, g).astype(BF16)
        vc_ref[0, g] = _dup(cv, g).astype(BF16)


def _compress(kc3, u3, pe_k, pe_v, wk, wv):
    b, t, _ = kc3.shape
    nrow = t // NSA_CMP_STRIDE
    const2 = lambda bb: (0, 0)
    const3 = lambda bb: (0, 0, 0)
    out = jax.ShapeDtypeStruct((b, 2, nrow, LANES), BF16)
    return pl.pallas_call(
        functools.partial(_compress_kernel, nrow=nrow),
        grid=(b,),
        in_specs=[
            pl.BlockSpec((1, t, LANES), lambda bb: (bb, 0, 0)),
            pl.BlockSpec((1, t, LANES), lambda bb: (bb, 0, U_NKV // LANES + 1)),
            pl.BlockSpec(pe_k.shape, const2),
            pl.BlockSpec(pe_v.shape, const2),
            pl.BlockSpec(wk.shape, const3),
            pl.BlockSpec(wv.shape, const3),
        ],
        out_specs=[pl.BlockSpec((1, 2, nrow, LANES), lambda bb: (bb, 0, 0, 0))] * 2,
        out_shape=[out, out],
        compiler_params=_cparams("parallel"),
        name="nsa_compress",
    )(kc3, u3, pe_k, pe_v, wk, wv)


def _attend_many(jobs, scale=None):
    tiles = [k.shape[0] // TK for (_, k, _, _, _, _) in jobs]
    for (q, k, _, s_ref, _, _), n in zip(jobs, tiles):
        s_ref[:, :n * TK] = _dot_nt(q, k)
    maxima = []
    for (_, _, _, s_ref, _, masks), n in zip(jobs, tiles):
        fold = None
        for j in range(n):
            sl = slice(j * TK, (j + 1) * TK)
            s = s_ref[:, sl]
            if j in masks:
                s = jnp.where(masks[j], s, NEG)
                s_ref[:, sl] = s
            f = jnp.maximum(s[:, :LANES], s[:, LANES:])
            fold = f if fold is None else jnp.maximum(fold, f)
        maxima.append(jnp.max(fold, axis=-1, keepdims=True))
    sums = []
    for (_, _, _, s_ref, p_ref, _), n, m in zip(jobs, tiles, maxima):
        fold = None
        for j in range(n):
            sl = slice(j * TK, (j + 1) * TK)
            p = jnp.exp2((s_ref[:, sl] - m) * (LOG2E if scale is None else scale * LOG2E))
            f = p[:, :LANES] + p[:, LANES:]
            fold = f if fold is None else fold + f
            p_ref[:, sl] = p.astype(BF16)
        sums.append(jnp.sum(fold, axis=-1, keepdims=True))
    return [_dot(p_ref[:, :n * TK], v) / l for (_, _, v, _, p_ref, _), n, l in zip(jobs, tiles, sums)]


def _nsa_kernel(q_ref, kc_ref, vc_ref, kse_ref, kso_ref, vs_ref, kw_ref, vw_ref, gt_ref, ov_ref,
                o_ref, s_ref, p_ref, sw_ref, pw_ref, *, nq, n_cmp, n_sblk, n_sel):
    i = pl.program_id(2)
    lane = _lane_iota()
    row = i * TQ + lax.broadcasted_iota(jnp.int32, (TQ, 1), 0)
    rep = 4
    q2s = [q_ref[0, :, p * LANES:(p + 1) * LANES] for p in range(2)]
    qs = []
    for q2 in q2s:
        qs.append(jnp.where(lane < HEAD_DIM, q2, jnp.zeros_like(q2)))
        qs.append(jnp.where(lane >= HEAD_DIM, q2, jnp.zeros_like(q2)))

    kc = kc_ref[0, 0]
    vc = vc_ref[0, 0]
    cmask = ((lane * NSA_CMP_STRIDE + (NSA_CMP_LEN - 1)) <= row) & (lane < n_cmp)
    psum = jnp.zeros((TQ, LANES), F32)
    s_cmp = [jnp.where(cmask, _dot_nt(qs[r], kc), NEG) for r in range(rep)]
    p_cmp = []
    for r in range(rep):
        mx = jnp.max(s_cmp[r], axis=-1, keepdims=True)
        p = jnp.where(cmask, jnp.exp(s_cmp[r] - mx), 0.0)
        p = p / jnp.maximum(jnp.sum(p, axis=-1, keepdims=True), 1e-30)
        p_cmp.append(p.astype(BF16))
        psum = psum + p
    o_cmp = [_dot(p_cmp[r], vc) for r in range(rep)]
    ov_t = ov_ref[0:n_sblk, :]
    p_hi = psum.astype(BF16)
    p_lo = (psum - p_hi.astype(F32)).astype(BF16)
    imp = _dot_nt(ov_t, p_hi) + _dot_nt(ov_t, p_lo)
    cur = (i * TQ + _lane_iota(TQ)) >> SEL_SHIFT
    blk = lax.broadcasted_iota(jnp.int32, (n_sblk, 1), 0)
    eligible = blk <= cur
    forced = (blk == 0) | (blk == cur) | (blk == cur - 1)
    score = jnp.where(eligible, imp + jnp.where(forced, NSA_FORCE_BONUS, 0.0), NEG)
    rank = jnp.zeros((n_sblk, TQ), F32)
    for sp in range(n_sblk):
        other = score[sp:sp + 1, :]
        beats = (other > score) | ((other == score) & (blk > sp))
        rank = rank + jnp.where(beats, 1.0, 0.0)
    bias_t = jnp.where(eligible & (rank < n_sel), 0.0, NEG)

    bias_lo = jnp.concatenate([bias_t, jnp.zeros((LANES - n_sblk, TQ), F32)], axis=0).T
    bias_hi = pltpu.roll(bias_lo, HEAD_DIM, 1).astype(BF16)
    bias_lo = bias_lo.astype(BF16)
    q_aug = []
    for q2 in q2s:
        q_aug.append(jnp.where(lane < HEAD_DIM, q2, bias_hi))
        q_aug.append(jnp.where(lane >= HEAD_DIM, q2, bias_lo))
    causal, strict = _diag_masks()
    wtiles = NSA_WINDOW // TK
    gt = gt_ref[0]

    for c in range(nq):
        @pl.when(i == c)
        def _():
            kv_len = (c + 1) * TK
            lo = max(c - wtiles, 0) * TK
            wmasks = {c - max(c - wtiles, 0): causal}
            if c >= wtiles:
                wmasks[0] = jnp.logical_not(causal)
            jobs = []
            for r in range(rep):
                ks_ref = kse_ref if r % 2 == 0 else kso_ref
                jobs.append((q_aug[r], ks_ref[0, :kv_len, :], vs_ref[0, :kv_len, :],
                             s_ref.at[r], p_ref.at[r], {c: causal}))
            for r in range(rep):
                jobs.append((qs[r], kw_ref[0, lo:kv_len, :], vw_ref[0, lo:kv_len, :],
                             sw_ref.at[r], pw_ref.at[r], wmasks))
            res = _attend_many(jobs)
            outs = []
            for r in range(rep):
                outs.append(gt[:, 3 * r:3 * r + 1] * o_cmp[r] + gt[:, 3 * r + 1:3 * r + 2] * res[r]
                            + gt[:, 3 * r + 2:3 * r + 3] * res[rep + r])
            o_ref[0, :, 0:LANES] = jnp.where(lane < HEAD_DIM, outs[0], outs[1])
            o_ref[0, :, LANES:2 * LANES] = jnp.where(lane < HEAD_DIM, outs[2], outs[3])


def _nsa_attention(qn, kcmp, vcmp, kse, kso, vsd, kwd, vwd, gates, overlap):
    b, t, _ = qn.shape
    nq = t // TQ
    n_cmp = (t - NSA_CMP_LEN) // NSA_CMP_STRIDE + 1
    n_sblk = t // NSA_SEL_LEN
    nrow = kcmp.shape[2]
    wlen = NSA_WINDOW + TK
    kv = lambda bb, g, i: (bb, 0, g)
    cm = lambda bb, g, i: (bb, g, 0, 0)
    return pl.pallas_call(
        functools.partial(_nsa_kernel, nq=nq, n_cmp=n_cmp, n_sblk=n_sblk, n_sel=min(NSA_N_SEL, n_sblk)),
        grid=(b, 2, nq),
        in_specs=[
            pl.BlockSpec((1, TQ, 2 * LANES), lambda bb, g, i: (bb, i, g)),
            pl.BlockSpec((1, 1, nrow, LANES), cm),
            pl.BlockSpec((1, 1, nrow, LANES), cm),
            pl.BlockSpec((1, t, LANES), kv),
            pl.BlockSpec((1, t, LANES), kv),
            pl.BlockSpec((1, t, LANES), kv),
            pl.BlockSpec((1, t, LANES), kv),
            pl.BlockSpec((1, t, LANES), kv),
            pl.BlockSpec((1, TQ, LANES), lambda bb, g, i: (bb, i, g)),
            pl.BlockSpec((LANES, LANES), lambda bb, g, i: (0, 0)),
        ],
        out_specs=pl.BlockSpec((1, TQ, 2 * LANES), lambda bb, g, i: (bb, i, g)),
        out_shape=jax.ShapeDtypeStruct((b, t, GROUP_WIDTH), F32),
        scratch_shapes=[pltpu.VMEM((4, TQ, t), F32), pltpu.VMEM((4, TQ, t), BF16),
                        pltpu.VMEM((4, TQ, wlen), F32), pltpu.VMEM((4, TQ, wlen), BF16)],
        compiler_params=_cparams("parallel", "parallel", "arbitrary"),
        name="nsa_attention",
    )(qn, kcmp, vcmp, kse, kso, vsd, kwd, vwd, gates, overlap)


def _mla_prep_kernel(u_ref, cos_ref, sin_ref, qlg_ref, kvlg_ref, wq_ref, wk_ref, wv_ref, qg_ref, kg_ref,
                     q_ref, k_ref, v_ref):
    lane = _lane_iota()
    cos = cos_ref[...]
    sin = sin_ref[...]
    half = MLA_ROPE // 2

    def lat_norm(x, g):
        ms = jnp.mean(x * x, axis=-1, keepdims=True)
        return (x * lax.rsqrt(ms + EPS) * g).astype(BF16)

    def head_norm_rope(x, g):
        ms = jnp.sum(x * x, axis=-1, keepdims=True) * (1.0 / MLA_QK)
        y = x * lax.rsqrt(ms + EPS) * g
        partner = jnp.where(lane < MLA_NOPE + half, pltpu.roll(y, LANES - half, 1), pltpu.roll(y, half, 1))
        return y * cos + partner * sin

    q = _dot(lat_norm(u_ref[:, 0:MLA_Q_RANK], qlg_ref[...]), wq_ref[...])
    kvn = lat_norm(u_ref[:, MLA_Q_RANK:MLA_Q_RANK + MLA_KV_RANK], kvlg_ref[...])
    kn = _dot(kvn, wk_ref[...])
    v_ref[...] = _dot(kvn, wv_ref[...]).astype(BF16)
    gk = u_ref[:, MLA_Q_RANK + MLA_KV_RANK:MLA_Q_RANK + MLA_KV_RANK + LANES]
    k_rope = jnp.where((lane >= MLA_NOPE) & (lane < MLA_QK), gk, 0.0)
    for h in range(MLA_HEADS):
        sl = slice(h * LANES, (h + 1) * LANES)
        q_ref[:, sl] = head_norm_rope(q[:, sl], qg_ref[...]).astype(BF16)
        k_ref[:, sl] = head_norm_rope(kn[:, sl] + k_rope, kg_ref[...]).astype(BF16)


def _mla_prep(u2, cos, sin, q_lat_gain, kv_lat_gain, wq, wk, wv, q_gain, k_gain, seq):
    m_rows = u2.shape[0]
    tm = 512
    per_seq = seq // tm
    width = MLA_Q_RANK + MLA_KV_RANK + LANES
    rowmap = lambda i: (i, 0)
    tab = lambda i: (i % per_seq, 0)
    const = lambda i: (0, 0)
    return pl.pallas_call(
        _mla_prep_kernel,
        grid=(m_rows // tm,),
        in_specs=[
            pl.BlockSpec((tm, width), lambda i: (i, U_MLA // width)),
            pl.BlockSpec((tm, LANES), tab),
            pl.BlockSpec((tm, LANES), tab),
            pl.BlockSpec((1, MLA_Q_RANK), const),
            pl.BlockSpec((1, MLA_KV_RANK), const),
            pl.BlockSpec(wq.shape, const),
            pl.BlockSpec(wk.shape, const),
            pl.BlockSpec(wv.shape, const),
            pl.BlockSpec((1, LANES), const),
            pl.BlockSpec((1, LANES), const),
        ],
        out_specs=[
            pl.BlockSpec((tm, MLA_HEADS * LANES), rowmap),
            pl.BlockSpec((tm, MLA_HEADS * LANES), rowmap),
            pl.BlockSpec((tm, GROUP_WIDTH), rowmap),
        ],
        out_shape=[
            jax.ShapeDtypeStruct((m_rows, MLA_HEADS * LANES), BF16),
            jax.ShapeDtypeStruct((m_rows, MLA_HEADS * LANES), BF16),
            jax.ShapeDtypeStruct((m_rows, GROUP_WIDTH), BF16),
        ],
        compiler_params=_cparams("parallel"),
        name="mla_prep",
    )(u2, cos, sin, q_lat_gain, kv_lat_gain, wq, wk, wv, q_gain, k_gain)


def _mla_kernel(q_ref, k_ref, v_ref, o_ref, s_ref, p_ref, *, nq):
    i = pl.program_id(2)
    lane = _lane_iota()
    causal, _ = _diag_masks()
    scale = MLA_QK ** -0.5
    for c in range(nq):
        @pl.when(i == c)
        def _():
            kv_len = (c + 1) * TK
            jobs = []
            for hd in range(HEADS_PER_STEP):
                sl = slice(hd * LANES, (hd + 1) * LANES)
                vs = slice((hd // 2) * LANES, (hd // 2 + 1) * LANES)
                jobs.append((q_ref[0, :, sl], k_ref[0, :kv_len, sl], v_ref[0, :kv_len, vs],
                             s_ref.at[hd], p_ref.at[hd], {c: causal}))
            outs = _attend_many(jobs, scale=scale)
            for pr in range(HEADS_PER_STEP // 2):
                o_ref[0, :, pr * LANES:(pr + 1) * LANES] = jnp.where(lane < HEAD_DIM, outs[2 * pr], outs[2 * pr + 1])


def _mla_attention(qm, km, vm):
    b, t, _ = qm.shape
    nq = t // TQ
    return pl.pallas_call(
        functools.partial(_mla_kernel, nq=nq),
        grid=(b, MLA_HEADS // HEADS_PER_STEP, nq),
        in_specs=[
            pl.BlockSpec((1, TQ, HEADS_PER_STEP * LANES), lambda bb, p, i: (bb, i, p)),
            pl.BlockSpec((1, t, HEADS_PER_STEP * LANES), lambda bb, p, i: (bb, 0, p)),
            pl.BlockSpec((1, t, HEADS_PER_STEP * HEAD_DIM), lambda bb, p, i: (bb, 0, p)),
        ],
        out_specs=pl.BlockSpec((1, TQ, HEADS_PER_STEP * HEAD_DIM), lambda bb, p, i: (bb, i, p)),
        out_shape=jax.ShapeDtypeStruct((b, t, GROUP_WIDTH), F32),
        scratch_shapes=[pltpu.VMEM((HEADS_PER_STEP, TQ, t), F32), pltpu.VMEM((HEADS_PER_STEP, TQ, t), BF16)],
        compiler_params=_cparams("parallel", "parallel", "arbitrary"),
        name="mla_attention",
    )(qm, km, vm)


def _out_kernel(a_ref, b_ref, c_ref, d_ref, gn_ref, w_ref, x_ref, mod_ref, o_ref):
    acc = None
    for gi, r in enumerate((a_ref, b_ref, c_ref, d_ref)):
        y = r[...]
        ms = jnp.mean(y * y, axis=-1, keepdims=True)
        sl = slice(gi * GROUP_WIDTH, (gi + 1) * GROUP_WIDTH)
        part = _dot((y * lax.rsqrt(ms + EPS) * gn_ref[:, sl]).astype(BF16), w_ref[sl, :])
        acc = part if acc is None else acc + part
    o_ref[...] = x_ref[...] + mod_ref[0, 2:3, :] * acc


def _out_proj(parts, gn, w, layer, x2, mod, seq):
    m_rows, d = x2.shape
    tm = 512
    per_seq = seq // tm
    part = pl.BlockSpec((tm, GROUP_WIDTH), lambda i: (i, 0))
    return pl.pallas_call(
        _out_kernel,
        grid=(m_rows // tm,),
        in_specs=[
            part, part, part, part,
            pl.BlockSpec((1, 4 * GROUP_WIDTH), lambda i: (0, 0)),
            pl.BlockSpec((None, 4 * GROUP_WIDTH, d), lambda i: (layer, 0, 0)),
            pl.BlockSpec((tm, d), lambda i: (i, 0)),
            pl.BlockSpec((1, 6, d), lambda i: (i // per_seq, 0, 0)),
        ],
        out_specs=pl.BlockSpec((tm, d), lambda i: (i, 0)),
        out_shape=jax.ShapeDtypeStruct((m_rows, d), F32),
        compiler_params=_cparams("parallel"),
        name="out_proj",
    )(*parts, gn, w, x2, mod)


FFN_HALO = 16


def _ffn_kernel(x_ref, xh_ref, mod_ref, g_ref, wa_ref, wg_ref, cwa_ref, cwg_ref, cba_ref, cbg_ref, wd_ref,
                o_ref, h_ref, acc_ref, *, per_seq):
    i = pl.program_id(0)
    j = pl.program_id(1)
    m = mod_ref[0]

    @pl.when(j == 0)
    def _():
        h_ref[FFN_HALO:, :] = _modulated_norm(x_ref[...], g_ref[...], m[3:4], m[4:5]).astype(BF16)
        halo = _modulated_norm(xh_ref[...], g_ref[...], m[3:4], m[4:5])
        h_ref[0:FFN_HALO, :] = jnp.where(i % per_seq == 0, 0.0, halo).astype(BF16)
        acc_ref[...] = jnp.zeros_like(acc_ref)

    h = h_ref[...]

    def conv(u, cw_ref, cb_ref):
        y = cw_ref[2:3, :] * u + cw_ref[1:2, :] * pltpu.roll(u, 1, 0) + cw_ref[0:1, :] * pltpu.roll(u, 2, 0)
        return y[FFN_HALO:, :] + cb_ref[...]

    ya = conv(_dot(h, wa_ref[...]), cwa_ref, cba_ref)
    yg = conv(_dot(h, wg_ref[...]), cwg_ref, cbg_ref)
    act = (yg * jax.nn.sigmoid(yg)) * ya
    acc_ref[...] += _dot(act.astype(BF16), wd_ref[...])

    @pl.when(j == pl.num_programs(1) - 1)
    def _():
        o_ref[...] = x_ref[...] + m[5:6] * acc_ref[...]


def _ffn(x2, mod, g, w_up, conv_w, conv_b, w_down, layer, seq):
    m_rows, d = x2.shape
    d_ff = w_down.shape[1]
    tm, tn = 512, 512
    per_seq = seq // tm
    nj = d_ff // tn
    hb = tm // FFN_HALO
    cb = conv_b.reshape(1, 2 * d_ff)
    return pl.pallas_call(
        functools.partial(_ffn_kernel, per_seq=per_seq),
        grid=(m_rows // tm, nj),
        in_specs=[
            pl.BlockSpec((tm, d), lambda i, j: (i, 0)),
            pl.BlockSpec((FFN_HALO, d), lambda i, j: (jnp.maximum(i * hb - 1, 0), 0)),
            pl.BlockSpec((1, 6, d), lambda i, j: (i // per_seq, 0, 0)),
            pl.BlockSpec((1, d), lambda i, j: (0, 0)),
            pl.BlockSpec((None, d, tn), lambda i, j: (layer, 0, j)),
            pl.BlockSpec((None, d, tn), lambda i, j: (layer, 0, nj + j)),
            pl.BlockSpec((3, tn), lambda i, j: (0, j)),
            pl.BlockSpec((3, tn), lambda i, j: (0, nj + j)),
            pl.BlockSpec((1, tn), lambda i, j: (0, j)),
            pl.BlockSpec((1, tn), lambda i, j: (0, nj + j)),
            pl.BlockSpec((None, tn, d), lambda i, j: (layer, j, 0)),
        ],
        out_specs=pl.BlockSpec((tm, d), lambda i, j: (i, 0)),
        out_shape=jax.ShapeDtypeStruct((m_rows, d), F32),
        scratch_shapes=[pltpu.VMEM((tm + FFN_HALO, d), BF16), pltpu.VMEM((tm, d), F32)],
        compiler_params=_cparams("parallel", "arbitrary"),
        name="conv_glu_ffn",
    )(x2, x2, mod, g, w_up, w_up, conv_w, conv_w, cb, cb, w_down)


def _reorder_w_in(w):
    n_gate = 24
    off_mla = REF_ALIGNED_COLS + n_gate
    lat = MLA_Q_RANK + MLA_KV_RANK
    z = lambda n: jnp.zeros(w.shape[:-1] + (n,), w.dtype)
    out = jnp.concatenate([
        w[..., :REF_ALIGNED_COLS], w[..., off_mla:off_mla + lat],
        w[..., REF_ALIGNED_COLS:off_mla], z(MLA_NOPE - n_gate),
        w[..., off_mla + lat:off_mla + lat + MLA_ROPE], z(LANES - MLA_QK)], axis=-1)
    assert out.shape[-1] == U_COLS
    return out.astype(BF16)


def _rope_tables(seq, dim, lane_cos, lane_sin):
    inv = ROPE_THETA ** (-jnp.arange(0, dim, 2, dtype=F32) / dim)
    ang = jnp.arange(seq).astype(F32)[:, None] * inv[None, :]
    cos, sin = jnp.cos(ang), jnp.sin(ang)
    return lane_cos(cos), lane_sin(sin)


def _nsa_tables(seq):
    return _rope_tables(seq, HEAD_DIM,
                        lambda c: jnp.concatenate([c, c, c, c], axis=1),
                        lambda s: jnp.concatenate([-s, s, -s, s], axis=1))


def _mla_tables(seq):
    ones = jnp.ones((seq, MLA_NOPE), F32)
    zeros = jnp.zeros((seq, MLA_NOPE), F32)
    pad1 = jnp.ones((seq, LANES - MLA_QK), F32)
    pad0 = jnp.zeros((seq, LANES - MLA_QK), F32)
    return _rope_tables(seq, MLA_ROPE,
                        lambda c: jnp.concatenate([ones, c, c, pad1], axis=1),
                        lambda s: jnp.concatenate([zeros, -s, s, pad0], axis=1))


def _overlap_matrix(seq):
    n_cmp = (seq - NSA_CMP_LEN) // NSA_CMP_STRIDE + 1
    n_sblk = seq // NSA_SEL_LEN
    starts = np.arange(n_cmp) * NSA_CMP_STRIDE
    sel_start = np.arange(n_sblk) * NSA_SEL_LEN
    ov = np.clip(np.minimum(starts[:, None] + NSA_CMP_LEN, sel_start[None, :] + NSA_SEL_LEN)
                 - np.maximum(starts[:, None], sel_start[None, :]), 0, None) / NSA_CMP_LEN
    full = np.zeros((LANES, LANES), np.float32)
    full[:n_sblk, :n_cmp] = ov.T
    return jnp.asarray(full, BF16)


def _pad_lanes(v, n):
    return jnp.pad(v, (0, n - v.shape[0]))


def _mixer(x2, mod, batch, seq, p, w_in_all, w_o_all, layer):
    m_rows = x2.shape[0]
    usb, u2 = _in_proj(x2, mod, p["norm_mix"].reshape(1, -1), w_in_all, layer, seq)
    u3 = u2.reshape(batch, seq, U_COLS - U_SB_COLS)

    o_sb = _sb_attention(usb.reshape(batch, seq, U_SB_COLS))
    o_conv = _conformer(u3, p["conv_dw_w"], p["conv_dw_b"], p["conv_ln_g"], p["conv_ln_b"],
                        p["conv_pw_w"], p["conv_pw_b"])

    cos_n, sin_n = _nsa_tables(seq)
    q_gain = jnp.tile(p["nsa_q_norm"], 2).reshape(1, LANES)
    k_gain = jnp.tile(p["nsa_k_norm"], (1, 2))
    qn, kc, kse, kso, vsd, kwd, vwd, gates = _nsa_prep(u2, cos_n, sin_n, q_gain, k_gain, seq)
    pe2 = jnp.tile(p["nsa_cmp_pe"], (1, 1, 2))
    w_l = p["nsa_cmp_w"].reshape(2, NSA_CMP_LEN, HEAD_DIM, HEAD_DIM)
    zero = jnp.zeros_like(w_l)
    w_bd = jnp.concatenate([jnp.concatenate([w_l, zero], axis=-1),
                            jnp.concatenate([zero, w_l], axis=-1)], axis=-2).astype(BF16)
    kcmp, vcmp = _compress(kc.reshape(batch, seq, LANES), u3, pe2[0], pe2[1], w_bd[0], w_bd[1])
    o_nsa = _nsa_attention(qn.reshape(batch, seq, -1), kcmp, vcmp,
                           kse.reshape(batch, seq, -1), kso.reshape(batch, seq, -1), vsd.reshape(batch, seq, -1),
                           kwd.reshape(batch, seq, -1), vwd.reshape(batch, seq, -1),
                           gates.reshape(batch, seq, -1), _overlap_matrix(seq))

    cos_m, sin_m = _mla_tables(seq)
    w_uq = p["mla_w_uq"].reshape(MLA_Q_RANK, MLA_HEADS, MLA_QK)
    wq = jnp.pad(w_uq, ((0, 0), (0, 0), (0, LANES - MLA_QK))).reshape(MLA_Q_RANK, MLA_HEADS * LANES).astype(BF16)
    w_ukv = p["mla_w_ukv"].reshape(MLA_KV_RANK, MLA_HEADS, 2 * HEAD_DIM)
    wk_m = jnp.pad(w_ukv[:, :, :MLA_NOPE], ((0, 0), (0, 0), (0, LANES - MLA_NOPE)))
    wk_m = wk_m.reshape(MLA_KV_RANK, MLA_HEADS * LANES).astype(BF16)
    wv_m = w_ukv[:, :, MLA_NOPE:].reshape(MLA_KV_RANK, GROUP_WIDTH).astype(BF16)
    qm, km, vm = _mla_prep(u2, cos_m, sin_m, p["mla_q_lat_norm"].reshape(1, -1),
                           p["mla_kv_lat_norm"].reshape(1, -1), wq, wk_m, wv_m,
                           _pad_lanes(p["mla_q_norm"], LANES).reshape(1, LANES),
                           _pad_lanes(p["mla_k_norm"], LANES).reshape(1, LANES), seq)
    o_mla = _mla_attention(qm.reshape(batch, seq, -1), km.reshape(batch, seq, -1), vm.reshape(batch, seq, -1))

    parts = [o.reshape(m_rows, GROUP_WIDTH) for o in (o_sb, o_conv, o_nsa, o_mla)]
    return _out_proj(parts, p["group_norm"].reshape(1, -1), w_o_all, layer, x2, mod, seq)


def kernel(x, c, ada_w, ada_b, norm_mix, norm_ffn, w_in, conv_dw_w, conv_dw_b, conv_ln_g, conv_ln_b, conv_pw_w, conv_pw_b, nsa_q_norm, nsa_k_norm, nsa_cmp_pe, nsa_cmp_w, mla_q_lat_norm, mla_kv_lat_norm, mla_w_uq, mla_w_ukv, mla_q_norm, mla_k_norm, group_norm, w_o, ffn_up, ffn_conv_w, ffn_conv_b, ffn_down):
    batch, seq, d = x.shape
    depth = ada_w.shape[0]
    per_layer = dict(
        norm_mix=norm_mix, conv_dw_w=conv_dw_w, conv_dw_b=conv_dw_b, conv_ln_g=conv_ln_g,
        conv_ln_b=conv_ln_b, conv_pw_w=conv_pw_w, conv_pw_b=conv_pw_b, nsa_q_norm=nsa_q_norm,
        nsa_k_norm=nsa_k_norm, nsa_cmp_pe=nsa_cmp_pe, nsa_cmp_w=nsa_cmp_w, mla_q_lat_norm=mla_q_lat_norm,
        mla_kv_lat_norm=mla_kv_lat_norm, mla_w_uq=mla_w_uq, mla_w_ukv=mla_w_ukv, mla_q_norm=mla_q_norm,
        mla_k_norm=mla_k_norm, group_norm=group_norm)
    w_in_all = _reorder_w_in(w_in)
    w_o_all = w_o.astype(BF16)
    ffn_up_all = ffn_up.astype(BF16)
    ffn_down_all = ffn_down.astype(BF16)
    mods = _ada(c, ada_w, ada_b)
    x2 = x.reshape(batch * seq, d)
    for l in range(depth):
        p = {k: v[l] for k, v in per_layer.items()}
        x2 = _mixer(x2, mods[l], batch, seq, p, w_in_all, w_o_all, l)
        x2 = _ffn(x2, mods[l], norm_ffn[l].reshape(1, -1), ffn_up_all, ffn_conv_w[l], ffn_conv_b[l],
                  ffn_down_all, l, seq)
    return x2.reshape(batch, seq, d)
```

```python
import functools

import numpy as np
import jax
import jax.numpy as jnp
from jax import lax
from jax.experimental import pallas as pl
from jax.experimental.pallas import tpu as pltpu

F32 = jnp.float32
BF16 = jnp.bfloat16

LANES = 128
VMEM_LIMIT = 56 * 1024 * 1024

HEAD_DIM = 64
HEAD_SHIFT = 6
ROPE_THETA = 10000.0
EPS = 1e-6
GROUP_WIDTH = 512
CONV_WIDTH = 31
CONV_LN_EPS = 1e-5
NSA_CMP_LEN = 32
NSA_CMP_STRIDE = 16
NSA_SEL_LEN = 64
SEL_SHIFT = 6
NSA_N_SEL = 16
NSA_WINDOW = 512
NSA_FORCE_BONUS = 1e3
MLA_Q_RANK = 384
MLA_KV_RANK = 256
MLA_NOPE = 64
MLA_ROPE = 32
MLA_QK = MLA_NOPE + MLA_ROPE
MLA_HEADS = 8
HEADS_PER_STEP = 4

U_SBQ, U_SBK, U_SBV = 0, 512, 1024
U_SB_COLS = 1536
U_CA, U_CG = 0, 512
U_NQ = 1024
U_NKV = 1536
U_MLA = 2304
U_COLS = 4608
REF_ALIGNED_COLS = 3840

TQ = 256
TK = 256
NEG = -1e30
LOG2E = 1.4426950408889634


def _cparams(*sem):
    return pltpu.CompilerParams(dimension_semantics=sem, vmem_limit_bytes=VMEM_LIMIT)


def _dot(a, b):
    return jnp.dot(a, b, preferred_element_type=F32)


def _dot_nt(a, b):
    return lax.dot_general(a, b, (((1,), (1,)), ((), ())), preferred_element_type=F32)


def _split_dot(x, w):
    hi = x.astype(BF16)
    lo = (x - hi.astype(F32)).astype(BF16)
    return _dot(hi, w) + _dot(lo, w)


def _lane_iota(n=LANES):
    return lax.broadcasted_iota(jnp.int32, (1, n), 1)


def _ada_kernel(c_ref, w_ref, b_ref, o_ref):
    c = c_ref[...]
    cond = (c * jax.nn.sigmoid(c)).astype(BF16)
    o_ref[0] = _dot(cond, w_ref[0].astype(BF16)) + b_ref[0]


def _ada(c, ada_w, ada_b):
    n_layers, d, n = ada_w.shape
    b = c.shape[0]
    rows = 8
    tn = 1024
    cpad = jnp.pad(c, ((0, rows - b), (0, 0)))
    out = pl.pallas_call(
        _ada_kernel,
        grid=(n_layers, n // tn),
        in_specs=[
            pl.BlockSpec((rows, d), lambda l, j: (0, 0)),
            pl.BlockSpec((1, d, tn), lambda l, j: (l, 0, j)),
            pl.BlockSpec((1, 1, tn), lambda l, j: (l, 0, j)),
        ],
        out_specs=pl.BlockSpec((1, rows, tn), lambda l, j: (l, 0, j)),
        out_shape=jax.ShapeDtypeStruct((n_layers, rows, n), F32),
        compiler_params=_cparams("parallel", "parallel"),
        name="ada_mod",
    )(cpad, ada_w, ada_b.reshape(n_layers, 1, n))
    return out[:, :b].reshape(n_layers, b, 6, d)


def _modulated_norm(x, g, shift, scale):
    ms = jnp.mean(x * x, axis=-1, keepdims=True)
    return x * lax.rsqrt(ms + EPS) * g * (1.0 + scale) + shift


def _in_kernel(x_ref, mod_ref, g_ref, w_ref, sb_ref, o_ref, h_ref, *, n_sb):
    j = pl.program_id(1)

    @pl.when(j == 0)
    def _():
        m = mod_ref[0]
        h_ref[...] = _modulated_norm(x_ref[...], g_ref[...], m[0:1], m[1:2]).astype(BF16)

    @pl.when(j < n_sb)
    def _():
        sb_ref[...] = _dot(h_ref[...], w_ref[...]).astype(BF16)

    @pl.when(j >= n_sb)
    def _():
        o_ref[...] = _dot(h_ref[...], w_ref[...])


def _in_proj(x2, mod, g, w, layer, seq):
    m_rows, d = x2.shape
    n = w.shape[2]
    tm, tn = 1024, 768
    per_seq = seq // tm
    n_sb = U_SB_COLS // tn
    return pl.pallas_call(
        functools.partial(_in_kernel, n_sb=n_sb),
        grid=(m_rows // tm, n // tn),
        in_specs=[
            pl.BlockSpec((tm, d), lambda i, j: (i, 0)),
            pl.BlockSpec((1, 6, d), lambda i, j: (i // per_seq, 0, 0)),
            pl.BlockSpec((1, d), lambda i, j: (0, 0)),
            pl.BlockSpec((None, d, tn), lambda i, j: (layer, 0, j)),
        ],
        out_specs=[
            pl.BlockSpec((tm, tn), lambda i, j: (i, jnp.minimum(j, n_sb - 1))),
            pl.BlockSpec((tm, tn), lambda i, j: (i, jnp.maximum(j - n_sb, 0))),
        ],
        out_shape=[jax.ShapeDtypeStruct((m_rows, U_SB_COLS), BF16),
                   jax.ShapeDtypeStruct((m_rows, n - U_SB_COLS), F32)],
        scratch_shapes=[pltpu.VMEM((tm, d), BF16)],
        compiler_params=_cparams("parallel", "arbitrary"),
        name="in_proj",
    )(x2, mod, g, w)


def _diag_masks():
    r = lax.broadcasted_iota(jnp.int32, (TQ, TK), 0)
    c = lax.broadcasted_iota(jnp.int32, (TQ, TK), 1)
    return c <= r, c < r


def _sb_kernel(q_ref, k_ref, v_ref, o_ref, z_ref, a_ref, *, nq):
    i = pl.program_id(2)
    lane = _lane_iota()
    heads = range(HEADS_PER_STEP)
    pair = lambda hd: slice((hd // 2) * LANES, (hd // 2 + 1) * LANES)
    qs = []
    for pr in range(HEADS_PER_STEP // 2):
        q2 = q_ref[0, :, pr * LANES:(pr + 1) * LANES] * (HEAD_DIM ** -0.5)
        qs.append(jnp.where(lane < HEAD_DIM, q2, jnp.zeros_like(q2)))
        qs.append(jnp.where(lane >= HEAD_DIM, q2, jnp.zeros_like(q2)))
    tri = (lax.broadcasted_iota(jnp.int32, (TK, TK), 0)
           > lax.broadcasted_iota(jnp.int32, (TK, TK), 1)).astype(BF16)
    tri2 = jnp.concatenate([tri, tri], axis=0)
    _, strict = _diag_masks()

    for c in range(nq):
        @pl.when(i == c)
        def _():
            kv_len = (c + 1) * TK
            for hd in heads:
                z_ref[hd, :, :kv_len] = _dot_nt(qs[hd], k_ref[0, :kv_len, pair(hd)])
            carry = [jnp.zeros((TQ, 1), F32) for _ in heads]
            for j in reversed(range(c + 1)):
                sl = slice(j * TK, (j + 1) * TK)
                for hd in heads:
                    z = z_ref[hd, :, sl]
                    sp = jnp.log(1.0 + jnp.exp(-jnp.abs(z)))
                    log_beta = jnp.minimum(z, 0.0) - sp
                    neg_1m = jnp.maximum(z, 0.0) + sp
                    if j == c:
                        neg_1m = jnp.where(strict, neg_1m, 0.0)
                    hi = neg_1m.astype(BF16)
                    pieces = jnp.concatenate([hi, (neg_1m - hi.astype(F32)).astype(BF16)], axis=1)
                    later = _dot(pieces, tri2)
                    a = jnp.exp(log_beta - later - carry[hd])
                    if j == c:
                        a = jnp.where(strict, a, 0.0)
                    a_ref[hd, :, sl] = a.astype(BF16)
                    carry[hd] = carry[hd] + jnp.sum(neg_1m, axis=-1, keepdims=True)
            outs = [_dot(a_ref[hd, :, :kv_len], v_ref[0, :kv_len, pair(hd)]) for hd in heads]
            for pr in range(HEADS_PER_STEP // 2):
                o_ref[0, :, pr * LANES:(pr + 1) * LANES] = jnp.where(lane < HEAD_DIM, outs[2 * pr], outs[2 * pr + 1])


def _sb_attention(usb3):
    b, t, _ = usb3.shape
    nq = t // TQ
    wide = HEADS_PER_STEP * HEAD_DIM
    return pl.pallas_call(
        functools.partial(_sb_kernel, nq=nq),
        grid=(b, GROUP_WIDTH // wide, nq),
        in_specs=[
            pl.BlockSpec((1, TQ, wide), lambda bb, p, i: (bb, i, U_SBQ // wide + p)),
            pl.BlockSpec((1, t, wide), lambda bb, p, i: (bb, 0, U_SBK // wide + p)),
            pl.BlockSpec((1, t, wide), lambda bb, p, i: (bb, 0, U_SBV // wide + p)),
        ],
        out_specs=pl.BlockSpec((1, TQ, wide), lambda bb, p, i: (bb, i, p)),
        out_shape=jax.ShapeDtypeStruct((b, t, GROUP_WIDTH), F32),
        scratch_shapes=[pltpu.VMEM((HEADS_PER_STEP, TQ, t), F32), pltpu.VMEM((HEADS_PER_STEP, TQ, t), BF16)],
        compiler_params=_cparams("parallel", "parallel", "arbitrary"),
        name="sb_attention",
    )(usb3, usb3, usb3)


CONV_HALO = 32


def _conv_kernel(a_ref, g_ref, ah_ref, gh_ref, dww_ref, dwb_ref, lng_ref, lnb_ref, pw_ref, pwb_ref,
                 o_ref, h_ref, s_ref, *, tm):
    i = pl.program_id(1)
    h_ref[CONV_HALO:, :] = a_ref[0] * jax.nn.sigmoid(g_ref[0])
    halo = ah_ref[0] * jax.nn.sigmoid(gh_ref[0])
    h_ref[0:CONV_HALO, :] = jnp.where(i == 0, 0.0, halo)
    acc = jnp.zeros((tm, GROUP_WIDTH), F32) + dwb_ref[...]
    base = CONV_HALO - (CONV_WIDTH - 1)
    sub = 8
    for r in range(sub):
        taps = [k for k in range(CONV_WIDTH) if (base + k) % sub == r]
        if not taps:
            continue
        lo = base + taps[0]
        span = (taps[-1] - taps[0]) + tm
        s_ref[r, 0:span, :] = h_ref[lo:lo + span, :]
        for k in taps:
            acc = acc + dww_ref[k:k + 1, :] * s_ref[r, k - taps[0]:k - taps[0] + tm, :]
    mu = jnp.mean(acc, axis=-1, keepdims=True)
    cen = acc - mu
    var = jnp.mean(cen * cen, axis=-1, keepdims=True)
    hn = cen * lax.rsqrt(var + CONV_LN_EPS) * lng_ref[...] + lnb_ref[...]
    act = hn * jax.nn.sigmoid(hn)
    o_ref[0] = _dot(act.astype(BF16), pw_ref[...]) + pwb_ref[...]


def _conformer(u3, dw_w, dw_b, ln_g, ln_b, pw_w, pw_b):
    b, t, _ = u3.shape
    tm = 512
    cw = GROUP_WIDTH
    hb = tm // CONV_HALO

    def halo_map(col):
        return lambda bb, i: (bb, jnp.maximum(i * hb - 1, 0), col)

    row = lambda bb, i: (0, 0)
    return pl.pallas_call(
        functools.partial(_conv_kernel, tm=tm),
        grid=(b, t // tm),
        in_specs=[
            pl.BlockSpec((1, tm, cw), lambda bb, i: (bb, i, U_CA // cw)),
            pl.BlockSpec((1, tm, cw), lambda bb, i: (bb, i, U_CG // cw)),
            pl.BlockSpec((1, CONV_HALO, cw), halo_map(U_CA // cw)),
            pl.BlockSpec((1, CONV_HALO, cw), halo_map(U_CG // cw)),
            pl.BlockSpec((CONV_WIDTH, cw), row),
            pl.BlockSpec((1, cw), row),
            pl.BlockSpec((1, cw), row),
            pl.BlockSpec((1, cw), row),
            pl.BlockSpec((cw, cw), row),
            pl.BlockSpec((1, cw), row),
        ],
        out_specs=pl.BlockSpec((1, tm, cw), lambda bb, i: (bb, i, 0)),
        out_shape=jax.ShapeDtypeStruct((b, t, cw), F32),
        scratch_shapes=[pltpu.VMEM((tm + CONV_HALO, cw), F32), pltpu.VMEM((8, tm + CONV_HALO, cw), F32)],
        compiler_params=_cparams("parallel", "parallel"),
        name="conformer_conv",
    )(u3, u3, u3, u3, dw_w, dw_b.reshape(1, cw), ln_g.reshape(1, cw), ln_b.reshape(1, cw),
      pw_w.astype(BF16), pw_b.reshape(1, cw))


def _head_norm_rope(x, gain, cos, sin_signed, seg_mean):
    ms = _split_dot(x * x, seg_mean)
    y = x * lax.rsqrt(ms + EPS) * gain
    lane = _lane_iota()
    half = HEAD_DIM // 2
    first = (lane & (HEAD_DIM - 1)) < half
    partner = jnp.where(first, pltpu.roll(y, LANES - half, 1), pltpu.roll(y, half, 1))
    return y * cos + partner * sin_signed


def _dup(x, g):
    lane = _lane_iota()
    sw = pltpu.roll(x, HEAD_DIM, 1)
    if g == 0:
        return jnp.where(lane < HEAD_DIM, x, sw)
    return jnp.where(lane < HEAD_DIM, sw, x)


def _nsa_prep_kernel(q_ref, kv_ref, gk_ref, cos_ref, sin_ref, qg_ref, kg_ref,
                     qn_ref, kc_ref, kse_ref, kso_ref, vs_ref, kw_ref, vw_ref, gt_ref, *, per_seq):
    cos = cos_ref[...]
    sin = sin_ref[...]
    seg = lax.broadcasted_iota(jnp.int32, (LANES, LANES), 0) >> HEAD_SHIFT
    seg_mean = jnp.where(seg == lax.broadcasted_iota(jnp.int32, (LANES, LANES), 1) >> HEAD_SHIFT,
                         1.0 / HEAD_DIM, 0.0).astype(BF16)
    scale = HEAD_DIM ** -0.5
    for p in range(GROUP_WIDTH // LANES):
        x = q_ref[:, p * LANES:(p + 1) * LANES]
        qn_ref[:, p * LANES:(p + 1) * LANES] = (
            _head_norm_rope(x, qg_ref[...], cos, sin, seg_mean) * scale).astype(BF16)

    def blk(n):
        return kv_ref[:, n * LANES:(n + 1) * LANES]

    kc_ref[...] = _head_norm_rope(blk(0), kg_ref[0:1, :], cos, sin, seg_mean)
    ks = _head_norm_rope(blk(2), kg_ref[1:2, :], cos, sin, seg_mean)
    kw = _head_norm_rope(blk(4), kg_ref[2:3, :], cos, sin, seg_mean)
    vs = blk(3)
    vw = blk(5)
    tm = ks.shape[0]
    lane = _lane_iota()
    t = (pl.program_id(0) % per_seq) * tm + lax.broadcasted_iota(jnp.int32, (tm, 1), 0)
    sblk = t >> SEL_SHIFT
    hot_lo = jnp.where(lane == sblk, 1.0, 0.0)
    hot_hi = jnp.where(lane - HEAD_DIM == sblk, 1.0, 0.0)
    ks_sw = pltpu.roll(ks, HEAD_DIM, 1)
    for g in range(2):
        sl = slice(g * LANES, (g + 1) * LANES)
        kse_ref[:, sl] = jnp.where(lane < HEAD_DIM, ks if g == 0 else ks_sw, hot_hi).astype(BF16)
        kso_ref[:, sl] = jnp.where(lane >= HEAD_DIM, ks_sw if g == 0 else ks, hot_lo).astype(BF16)
        vs_ref[:, sl] = _dup(vs, g).astype(BF16)
        kw_ref[:, sl] = _dup(kw, g).astype(BF16)
        vw_ref[:, sl] = _dup(vw, g).astype(BF16)
    gates = jax.nn.sigmoid(gk_ref[...])
    gt_ref[:, 0:LANES] = gates
    gt_ref[:, LANES:2 * LANES] = pltpu.roll(gates, LANES - 12, 1)


def _nsa_prep(u2, cos, sin, q_gain, k_gain, seq):
    m_rows = u2.shape[0]
    tm = 512
    per_seq = seq // tm
    rowmap = lambda i: (i, 0)
    tab = lambda i: (i % per_seq, 0)
    const = lambda i: (0, 0)
    outs = [
        jax.ShapeDtypeStruct((m_rows, GROUP_WIDTH), BF16),
        jax.ShapeDtypeStruct((m_rows, LANES), F32),
        jax.ShapeDtypeStruct((m_rows, 2 * LANES), BF16),
        jax.ShapeDtypeStruct((m_rows, 2 * LANES), BF16),
        jax.ShapeDtypeStruct((m_rows, 2 * LANES), BF16),
        jax.ShapeDtypeStruct((m_rows, 2 * LANES), BF16),
        jax.ShapeDtypeStruct((m_rows, 2 * LANES), BF16),
        jax.ShapeDtypeStruct((m_rows, 2 * LANES), F32),
    ]
    return pl.pallas_call(
        functools.partial(_nsa_prep_kernel, per_seq=per_seq),
        grid=(m_rows // tm,),
        in_specs=[
            pl.BlockSpec((tm, GROUP_WIDTH), lambda i: (i, U_NQ // GROUP_WIDTH)),
            pl.BlockSpec((tm, 6 * LANES), lambda i: (i, U_NKV // (6 * LANES))),
            pl.BlockSpec((tm, LANES), lambda i: (i, (U_MLA + MLA_Q_RANK + MLA_KV_RANK) // LANES)),
            pl.BlockSpec((tm, LANES), tab),
            pl.BlockSpec((tm, LANES), tab),
            pl.BlockSpec((1, LANES), const),
            pl.BlockSpec((3, LANES), const),
        ],
        out_specs=[
            pl.BlockSpec((tm, GROUP_WIDTH), rowmap),
            pl.BlockSpec((tm, LANES), rowmap),
            pl.BlockSpec((tm, 2 * LANES), rowmap),
            pl.BlockSpec((tm, 2 * LANES), rowmap),
            pl.BlockSpec((tm, 2 * LANES), rowmap),
            pl.BlockSpec((tm, 2 * LANES), rowmap),
            pl.BlockSpec((tm, 2 * LANES), rowmap),
            pl.BlockSpec((tm, 2 * LANES), rowmap),
        ],
        out_shape=outs,
        compiler_params=_cparams("parallel"),
        name="nsa_prep",
    )(u2, u2, u2, cos, sin, q_gain, k_gain)


def _compress_kernel(xk_ref, xv_ref, pek_ref, pev_ref, wk_ref, wv_ref, kc_ref, vc_ref, *, nrow):
    stride = NSA_CMP_STRIDE

    def comp(x_ref, pe_ref, w_ref):
        first = jnp.zeros((nrow, LANES), F32)
        second = jnp.zeros((nrow, LANES), F32)
        for r in range(stride):
            x = x_ref[0, pl.ds(r, nrow, stride=stride), :]
            first = first + _dot((x + pe_ref[r:r + 1, :]).astype(BF16), w_ref[r])
            second = second + _dot((x + pe_ref[stride + r:stride + r + 1, :]).astype(BF16), w_ref[stride + r])
        return first + pltpu.roll(second, nrow - 1, 0)

    ck = comp(xk_ref, pek_ref, wk_ref)
    cv = comp(xv_ref, pev_ref, wv_ref)
    for g in range(2):
        kc_ref[0, g] = _dup(ck, g).astype(BF16)
        vc_ref[0, g] = _dup(cv, g).astype(BF16)


def _compress(kc3, u3, pe_k, pe_v, wk, wv):
    b, t, _ = kc3.shape
    nrow = t // NSA_CMP_STRIDE
    const2 = lambda bb: (0, 0)
    const3 = lambda bb: (0, 0, 0)
    out = jax.ShapeDtypeStruct((b, 2, nrow, LANES), BF16)
    return pl.pallas_call(
        functools.partial(_compress_kernel, nrow=nrow),
        grid=(b,),
        in_specs=[
            pl.BlockSpec((1, t, LANES), lambda bb: (bb, 0, 0)),
            pl.BlockSpec((1, t, LANES), lambda bb: (bb, 0, U_NKV // LANES + 1)),
            pl.BlockSpec(pe_k.shape, const2),
            pl.BlockSpec(pe_v.shape, const2),
            pl.BlockSpec(wk.shape, const3),
            pl.BlockSpec(wv.shape, const3),
        ],
        out_specs=[pl.BlockSpec((1, 2, nrow, LANES), lambda bb: (bb, 0, 0, 0))] * 2,
        out_shape=[out, out],
        compiler_params=_cparams("parallel"),
        name="nsa_compress",
    )(kc3, u3, pe_k, pe_v, wk, wv)


def _attend_many(jobs, scale=None):
    tiles = [k.shape[0] // TK for (_, k, _, _, _, _) in jobs]
    for (q, k, _, s_ref, _, _), n in zip(jobs, tiles):
        s_ref[:, :n * TK] = _dot_nt(q, k)
    maxima = []
    for (_, _, _, s_ref, _, masks), n in zip(jobs, tiles):
        fold = None
        for j in range(n):
            sl = slice(j * TK, (j + 1) * TK)
            s = s_ref[:, sl]
            if j in masks:
                s = jnp.where(masks[j], s, NEG)
                s_ref[:, sl] = s
            f = jnp.maximum(s[:, :LANES], s[:, LANES:])
            fold = f if fold is None else jnp.maximum(fold, f)
        maxima.append(jnp.max(fold, axis=-1, keepdims=True))
    sums = []
    for (_, _, _, s_ref, p_ref, _), n, m in zip(jobs, tiles, maxima):
        fold = None
        for j in range(n):
            sl = slice(j * TK, (j + 1) * TK)
            p = jnp.exp2((s_ref[:, sl] - m) * (LOG2E if scale is None else scale * LOG2E))
            f = p[:, :LANES] + p[:, LANES:]
            fold = f if fold is None else fold + f
            p_ref[:, sl] = p.astype(BF16)
        sums.append(jnp.sum(fold, axis=-1, keepdims=True))
    return [_dot(p_ref[:, :n * TK], v) / l for (_, _, v, _, p_ref, _), n, l in zip(jobs, tiles, sums)]


def _nsa_kernel(q_ref, kc_ref, vc_ref, kse_ref, kso_ref, vs_ref, kw_ref, vw_ref, gt_ref, ov_ref,
                o_ref, s_ref, p_ref, sw_ref, pw_ref, *, nq, n_cmp, n_sblk, n_sel):
    i = pl.program_id(2)
    lane = _lane_iota()
    row = i * TQ + lax.broadcasted_iota(jnp.int32, (TQ, 1), 0)
    rep = 4
    q2s = [q_ref[0, :, p * LANES:(p + 1) * LANES] for p in range(2)]
    qs = []
    for q2 in q2s:
        qs.append(jnp.where(lane < HEAD_DIM, q2, jnp.zeros_like(q2)))
        qs.append(jnp.where(lane >= HEAD_DIM, q2, jnp.zeros_like(q2)))

    kc = kc_ref[0, 0]
    vc = vc_ref[0, 0]
    cmask = ((lane * NSA_CMP_STRIDE + (NSA_CMP_LEN - 1)) <= row) & (lane < n_cmp)
    psum = jnp.zeros((TQ, LANES), F32)
    s_cmp = [jnp.where(cmask, _dot_nt(qs[r], kc), NEG) for r in range(rep)]
    p_cmp = []
    for r in range(rep):
        mx = jnp.max(s_cmp[r], axis=-1, keepdims=True)
        p = jnp.where(cmask, jnp.exp(s_cmp[r] - mx), 0.0)
        p = p / jnp.maximum(jnp.sum(p, axis=-1, keepdims=True), 1e-30)
        p_cmp.append(p.astype(BF16))
        psum = psum + p
    o_cmp = [_dot(p_cmp[r], vc) for r in range(rep)]
    ov_t = ov_ref[0:n_sblk, :]
    p_hi = psum.astype(BF16)
    p_lo = (psum - p_hi.astype(F32)).astype(BF16)
    imp = _dot_nt(ov_t, p_hi) + _dot_nt(ov_t, p_lo)
    cur = (i * TQ + _lane_iota(TQ)) >> SEL_SHIFT
    blk = lax.broadcasted_iota(jnp.int32, (n_sblk, 1), 0)
    eligible = blk <= cur
    forced = (blk == 0) | (blk == cur) | (blk == cur - 1)
    score = jnp.where(eligible, imp + jnp.where(forced, NSA_FORCE_BONUS, 0.0), NEG)
    rank = jnp.zeros((n_sblk, TQ), F32)
    for sp in range(n_sblk):
        other = score[sp:sp + 1, :]
        beats = (other > score) | ((other == score) & (blk > sp))
        rank = rank + jnp.where(beats, 1.0, 0.0)
    bias_t = jnp.where(eligible & (rank < n_sel), 0.0, NEG)

    bias_lo = jnp.concatenate([bias_t, jnp.zeros((LANES - n_sblk, TQ), F32)], axis=0).T
    bias_hi = pltpu.roll(bias_lo, HEAD_DIM, 1).astype(BF16)
    bias_lo = bias_lo.astype(BF16)
    q_aug = []
    for q2 in q2s:
        q_aug.append(jnp.where(lane < HEAD_DIM, q2, bias_hi))
        q_aug.append(jnp.where(lane >= HEAD_DIM, q2, bias_lo))
    causal, strict = _diag_masks()
    wtiles = NSA_WINDOW // TK
    gt = gt_ref[0]

    for c in range(nq):
        @pl.when(i == c)
        def _():
            kv_len = (c + 1) * TK
            lo = max(c - wtiles, 0) * TK
            wmasks = {c - max(c - wtiles, 0): causal}
            if c >= wtiles:
                wmasks[0] = jnp.logical_not(causal)
            jobs = []
            for r in range(rep):
                ks_ref = kse_ref if r % 2 == 0 else kso_ref
                jobs.append((q_aug[r], ks_ref[0, :kv_len, :], vs_ref[0, :kv_len, :],
                             s_ref.at[r], p_ref.at[r], {c: causal}))
            for r in range(rep):
                jobs.append((qs[r], kw_ref[0, lo:kv_len, :], vw_ref[0, lo:kv_len, :],
                             sw_ref.at[r], pw_ref.at[r], wmasks))
            res = _attend_many(jobs)
            outs = []
            for r in range(rep):
                outs.append(gt[:, 3 * r:3 * r + 1] * o_cmp[r] + gt[:, 3 * r + 1:3 * r + 2] * res[r]
                            + gt[:, 3 * r + 2:3 * r + 3] * res[rep + r])
            o_ref[0, :, 0:LANES] = jnp.where(lane < HEAD_DIM, outs[0], outs[1])
            o_ref[0, :, LANES:2 * LANES] = jnp.where(lane < HEAD_DIM, outs[2], outs[3])


def _nsa_attention(qn, kcmp, vcmp, kse, kso, vsd, kwd, vwd, gates, overlap):
    b, t, _ = qn.shape
    nq = t // TQ
    n_cmp = (t - NSA_CMP_LEN) // NSA_CMP_STRIDE + 1
    n_sblk = t // NSA_SEL_LEN
    nrow = kcmp.shape[2]
    wlen = NSA_WINDOW + TK
    kv = lambda bb, g, i: (bb, 0, g)
    cm = lambda bb, g, i: (bb, g, 0, 0)
    return pl.pallas_call(
        functools.partial(_nsa_kernel, nq=nq, n_cmp=n_cmp, n_sblk=n_sblk, n_sel=min(NSA_N_SEL, n_sblk)),
        grid=(b, 2, nq),
        in_specs=[
            pl.BlockSpec((1, TQ, 2 * LANES), lambda bb, g, i: (bb, i, g)),
            pl.BlockSpec((1, 1, nrow, LANES), cm),
            pl.BlockSpec((1, 1, nrow, LANES), cm),
            pl.BlockSpec((1, t, LANES), kv),
            pl.BlockSpec((1, t, LANES), kv),
            pl.BlockSpec((1, t, LANES), kv),
            pl.BlockSpec((1, t, LANES), kv),
            pl.BlockSpec((1, t, LANES), kv),
            pl.BlockSpec((1, TQ, LANES), lambda bb, g, i: (bb, i, g)),
            pl.BlockSpec((LANES, LANES), lambda bb, g, i: (0, 0)),
        ],
        out_specs=pl.BlockSpec((1, TQ, 2 * LANES), lambda bb, g, i: (bb, i, g)),
        out_shape=jax.ShapeDtypeStruct((b, t, GROUP_WIDTH), F32),
        scratch_shapes=[pltpu.VMEM((4, TQ, t), F32), pltpu.VMEM((4, TQ, t), BF16),
                        pltpu.VMEM((4, TQ, wlen), F32), pltpu.VMEM((4, TQ, wlen), BF16)],
        compiler_params=_cparams("parallel", "parallel", "arbitrary"),
        name="nsa_attention",
    )(qn, kcmp, vcmp, kse, kso, vsd, kwd, vwd, gates, overlap)


def _mla_prep_kernel(u_ref, cos_ref, sin_ref, qlg_ref, kvlg_ref, wq_ref, wk_ref, wv_ref, qg_ref, kg_ref,
                     q_ref, k_ref, v_ref):
    lane = _lane_iota()
    cos = cos_ref[...]
    sin = sin_ref[...]
    half = MLA_ROPE // 2

    def lat_norm(x, g):
        ms = jnp.mean(x * x, axis=-1, keepdims=True)
        return (x * lax.rsqrt(ms + EPS) * g).astype(BF16)

    live = lane < MLA_QK

    def head_norm_rope(x, g):
        ms = jnp.sum(jnp.where(live, x * x, 0.0), axis=-1, keepdims=True) * (1.0 / MLA_QK)
        y = x * lax.rsqrt(ms + EPS) * g
        return y * cos + pltpu.roll(y, LANES - half, 1) * sin

    q = _dot(lat_norm(u_ref[:, 0:MLA_Q_RANK], qlg_ref[...]), wq_ref[...])
    kvn = lat_norm(u_ref[:, MLA_Q_RANK:MLA_Q_RANK + MLA_KV_RANK], kvlg_ref[...])
    kn = _dot(kvn, wk_ref[...])
    v_ref[...] = _dot(kvn, wv_ref[...]).astype(BF16)
    gk = u_ref[:, MLA_Q_RANK + MLA_KV_RANK:MLA_Q_RANK + MLA_KV_RANK + LANES]
    k_rope = jnp.where(lane >= MLA_NOPE, gk, 0.0)
    for h in range(MLA_HEADS):
        sl = slice(h * LANES, (h + 1) * LANES)
        q_ref[:, sl] = head_norm_rope(q[:, sl], qg_ref[...]).astype(BF16)
        k_ref[:, sl] = head_norm_rope(kn[:, sl] + k_rope, kg_ref[...]).astype(BF16)


def _mla_prep(u2, cos, sin, q_lat_gain, kv_lat_gain, wq, wk, wv, q_gain, k_gain, seq):
    m_rows = u2.shape[0]
    tm = 512
    per_seq = seq // tm
    width = MLA_Q_RANK + MLA_KV_RANK + LANES
    rowmap = lambda i: (i, 0)
    tab = lambda i: (i % per_seq, 0)
    const = lambda i: (0, 0)
    return pl.pallas_call(
        _mla_prep_kernel,
        grid=(m_rows // tm,),
        in_specs=[
            pl.BlockSpec((tm, width), lambda i: (i, U_MLA // width)),
            pl.BlockSpec((tm, LANES), tab),
            pl.BlockSpec((tm, LANES), tab),
            pl.BlockSpec((1, MLA_Q_RANK), const),
            pl.BlockSpec((1, MLA_KV_RANK), const),
            pl.BlockSpec(wq.shape, const),
            pl.BlockSpec(wk.shape, const),
            pl.BlockSpec(wv.shape, const),
            pl.BlockSpec((1, LANES), const),
            pl.BlockSpec((1, LANES), const),
        ],
        out_specs=[
            pl.BlockSpec((tm, MLA_HEADS * LANES), rowmap),
            pl.BlockSpec((tm, MLA_HEADS * LANES), rowmap),
            pl.BlockSpec((tm, GROUP_WIDTH), rowmap),
        ],
        out_shape=[
            jax.ShapeDtypeStruct((m_rows, MLA_HEADS * LANES), BF16),
            jax.ShapeDtypeStruct((m_rows, MLA_HEADS * LANES), BF16),
            jax.ShapeDtypeStruct((m_rows, GROUP_WIDTH), BF16),
        ],
        compiler_params=_cparams("parallel"),
        name="mla_prep",
    )(u2, cos, sin, q_lat_gain, kv_lat_gain, wq, wk, wv, q_gain, k_gain)


def _mla_kernel(q_ref, k_ref, v_ref, o_ref, s_ref, p_ref, *, nq):
    i = pl.program_id(2)
    lane = _lane_iota()
    causal, _ = _diag_masks()
    scale = MLA_QK ** -0.5
    for c in range(nq):
        @pl.when(i == c)
        def _():
            kv_len = (c + 1) * TK
            jobs = []
            for hd in range(HEADS_PER_STEP):
                sl = slice(hd * LANES, (hd + 1) * LANES)
                vs = slice((hd // 2) * LANES, (hd // 2 + 1) * LANES)
                jobs.append((q_ref[0, :, sl], k_ref[0, :kv_len, sl], v_ref[0, :kv_len, vs],
                             s_ref.at[hd], p_ref.at[hd], {c: causal}))
            outs = _attend_many(jobs, scale=scale)
            for pr in range(HEADS_PER_STEP // 2):
                o_ref[0, :, pr * LANES:(pr + 1) * LANES] = jnp.where(lane < HEAD_DIM, outs[2 * pr], outs[2 * pr + 1])


def _mla_attention(qm, km, vm):
    b, t, _ = qm.shape
    nq = t // TQ
    return pl.pallas_call(
        functools.partial(_mla_kernel, nq=nq),
        grid=(b, MLA_HEADS // HEADS_PER_STEP, nq),
        in_specs=[
            pl.BlockSpec((1, TQ, HEADS_PER_STEP * LANES), lambda bb, p, i: (bb, i, p)),
            pl.BlockSpec((1, t, HEADS_PER_STEP * LANES), lambda bb, p, i: (bb, 0, p)),
            pl.BlockSpec((1, t, HEADS_PER_STEP * HEAD_DIM), lambda bb, p, i: (bb, 0, p)),
        ],
        out_specs=pl.BlockSpec((1, TQ, HEADS_PER_STEP * HEAD_DIM), lambda bb, p, i: (bb, i, p)),
        out_shape=jax.ShapeDtypeStruct((b, t, GROUP_WIDTH), F32),
        scratch_shapes=[pltpu.VMEM((HEADS_PER_STEP, TQ, t), F32), pltpu.VMEM((HEADS_PER_STEP, TQ, t), BF16)],
        compiler_params=_cparams("parallel", "parallel", "arbitrary"),
        name="mla_attention",
    )(qm, km, vm)


def _out_kernel(a_ref, b_ref, c_ref, d_ref, gn_ref, w_ref, x_ref, mod_ref, o_ref):
    acc = None
    for gi, r in enumerate((a_ref, b_ref, c_ref, d_ref)):
        y = r[...]
        ms = jnp.mean(y * y, axis=-1, keepdims=True)
        sl = slice(gi * GROUP_WIDTH, (gi + 1) * GROUP_WIDTH)
        part = _dot((y * lax.rsqrt(ms + EPS) * gn_ref[:, sl]).astype(BF16), w_ref[sl, :])
        acc = part if acc is None else acc + part
    o_ref[...] = x_ref[...] + mod_ref[0, 2:3, :] * acc


def _out_proj(parts, gn, w, layer, x2, mod, seq):
    m_rows, d = x2.shape
    tm = 512
    per_seq = seq // tm
    part = pl.BlockSpec((tm, GROUP_WIDTH), lambda i: (i, 0))
    return pl.pallas_call(
        _out_kernel,
        grid=(m_rows // tm,),
        in_specs=[
            part, part, part, part,
            pl.BlockSpec((1, 4 * GROUP_WIDTH), lambda i: (0, 0)),
            pl.BlockSpec((None, 4 * GROUP_WIDTH, d), lambda i: (layer, 0, 0)),
            pl.BlockSpec((tm, d), lambda i: (i, 0)),
            pl.BlockSpec((1, 6, d), lambda i: (i // per_seq, 0, 0)),
        ],
        out_specs=pl.BlockSpec((tm, d), lambda i: (i, 0)),
        out_shape=jax.ShapeDtypeStruct((m_rows, d), F32),
        compiler_params=_cparams("parallel"),
        name="out_proj",
    )(*parts, gn, w, x2, mod)


FFN_HALO = 16


def _ffn_kernel(x_ref, xh_ref, mod_ref, g_ref, wa_ref, wg_ref, cwa_ref, cwg_ref, cba_ref, cbg_ref, wd_ref,
                o_ref, h_ref, act_ref, acc_ref, *, per_seq, nj):
    i = pl.program_id(0)
    j = pl.program_id(1)
    m = mod_ref[0]

    def conv(u, cw_ref, cb_ref):
        y = cw_ref[2:3, :] * u + cw_ref[1:2, :] * pltpu.roll(u, 1, 0) + cw_ref[0:1, :] * pltpu.roll(u, 2, 0)
        return y[FFN_HALO:, :] + cb_ref[...]

    def activation():
        h = h_ref[...]
        ya = conv(_dot(h, wa_ref[...]), cwa_ref, cba_ref)
        yg = conv(_dot(h, wg_ref[...]), cwg_ref, cbg_ref)
        return ((yg * jax.nn.sigmoid(yg)) * ya).astype(BF16)

    @pl.when(j == 0)
    def _():
        h_ref[FFN_HALO:, :] = _modulated_norm(x_ref[...], g_ref[...], m[3:4], m[4:5]).astype(BF16)
        halo = _modulated_norm(xh_ref[...], g_ref[...], m[3:4], m[4:5])
        h_ref[0:FFN_HALO, :] = jnp.where(i % per_seq == 0, 0.0, halo).astype(BF16)
        act_ref[...] = activation()

    @pl.when(j == 1)
    def _():
        new = activation()
        acc_ref[...] = _dot(act_ref[...], wd_ref[...])
        act_ref[...] = new

    @pl.when((j > 1) & (j < nj))
    def _():
        new = activation()
        acc_ref[...] += _dot(act_ref[...], wd_ref[...])
        act_ref[...] = new

    @pl.when(j == nj)
    def _():
        o_ref[...] = x_ref[...] + m[5:6] * (acc_ref[...] + _dot(act_ref[...], wd_ref[...]))


def _ffn(x2, mod, g, w_up, conv_w, conv_b, w_down, layer, seq):
    m_rows, d = x2.shape
    d_ff = w_down.shape[1]
    tm, tn = 512, 512
    per_seq = seq // tm
    nj = d_ff // tn
    hb = tm // FFN_HALO
    cb = conv_b.reshape(1, 2 * d_ff)
    assert nj >= 2
    up = lambda j: jnp.minimum(j, nj - 1)
    return pl.pallas_call(
        functools.partial(_ffn_kernel, per_seq=per_seq, nj=nj),
        grid=(m_rows // tm, nj + 1),
        in_specs=[
            pl.BlockSpec((tm, d), lambda i, j: (i, 0)),
            pl.BlockSpec((FFN_HALO, d), lambda i, j: (jnp.maximum(i * hb - 1, 0), 0)),
            pl.BlockSpec((1, 6, d), lambda i, j: (i // per_seq, 0, 0)),
            pl.BlockSpec((1, d), lambda i, j: (0, 0)),
            pl.BlockSpec((None, d, tn), lambda i, j: (layer, 0, up(j))),
            pl.BlockSpec((None, d, tn), lambda i, j: (layer, 0, nj + up(j))),
            pl.BlockSpec((3, tn), lambda i, j: (0, up(j))),
            pl.BlockSpec((3, tn), lambda i, j: (0, nj + up(j))),
            pl.BlockSpec((1, tn), lambda i, j: (0, up(j))),
            pl.BlockSpec((1, tn), lambda i, j: (0, nj + up(j))),
            pl.BlockSpec((None, tn, d), lambda i, j: (layer, jnp.maximum(j - 1, 0), 0)),
        ],
        out_specs=pl.BlockSpec((tm, d), lambda i, j: (i, 0)),
        out_shape=jax.ShapeDtypeStruct((m_rows, d), F32),
        scratch_shapes=[pltpu.VMEM((tm + FFN_HALO, d), BF16), pltpu.VMEM((tm, tn), BF16),
                        pltpu.VMEM((tm, d), F32)],
        compiler_params=_cparams("parallel", "arbitrary"),
        name="conv_glu_ffn",
    )(x2, x2, mod, g, w_up, w_up, conv_w, conv_w, cb, cb, w_down)


def _reorder_w_in(w):
    n_gate = 24
    off_mla = REF_ALIGNED_COLS + n_gate
    lat = MLA_Q_RANK + MLA_KV_RANK
    z = lambda n: jnp.zeros(w.shape[:-1] + (n,), w.dtype)
    half = MLA_ROPE // 2
    out = jnp.concatenate([
        w[..., :REF_ALIGNED_COLS], w[..., off_mla:off_mla + lat],
        w[..., REF_ALIGNED_COLS:off_mla], z(MLA_NOPE - n_gate),
        w[..., off_mla + lat:off_mla + lat + MLA_ROPE], w[..., off_mla + lat:off_mla + lat + half],
        z(LANES - MLA_QK - half)], axis=-1)
    assert out.shape[-1] == U_COLS
    return out.astype(BF16)


def _rope_tables(seq, dim, lane_cos, lane_sin):
    inv = ROPE_THETA ** (-jnp.arange(0, dim, 2, dtype=F32) / dim)
    ang = jnp.arange(seq).astype(F32)[:, None] * inv[None, :]
    cos, sin = jnp.cos(ang), jnp.sin(ang)
    return lane_cos(cos), lane_sin(sin)


def _nsa_tables(seq):
    return _rope_tables(seq, HEAD_DIM,
                        lambda c: jnp.concatenate([c, c, c, c], axis=1),
                        lambda s: jnp.concatenate([-s, s, -s, s], axis=1))


def _mla_tables(seq):
    ones = jnp.ones((seq, MLA_NOPE), F32)
    zeros = jnp.zeros((seq, MLA_NOPE), F32)
    pad0 = jnp.zeros((seq, LANES - MLA_QK), F32)
    return _rope_tables(seq, MLA_ROPE,
                        lambda c: jnp.concatenate([ones, c, c, pad0], axis=1),
                        lambda s: jnp.concatenate([zeros, -s, s, pad0], axis=1))


def _mla_head_lanes(v):
    half = MLA_ROPE // 2
    pad = jnp.zeros(v.shape[:-1] + (LANES - MLA_QK - half,), v.dtype)
    return jnp.concatenate([v, v[..., MLA_NOPE:MLA_NOPE + half], pad], axis=-1)


def _overlap_matrix(seq):
    n_cmp = (seq - NSA_CMP_LEN) // NSA_CMP_STRIDE + 1
    n_sblk = seq // NSA_SEL_LEN
    starts = np.arange(n_cmp) * NSA_CMP_STRIDE
    sel_start = np.arange(n_sblk) * NSA_SEL_LEN
    ov = np.clip(np.minimum(starts[:, None] + NSA_CMP_LEN, sel_start[None, :] + NSA_SEL_LEN)
                 - np.maximum(starts[:, None], sel_start[None, :]), 0, None) / NSA_CMP_LEN
    full = np.zeros((LANES, LANES), np.float32)
    full[:n_sblk, :n_cmp] = ov.T
    return jnp.asarray(full, BF16)


def _mixer(x2, mod, batch, seq, p, w_in_all, w_o_all, layer):
    m_rows = x2.shape[0]
    usb, u2 = _in_proj(x2, mod, p["norm_mix"].reshape(1, -1), w_in_all, layer, seq)
    u3 = u2.reshape(batch, seq, U_COLS - U_SB_COLS)

    o_sb = _sb_attention(usb.reshape(batch, seq, U_SB_COLS))
    o_conv = _conformer(u3, p["conv_dw_w"], p["conv_dw_b"], p["conv_ln_g"], p["conv_ln_b"],
                        p["conv_pw_w"], p["conv_pw_b"])

    cos_n, sin_n = _nsa_tables(seq)
    q_gain = jnp.tile(p["nsa_q_norm"], 2).reshape(1, LANES)
    k_gain = jnp.tile(p["nsa_k_norm"], (1, 2))
    qn, kc, kse, kso, vsd, kwd, vwd, gates = _nsa_prep(u2, cos_n, sin_n, q_gain, k_gain, seq)
    pe2 = jnp.tile(p["nsa_cmp_pe"], (1, 1, 2))
    w_l = p["nsa_cmp_w"].reshape(2, NSA_CMP_LEN, HEAD_DIM, HEAD_DIM)
    zero = jnp.zeros_like(w_l)
    w_bd = jnp.concatenate([jnp.concatenate([w_l, zero], axis=-1),
                            jnp.concatenate([zero, w_l], axis=-1)], axis=-2).astype(BF16)
    kcmp, vcmp = _compress(kc.reshape(batch, seq, LANES), u3, pe2[0], pe2[1], w_bd[0], w_bd[1])
    o_nsa = _nsa_attention(qn.reshape(batch, seq, -1), kcmp, vcmp,
                           kse.reshape(batch, seq, -1), kso.reshape(batch, seq, -1), vsd.reshape(batch, seq, -1),
                           kwd.reshape(batch, seq, -1), vwd.reshape(batch, seq, -1),
                           gates.reshape(batch, seq, -1), _overlap_matrix(seq))

    cos_m, sin_m = _mla_tables(seq)
    w_uq = p["mla_w_uq"].reshape(MLA_Q_RANK, MLA_HEADS, MLA_QK)
    wq = _mla_head_lanes(w_uq).reshape(MLA_Q_RANK, MLA_HEADS * LANES).astype(BF16)
    w_ukv = p["mla_w_ukv"].reshape(MLA_KV_RANK, MLA_HEADS, 2 * HEAD_DIM)
    wk_m = jnp.pad(w_ukv[:, :, :MLA_NOPE], ((0, 0), (0, 0), (0, LANES - MLA_NOPE)))
    wk_m = wk_m.reshape(MLA_KV_RANK, MLA_HEADS * LANES).astype(BF16)
    wv_m = w_ukv[:, :, MLA_NOPE:].reshape(MLA_KV_RANK, GROUP_WIDTH).astype(BF16)
    qm, km, vm = _mla_prep(u2, cos_m, sin_m, p["mla_q_lat_norm"].reshape(1, -1),
                           p["mla_kv_lat_norm"].reshape(1, -1), wq, wk_m, wv_m,
                           _mla_head_lanes(p["mla_q_norm"]).reshape(1, LANES),
                           _mla_head_lanes(p["mla_k_norm"]).reshape(1, LANES), seq)
    o_mla = _mla_attention(qm.reshape(batch, seq, -1), km.reshape(batch, seq, -1), vm.reshape(batch, seq, -1))

    parts = [o.reshape(m_rows, GROUP_WIDTH) for o in (o_sb, o_conv, o_nsa, o_mla)]
    return _out_proj(parts, p["group_norm"].reshape(1, -1), w_o_all, layer, x2, mod, seq)


def kernel(x, c, ada_w, ada_b, norm_mix, norm_ffn, w_in, conv_dw_w, conv_dw_b, conv_ln_g, conv_ln_b, conv_pw_w, conv_pw_b, nsa_q_norm, nsa_k_norm, nsa_cmp_pe, nsa_cmp_w, mla_q_lat_norm, mla_kv_lat_norm, mla_w_uq, mla_w_ukv, mla_q_norm, mla_k_norm, group_norm, w_o, ffn_up, ffn_conv_w, ffn_conv_b, ffn_down):
    batch, seq, d = x.shape
    depth = ada_w.shape[0]
    per_layer = dict(
        norm_mix=norm_mix, conv_dw_w=conv_dw_w, conv_dw_b=conv_dw_b, conv_ln_g=conv_ln_g,
        conv_ln_b=conv_ln_b, conv_pw_w=conv_pw_w, conv_pw_b=conv_pw_b, nsa_q_norm=nsa_q_norm,
        nsa_k_norm=nsa_k_norm, nsa_cmp_pe=nsa_cmp_pe, nsa_cmp_w=nsa_cmp_w, mla_q_lat_norm=mla_q_lat_norm,
        mla_kv_lat_norm=mla_kv_lat_norm, mla_w_uq=mla_w_uq, mla_w_ukv=mla_w_ukv, mla_q_norm=mla_q_norm,
        mla_k_norm=mla_k_norm, group_norm=group_norm)
    w_in_all = _reorder_w_in(w_in)
    w_o_all = w_o.astype(BF16)
    ffn_up_all = ffn_up.astype(BF16)
    ffn_down_all = ffn_down.astype(BF16)
    mods = _ada(c, ada_w, ada_b)
    x2 = x.reshape(batch * seq, d)
    for l in range(depth):
        p = {k: v[l] for k, v in per_layer.items()}
        x2 = _mixer(x2, mods[l], batch, seq, p, w_in_all, w_o_all, l)
        x2 = _ffn(x2, mods[l], norm_ffn[l].reshape(1, -1), ffn_up_all, ffn_conv_w[l], ffn_conv_b[l],
                  ffn_down_all, l, seq)
    return x2.reshape(batch, seq, d)
```

```python
import functools

import numpy as np
import jax
import jax.numpy as jnp
from jax import lax
from jax.experimental import pallas as pl
from jax.experimental.pallas import tpu as pltpu

F32 = jnp.float32
BF16 = jnp.bfloat16

LANES = 128
VMEM_LIMIT = 56 * 1024 * 1024

HEAD_DIM = 64
HEAD_SHIFT = 6
ROPE_THETA = 10000.0
EPS = 1e-6
GROUP_WIDTH = 512
CONV_WIDTH = 31
CONV_LN_EPS = 1e-5
NSA_CMP_LEN = 32
NSA_CMP_STRIDE = 16
NSA_SEL_LEN = 64
SEL_SHIFT = 6
NSA_N_SEL = 16
NSA_WINDOW = 512
NSA_FORCE_BONUS = 1e3
MLA_Q_RANK = 384
MLA_KV_RANK = 256
MLA_NOPE = 64
MLA_ROPE = 32
MLA_QK = MLA_NOPE + MLA_ROPE
MLA_HEADS = 8
HEADS_PER_STEP = 4

U_SBQ, U_SBK, U_SBV = 0, 512, 1024
U_SB_COLS = 1536
U_CA, U_CG = 0, 512
U_NQ = 1024
U_NKV = 1536
U_MLA = 2304
U_COLS = 4608
REF_ALIGNED_COLS = 3840

TQ = 256
TK = 256
NEG = -1e30
LOG2E = 1.4426950408889634


def _cparams(*sem):
    return pltpu.CompilerParams(dimension_semantics=sem, vmem_limit_bytes=VMEM_LIMIT)


def _dot(a, b):
    return jnp.dot(a, b, preferred_element_type=F32)


def _dot_nt(a, b):
    return lax.dot_general(a, b, (((1,), (1,)), ((), ())), preferred_element_type=F32)


def _split_dot(x, w):
    hi = x.astype(BF16)
    lo = (x - hi.astype(F32)).astype(BF16)
    return _dot(hi, w) + _dot(lo, w)


def _lane_iota(n=LANES):
    return lax.broadcasted_iota(jnp.int32, (1, n), 1)


def _ada_kernel(c_ref, w_ref, b_ref, o_ref):
    c = c_ref[...]
    cond = (c * jax.nn.sigmoid(c)).astype(BF16)
    o_ref[0] = _dot(cond, w_ref[0].astype(BF16)) + b_ref[0]


def _ada(c, ada_w, ada_b):
    n_layers, d, n = ada_w.shape
    b = c.shape[0]
    rows = 8
    tn = 1024
    cpad = jnp.pad(c, ((0, rows - b), (0, 0)))
    out = pl.pallas_call(
        _ada_kernel,
        grid=(n_layers, n // tn),
        in_specs=[
            pl.BlockSpec((rows, d), lambda l, j: (0, 0)),
            pl.BlockSpec((1, d, tn), lambda l, j: (l, 0, j)),
            pl.BlockSpec((1, 1, tn), lambda l, j: (l, 0, j)),
        ],
        out_specs=pl.BlockSpec((1, rows, tn), lambda l, j: (l, 0, j)),
        out_shape=jax.ShapeDtypeStruct((n_layers, rows, n), F32),
        compiler_params=_cparams("parallel", "parallel"),
        name="ada_mod",
    )(cpad, ada_w, ada_b.reshape(n_layers, 1, n))
    return out[:, :b].reshape(n_layers, b, 6, d)


def _modulated_norm(x, g, shift, scale):
    ms = jnp.mean(x * x, axis=-1, keepdims=True)
    return x * lax.rsqrt(ms + EPS) * g * (1.0 + scale) + shift


def _in_kernel(x_ref, mod_ref, g_ref, w_ref, sb_ref, o_ref, h_ref, *, n_sb):
    j = pl.program_id(1)

    @pl.when(j == 0)
    def _():
        m = mod_ref[0]
        h_ref[...] = _modulated_norm(x_ref[...], g_ref[...], m[0:1], m[1:2]).astype(BF16)

    @pl.when(j < n_sb)
    def _():
        sb_ref[...] = _dot(h_ref[...], w_ref[...]).astype(BF16)

    @pl.when(j >= n_sb)
    def _():
        o_ref[...] = _dot(h_ref[...], w_ref[...])


def _in_proj(x2, mod, g, w, layer, seq):
    m_rows, d = x2.shape
    n = w.shape[2]
    tm, tn = 1024, 768
    per_seq = seq // tm
    n_sb = U_SB_COLS // tn
    return pl.pallas_call(
        functools.partial(_in_kernel, n_sb=n_sb),
        grid=(m_rows // tm, n // tn),
        in_specs=[
            pl.BlockSpec((tm, d), lambda i, j: (i, 0)),
            pl.BlockSpec((1, 6, d), lambda i, j: (i // per_seq, 0, 0)),
            pl.BlockSpec((1, d), lambda i, j: (0, 0)),
            pl.BlockSpec((None, d, tn), lambda i, j: (layer, 0, j)),
        ],
        out_specs=[
            pl.BlockSpec((tm, tn), lambda i, j: (i, jnp.minimum(j, n_sb - 1))),
            pl.BlockSpec((tm, tn), lambda i, j: (i, jnp.maximum(j - n_sb, 0))),
        ],
        out_shape=[jax.ShapeDtypeStruct((m_rows, U_SB_COLS), BF16),
                   jax.ShapeDtypeStruct((m_rows, n - U_SB_COLS), F32)],
        scratch_shapes=[pltpu.VMEM((tm, d), BF16)],
        compiler_params=_cparams("parallel", "arbitrary"),
        name="in_proj",
    )(x2, mod, g, w)


def _diag_masks():
    r = lax.broadcasted_iota(jnp.int32, (TQ, TK), 0)
    c = lax.broadcasted_iota(jnp.int32, (TQ, TK), 1)
    return c <= r, c < r


def _sb_kernel(q_ref, k_ref, v_ref, o_ref, z_ref, a_ref, *, nq):
    i = pl.program_id(2)
    lane = _lane_iota()
    heads = range(HEADS_PER_STEP)
    pair = lambda hd: slice((hd // 2) * LANES, (hd // 2 + 1) * LANES)
    qs = []
    for pr in range(HEADS_PER_STEP // 2):
        q2 = q_ref[0, :, pr * LANES:(pr + 1) * LANES] * (HEAD_DIM ** -0.5)
        qs.append(jnp.where(lane < HEAD_DIM, q2, jnp.zeros_like(q2)))
        qs.append(jnp.where(lane >= HEAD_DIM, q2, jnp.zeros_like(q2)))
    tri = (lax.broadcasted_iota(jnp.int32, (TK, TK), 0)
           > lax.broadcasted_iota(jnp.int32, (TK, TK), 1)).astype(BF16)
    tri2 = jnp.concatenate([tri, tri], axis=0)
    _, strict = _diag_masks()

    for c in range(nq):
        @pl.when(i == c)
        def _():
            kv_len = (c + 1) * TK
            for hd in heads:
                z_ref[hd, :, :kv_len] = _dot_nt(qs[hd], k_ref[0, :kv_len, pair(hd)])
            carry = [jnp.zeros((TQ, 1), F32) for _ in heads]
            for j in reversed(range(c + 1)):
                sl = slice(j * TK, (j + 1) * TK)
                for hd in heads:
                    z = z_ref[hd, :, sl]
                    sp = jnp.log(1.0 + jnp.exp(-jnp.abs(z)))
                    log_beta = jnp.minimum(z, 0.0) - sp
                    neg_1m = jnp.maximum(z, 0.0) + sp
                    if j == c:
                        neg_1m = jnp.where(strict, neg_1m, 0.0)
                    hi = neg_1m.astype(BF16)
                    pieces = jnp.concatenate([hi, (neg_1m - hi.astype(F32)).astype(BF16)], axis=1)
                    later = _dot(pieces, tri2)
                    a = jnp.exp(log_beta - later - carry[hd])
                    if j == c:
                        a = jnp.where(strict, a, 0.0)
                    a_ref[hd, :, sl] = a.astype(BF16)
                    carry[hd] = carry[hd] + jnp.sum(neg_1m, axis=-1, keepdims=True)
            outs = [_dot(a_ref[hd, :, :kv_len], v_ref[0, :kv_len, pair(hd)]) for hd in heads]
            for pr in range(HEADS_PER_STEP // 2):
                o_ref[0, :, pr * LANES:(pr + 1) * LANES] = jnp.where(lane < HEAD_DIM, outs[2 * pr], outs[2 * pr + 1])


def _sb_attention(usb3):
    b, t, _ = usb3.shape
    nq = t // TQ
    wide = HEADS_PER_STEP * HEAD_DIM
    return pl.pallas_call(
        functools.partial(_sb_kernel, nq=nq),
        grid=(b, GROUP_WIDTH // wide, nq),
        in_specs=[
            pl.BlockSpec((1, TQ, wide), lambda bb, p, i: (bb, i, U_SBQ // wide + p)),
            pl.BlockSpec((1, t, wide), lambda bb, p, i: (bb, 0, U_SBK // wide + p)),
            pl.BlockSpec((1, t, wide), lambda bb, p, i: (bb, 0, U_SBV // wide + p)),
        ],
        out_specs=pl.BlockSpec((1, TQ, wide), lambda bb, p, i: (bb, i, p)),
        out_shape=jax.ShapeDtypeStruct((b, t, GROUP_WIDTH), F32),
        scratch_shapes=[pltpu.VMEM((HEADS_PER_STEP, TQ, t), F32), pltpu.VMEM((HEADS_PER_STEP, TQ, t), BF16)],
        compiler_params=_cparams("parallel", "parallel", "arbitrary"),
        name="sb_attention",
    )(usb3, usb3, usb3)


CONV_HALO = 32


def _conv_kernel(a_ref, g_ref, ah_ref, gh_ref, dww_ref, dwb_ref, lng_ref, lnb_ref, pw_ref, pwb_ref,
                 o_ref, h_ref, s_ref, *, tm):
    i = pl.program_id(1)
    h_ref[CONV_HALO:, :] = a_ref[0] * jax.nn.sigmoid(g_ref[0])
    halo = ah_ref[0] * jax.nn.sigmoid(gh_ref[0])
    h_ref[0:CONV_HALO, :] = jnp.where(i == 0, 0.0, halo)
    acc = jnp.zeros((tm, GROUP_WIDTH), F32) + dwb_ref[...]
    base = CONV_HALO - (CONV_WIDTH - 1)
    sub = 8
    for r in range(sub):
        taps = [k for k in range(CONV_WIDTH) if (base + k) % sub == r]
        if not taps:
            continue
        lo = base + taps[0]
        span = (taps[-1] - taps[0]) + tm
        s_ref[r, 0:span, :] = h_ref[lo:lo + span, :]
        for k in taps:
            acc = acc + dww_ref[k:k + 1, :] * s_ref[r, k - taps[0]:k - taps[0] + tm, :]
    mu = jnp.mean(acc, axis=-1, keepdims=True)
    cen = acc - mu
    var = jnp.mean(cen * cen, axis=-1, keepdims=True)
    hn = cen * lax.rsqrt(var + CONV_LN_EPS) * lng_ref[...] + lnb_ref[...]
    act = hn * jax.nn.sigmoid(hn)
    o_ref[0] = _dot(act.astype(BF16), pw_ref[...]) + pwb_ref[...]


def _conformer(u3, dw_w, dw_b, ln_g, ln_b, pw_w, pw_b):
    b, t, _ = u3.shape
    tm = 512
    cw = GROUP_WIDTH
    hb = tm // CONV_HALO

    def halo_map(col):
        return lambda bb, i: (bb, jnp.maximum(i * hb - 1, 0), col)

    row = lambda bb, i: (0, 0)
    return pl.pallas_call(
        functools.partial(_conv_kernel, tm=tm),
        grid=(b, t // tm),
        in_specs=[
            pl.BlockSpec((1, tm, cw), lambda bb, i: (bb, i, U_CA // cw)),
            pl.BlockSpec((1, tm, cw), lambda bb, i: (bb, i, U_CG // cw)),
            pl.BlockSpec((1, CONV_HALO, cw), halo_map(U_CA // cw)),
            pl.BlockSpec((1, CONV_HALO, cw), halo_map(U_CG // cw)),
            pl.BlockSpec((CONV_WIDTH, cw), row),
            pl.BlockSpec((1, cw), row),
            pl.BlockSpec((1, cw), row),
            pl.BlockSpec((1, cw), row),
            pl.BlockSpec((cw, cw), row),
            pl.BlockSpec((1, cw), row),
        ],
        out_specs=pl.BlockSpec((1, tm, cw), lambda bb, i: (bb, i, 0)),
        out_shape=jax.ShapeDtypeStruct((b, t, cw), F32),
        scratch_shapes=[pltpu.VMEM((tm + CONV_HALO, cw), F32), pltpu.VMEM((8, tm + CONV_HALO, cw), F32)],
        compiler_params=_cparams("parallel", "parallel"),
        name="conformer_conv",
    )(u3, u3, u3, u3, dw_w, dw_b.reshape(1, cw), ln_g.reshape(1, cw), ln_b.reshape(1, cw),
      pw_w.astype(BF16), pw_b.reshape(1, cw))


def _head_norm_rope(x, gain, cos, sin_signed, seg_mean):
    ms = _split_dot(x * x, seg_mean)
    y = x * lax.rsqrt(ms + EPS) * gain
    lane = _lane_iota()
    half = HEAD_DIM // 2
    first = (lane & (HEAD_DIM - 1)) < half
    partner = jnp.where(first, pltpu.roll(y, LANES - half, 1), pltpu.roll(y, half, 1))
    return y * cos + partner * sin_signed


def _dup(x, g):
    lane = _lane_iota()
    sw = pltpu.roll(x, HEAD_DIM, 1)
    if g == 0:
        return jnp.where(lane < HEAD_DIM, x, sw)
    return jnp.where(lane < HEAD_DIM, sw, x)


def _nsa_prep_kernel(q_ref, kv_ref, gk_ref, cos_ref, sin_ref, qg_ref, kg_ref,
                     qn_ref, kc_ref, kse_ref, kso_ref, vs_ref, kw_ref, vw_ref, gt_ref, *, per_seq):
    cos = cos_ref[...]
    sin = sin_ref[...]
    seg = lax.broadcasted_iota(jnp.int32, (LANES, LANES), 0) >> HEAD_SHIFT
    seg_mean = jnp.where(seg == lax.broadcasted_iota(jnp.int32, (LANES, LANES), 1) >> HEAD_SHIFT,
                         1.0 / HEAD_DIM, 0.0).astype(BF16)
    scale = HEAD_DIM ** -0.5
    for p in range(GROUP_WIDTH // LANES):
        x = q_ref[:, p * LANES:(p + 1) * LANES]
        qn_ref[:, p * LANES:(p + 1) * LANES] = (
            _head_norm_rope(x, qg_ref[...], cos, sin, seg_mean) * scale).astype(BF16)

    def blk(n):
        return kv_ref[:, n * LANES:(n + 1) * LANES]

    kc_ref[...] = _head_norm_rope(blk(0), kg_ref[0:1, :], cos, sin, seg_mean)
    ks = _head_norm_rope(blk(2), kg_ref[1:2, :], cos, sin, seg_mean)
    kw = _head_norm_rope(blk(4), kg_ref[2:3, :], cos, sin, seg_mean)
    vs = blk(3)
    vw = blk(5)
    tm = ks.shape[0]
    lane = _lane_iota()
    t = (pl.program_id(0) % per_seq) * tm + lax.broadcasted_iota(jnp.int32, (tm, 1), 0)
    sblk = t >> SEL_SHIFT
    hot_lo = jnp.where(lane == sblk, 1.0, 0.0)
    hot_hi = jnp.where(lane - HEAD_DIM == sblk, 1.0, 0.0)
    ks_sw = pltpu.roll(ks, HEAD_DIM, 1)
    for g in range(2):
        sl = slice(g * LANES, (g + 1) * LANES)
        kse_ref[:, sl] = jnp.where(lane < HEAD_DIM, ks if g == 0 else ks_sw, hot_hi).astype(BF16)
        kso_ref[:, sl] = jnp.where(lane >= HEAD_DIM, ks_sw if g == 0 else ks, hot_lo).astype(BF16)
        vs_ref[:, sl] = _dup(vs, g).astype(BF16)
        kw_ref[:, sl] = _dup(kw, g).astype(BF16)
        vw_ref[:, sl] = _dup(vw, g).astype(BF16)
    gates = jax.nn.sigmoid(gk_ref[...])
    gt_ref[:, 0:LANES] = gates
    gt_ref[:, LANES:2 * LANES] = pltpu.roll(gates, LANES - 12, 1)


def _nsa_prep(u2, cos, sin, q_gain, k_gain, seq):
    m_rows = u2.shape[0]
    tm = 512
    per_seq = seq // tm
    rowmap = lambda i: (i, 0)
    tab = lambda i: (i % per_seq, 0)
    const = lambda i: (0, 0)
    outs = [
        jax.ShapeDtypeStruct((m_rows, GROUP_WIDTH), BF16),
        jax.ShapeDtypeStruct((m_rows, LANES), F32),
        jax.ShapeDtypeStruct((m_rows, 2 * LANES), BF16),
        jax.ShapeDtypeStruct((m_rows, 2 * LANES), BF16),
        jax.ShapeDtypeStruct((m_rows, 2 * LANES), BF16),
        jax.ShapeDtypeStruct((m_rows, 2 * LANES), BF16),
        jax.ShapeDtypeStruct((m_rows, 2 * LANES), BF16),
        jax.ShapeDtypeStruct((m_rows, 2 * LANES), F32),
    ]
    return pl.pallas_call(
        functools.partial(_nsa_prep_kernel, per_seq=per_seq),
        grid=(m_rows // tm,),
        in_specs=[
            pl.BlockSpec((tm, GROUP_WIDTH), lambda i: (i, U_NQ // GROUP_WIDTH)),
            pl.BlockSpec((tm, 6 * LANES), lambda i: (i, U_NKV // (6 * LANES))),
            pl.BlockSpec((tm, LANES), lambda i: (i, (U_MLA + MLA_Q_RANK + MLA_KV_RANK) // LANES)),
            pl.BlockSpec((tm, LANES), tab),
            pl.BlockSpec((tm, LANES), tab),
            pl.BlockSpec((1, LANES), const),
            pl.BlockSpec((3, LANES), const),
        ],
        out_specs=[
            pl.BlockSpec((tm, GROUP_WIDTH), rowmap),
            pl.BlockSpec((tm, LANES), rowmap),
            pl.BlockSpec((tm, 2 * LANES), rowmap),
            pl.BlockSpec((tm, 2 * LANES), rowmap),
            pl.BlockSpec((tm, 2 * LANES), rowmap),
            pl.BlockSpec((tm, 2 * LANES), rowmap),
            pl.BlockSpec((tm, 2 * LANES), rowmap),
            pl.BlockSpec((tm, 2 * LANES), rowmap),
        ],
        out_shape=outs,
        compiler_params=_cparams("parallel"),
        name="nsa_prep",
    )(u2, u2, u2, cos, sin, q_gain, k_gain)


def _compress_kernel(xk_ref, xv_ref, pek_ref, pev_ref, wk_ref, wv_ref, kc_ref, vc_ref, *, nrow):
    stride = NSA_CMP_STRIDE

    def comp(x_ref, pe_ref, w_ref):
        first = jnp.zeros((nrow, LANES), F32)
        second = jnp.zeros((nrow, LANES), F32)
        for r in range(stride):
            x = x_ref[0, pl.ds(r, nrow, stride=stride), :]
            first = first + _dot((x + pe_ref[r:r + 1, :]).astype(BF16), w_ref[r])
            second = second + _dot((x + pe_ref[stride + r:stride + r + 1, :]).astype(BF16), w_ref[stride + r])
        return first + pltpu.roll(second, nrow - 1, 0)

    ck = comp(xk_ref, pek_ref, wk_ref)
    cv = comp(xv_ref, pev_ref, wv_ref)
    for g in range(2):
        kc_ref[0, g] = _dup(ck, g).astype(BF16)
        vc_ref[0, g] = _dup(cv, g).astype(BF16)


def _compress(kc3, u3, pe_k, pe_v, wk, wv):
    b, t, _ = kc3.shape
    nrow = t // NSA_CMP_STRIDE
    const2 = lambda bb: (0, 0)
    const3 = lambda bb: (0, 0, 0)
    out = jax.ShapeDtypeStruct((b, 2, nrow, LANES), BF16)
    return pl.pallas_call(
        functools.partial(_compress_kernel, nrow=nrow),
        grid=(b,),
        in_specs=[
            pl.BlockSpec((1, t, LANES), lambda bb: (bb, 0, 0)),
            pl.BlockSpec((1, t, LANES), lambda bb: (bb, 0, U_NKV // LANES + 1)),
            pl.BlockSpec(pe_k.shape, const2),
            pl.BlockSpec(pe_v.shape, const2),
            pl.BlockSpec(wk.shape, const3),
            pl.BlockSpec(wv.shape, const3),
        ],
        out_specs=[pl.BlockSpec((1, 2, nrow, LANES), lambda bb: (bb, 0, 0, 0))] * 2,
        out_shape=[out, out],
        compiler_params=_cparams("parallel"),
        name="nsa_compress",
    )(kc3, u3, pe_k, pe_v, wk, wv)


def _attend_many(jobs, scale=None):
    tiles = [k.shape[0] // TK for (_, k, _, _, _, _) in jobs]
    for (q, k, _, s_ref, _, _), n in zip(jobs, tiles):
        s_ref[:, :n * TK] = _dot_nt(q, k)
    maxima = []
    for (_, _, _, s_ref, _, masks), n in zip(jobs, tiles):
        fold = None
        for j in range(n):
            sl = slice(j * TK, (j + 1) * TK)
            s = s_ref[:, sl]
            if j in masks:
                s = jnp.where(masks[j], s, NEG)
                s_ref[:, sl] = s
            f = jnp.maximum(s[:, :LANES], s[:, LANES:])
            fold = f if fold is None else jnp.maximum(fold, f)
        maxima.append(jnp.max(fold, axis=-1, keepdims=True))
    sums = []
    for (_, _, _, s_ref, p_ref, _), n, m in zip(jobs, tiles, maxima):
        fold = None
        for j in range(n):
            sl = slice(j * TK, (j + 1) * TK)
            p = jnp.exp2((s_ref[:, sl] - m) * (LOG2E if scale is None else scale * LOG2E))
            f = p[:, :LANES] + p[:, LANES:]
            fold = f if fold is None else fold + f
            p_ref[:, sl] = p.astype(BF16)
        sums.append(jnp.sum(fold, axis=-1, keepdims=True))
    return [_dot(p_ref[:, :n * TK], v) / l for (_, _, v, _, p_ref, _), n, l in zip(jobs, tiles, sums)]


def _nsa_kernel(q_ref, kc_ref, vc_ref, kse_ref, kso_ref, vs_ref, kw_ref, vw_ref, gt_ref, ov_ref,
                o_ref, s_ref, p_ref, sw_ref, pw_ref, *, nq, n_cmp, n_sblk, n_sel):
    i = pl.program_id(2)
    lane = _lane_iota()
    row = i * TQ + lax.broadcasted_iota(jnp.int32, (TQ, 1), 0)
    rep = 4
    q2s = [q_ref[0, :, p * LANES:(p + 1) * LANES] for p in range(2)]
    qs = []
    for q2 in q2s:
        qs.append(jnp.where(lane < HEAD_DIM, q2, jnp.zeros_like(q2)))
        qs.append(jnp.where(lane >= HEAD_DIM, q2, jnp.zeros_like(q2)))

    kc = kc_ref[0, 0]
    vc = vc_ref[0, 0]
    cmask = ((lane * NSA_CMP_STRIDE + (NSA_CMP_LEN - 1)) <= row) & (lane < n_cmp)
    psum = jnp.zeros((TQ, LANES), F32)
    s_cmp = [jnp.where(cmask, _dot_nt(qs[r], kc), NEG) for r in range(rep)]
    p_cmp = []
    for r in range(rep):
        mx = jnp.max(s_cmp[r], axis=-1, keepdims=True)
        p = jnp.where(cmask, jnp.exp(s_cmp[r] - mx), 0.0)
        p = p / jnp.maximum(jnp.sum(p, axis=-1, keepdims=True), 1e-30)
        p_cmp.append(p.astype(BF16))
        psum = psum + p
    o_cmp = [_dot(p_cmp[r], vc) for r in range(rep)]
    ov_t = ov_ref[0:n_sblk, :]
    p_hi = psum.astype(BF16)
    p_lo = (psum - p_hi.astype(F32)).astype(BF16)
    imp = _dot_nt(ov_t, p_hi) + _dot_nt(ov_t, p_lo)
    cur = (i * TQ + _lane_iota(TQ)) >> SEL_SHIFT
    blk = lax.broadcasted_iota(jnp.int32, (n_sblk, 1), 0)
    eligible = blk <= cur
    forced = (blk == 0) | (blk == cur) | (blk == cur - 1)
    score = jnp.where(eligible, imp + jnp.where(forced, NSA_FORCE_BONUS, 0.0), NEG)
    rank = jnp.zeros((n_sblk, TQ), F32)
    for sp in range(n_sblk):
        other = score[sp:sp + 1, :]
        beats = (other > score) | ((other == score) & (blk > sp))
        rank = rank + jnp.where(beats, 1.0, 0.0)
    bias_t = jnp.where(eligible & (rank < n_sel), 0.0, NEG)

    bias_lo = jnp.concatenate([bias_t, jnp.zeros((LANES - n_sblk, TQ), F32)], axis=0).T
    bias_hi = pltpu.roll(bias_lo, HEAD_DIM, 1).astype(BF16)
    bias_lo = bias_lo.astype(BF16)
    q_aug = []
    for q2 in q2s:
        q_aug.append(jnp.where(lane < HEAD_DIM, q2, bias_hi))
        q_aug.append(jnp.where(lane >= HEAD_DIM, q2, bias_lo))
    causal, strict = _diag_masks()
    wtiles = NSA_WINDOW // TK
    gt = gt_ref[0]

    for c in range(nq):
        @pl.when(i == c)
        def _():
            kv_len = (c + 1) * TK
            lo = max(c - wtiles, 0) * TK
            wmasks = {c - max(c - wtiles, 0): causal}
            if c >= wtiles:
                wmasks[0] = jnp.logical_not(causal)
            jobs = []
            for r in range(rep):
                ks_ref = kse_ref if r % 2 == 0 else kso_ref
                jobs.append((q_aug[r], ks_ref[0, :kv_len, :], vs_ref[0, :kv_len, :],
                             s_ref.at[r], p_ref.at[r], {c: causal}))
            for r in range(rep):
                jobs.append((qs[r], kw_ref[0, lo:kv_len, :], vw_ref[0, lo:kv_len, :],
                             sw_ref.at[r], pw_ref.at[r], wmasks))
            res = _attend_many(jobs)
            outs = []
            for r in range(rep):
                outs.append(gt[:, 3 * r:3 * r + 1] * o_cmp[r] + gt[:, 3 * r + 1:3 * r + 2] * res[r]
                            + gt[:, 3 * r + 2:3 * r + 3] * res[rep + r])
            o_ref[0, :, 0:LANES] = jnp.where(lane < HEAD_DIM, outs[0], outs[1])
            o_ref[0, :, LANES:2 * LANES] = jnp.where(lane < HEAD_DIM, outs[2], outs[3])


def _nsa_attention(qn, kcmp, vcmp, kse, kso, vsd, kwd, vwd, gates, overlap):
    b, t, _ = qn.shape
    nq = t // TQ
    n_cmp = (t - NSA_CMP_LEN) // NSA_CMP_STRIDE + 1
    n_sblk = t // NSA_SEL_LEN
    nrow = kcmp.shape[2]
    wlen = NSA_WINDOW + TK
    kv = lambda bb, g, i: (bb, 0, g)
    cm = lambda bb, g, i: (bb, g, 0, 0)
    return pl.pallas_call(
        functools.partial(_nsa_kernel, nq=nq, n_cmp=n_cmp, n_sblk=n_sblk, n_sel=min(NSA_N_SEL, n_sblk)),
        grid=(b, 2, nq),
        in_specs=[
            pl.BlockSpec((1, TQ, 2 * LANES), lambda bb, g, i: (bb, i, g)),
            pl.BlockSpec((1, 1, nrow, LANES), cm),
            pl.BlockSpec((1, 1, nrow, LANES), cm),
            pl.BlockSpec((1, t, LANES), kv),
            pl.BlockSpec((1, t, LANES), kv),
            pl.BlockSpec((1, t, LANES), kv),
            pl.BlockSpec((1, t, LANES), kv),
            pl.BlockSpec((1, t, LANES), kv),
            pl.BlockSpec((1, TQ, LANES), lambda bb, g, i: (bb, i, g)),
            pl.BlockSpec((LANES, LANES), lambda bb, g, i: (0, 0)),
        ],
        out_specs=pl.BlockSpec((1, TQ, 2 * LANES), lambda bb, g, i: (bb, i, g)),
        out_shape=jax.ShapeDtypeStruct((b, t, GROUP_WIDTH), F32),
        scratch_shapes=[pltpu.VMEM((4, TQ, t), F32), pltpu.VMEM((4, TQ, t), BF16),
                        pltpu.VMEM((4, TQ, wlen), F32), pltpu.VMEM((4, TQ, wlen), BF16)],
        compiler_params=_cparams("parallel", "parallel", "arbitrary"),
        name="nsa_attention",
    )(qn, kcmp, vcmp, kse, kso, vsd, kwd, vwd, gates, overlap)


def _mla_prep_kernel(u_ref, cos_ref, sin_ref, qlg_ref, kvlg_ref, wq_ref, wk_ref, wv_ref, qg_ref, kg_ref,
                     q_ref, k_ref, v_ref):
    lane = _lane_iota()
    cos = cos_ref[...]
    sin = sin_ref[...]
    half = MLA_ROPE // 2

    def lat_norm(x, g):
        ms = jnp.mean(x * x, axis=-1, keepdims=True)
        return (x * lax.rsqrt(ms + EPS) * g).astype(BF16)

    live = lane < MLA_QK

    def head_norm_rope(x, g):
        ms = jnp.sum(jnp.where(live, x * x, 0.0), axis=-1, keepdims=True) * (1.0 / MLA_QK)
        y = x * lax.rsqrt(ms + EPS) * g
        return y * cos + pltpu.roll(y, LANES - half, 1) * sin

    q = _dot(lat_norm(u_ref[:, 0:MLA_Q_RANK], qlg_ref[...]), wq_ref[...])
    kvn = lat_norm(u_ref[:, MLA_Q_RANK:MLA_Q_RANK + MLA_KV_RANK], kvlg_ref[...])
    kn = _dot(kvn, wk_ref[...])
    v_ref[...] = _dot(kvn, wv_ref[...]).astype(BF16)
    gk = u_ref[:, MLA_Q_RANK + MLA_KV_RANK:MLA_Q_RANK + MLA_KV_RANK + LANES]
    k_rope = jnp.where(lane >= MLA_NOPE, gk, 0.0)
    for h in range(MLA_HEADS):
        sl = slice(h * LANES, (h + 1) * LANES)
        q_ref[:, sl] = head_norm_rope(q[:, sl], qg_ref[...]).astype(BF16)
        k_ref[:, sl] = head_norm_rope(kn[:, sl] + k_rope, kg_ref[...]).astype(BF16)


def _mla_prep(u2, cos, sin, q_lat_gain, kv_lat_gain, wq, wk, wv, q_gain, k_gain, seq):
    m_rows = u2.shape[0]
    tm = 512
    per_seq = seq // tm
    width = MLA_Q_RANK + MLA_KV_RANK + LANES
    rowmap = lambda i: (i, 0)
    tab = lambda i: (i % per_seq, 0)
    const = lambda i: (0, 0)
    return pl.pallas_call(
        _mla_prep_kernel,
        grid=(m_rows // tm,),
        in_specs=[
            pl.BlockSpec((tm, width), lambda i: (i, U_MLA // width)),
            pl.BlockSpec((tm, LANES), tab),
            pl.BlockSpec((tm, LANES), tab),
            pl.BlockSpec((1, MLA_Q_RANK), const),
            pl.BlockSpec((1, MLA_KV_RANK), const),
            pl.BlockSpec(wq.shape, const),
            pl.BlockSpec(wk.shape, const),
            pl.BlockSpec(wv.shape, const),
            pl.BlockSpec((1, LANES), const),
            pl.BlockSpec((1, LANES), const),
        ],
        out_specs=[
            pl.BlockSpec((tm, MLA_HEADS * LANES), rowmap),
            pl.BlockSpec((tm, MLA_HEADS * LANES), rowmap),
            pl.BlockSpec((tm, GROUP_WIDTH), rowmap),
        ],
        out_shape=[
            jax.ShapeDtypeStruct((m_rows, MLA_HEADS * LANES), BF16),
            jax.ShapeDtypeStruct((m_rows, MLA_HEADS * LANES), BF16),
            jax.ShapeDtypeStruct((m_rows, GROUP_WIDTH), BF16),
        ],
        compiler_params=_cparams("parallel"),
        name="mla_prep",
    )(u2, cos, sin, q_lat_gain, kv_lat_gain, wq, wk, wv, q_gain, k_gain)


def _mla_kernel(q_ref, k_ref, v_ref, o_ref, s_ref, p_ref, *, nq):
    i = pl.program_id(2)
    lane = _lane_iota()
    causal, _ = _diag_masks()
    scale = MLA_QK ** -0.5
    for c in range(nq):
        @pl.when(i == c)
        def _():
            kv_len = (c + 1) * TK
            jobs = []
            for hd in range(HEADS_PER_STEP):
                sl = slice(hd * LANES, (hd + 1) * LANES)
                vs = slice((hd // 2) * LANES, (hd // 2 + 1) * LANES)
                jobs.append((q_ref[0, :, sl], k_ref[0, :kv_len, sl], v_ref[0, :kv_len, vs],
                             s_ref.at[hd], p_ref.at[hd], {c: causal}))
            outs = _attend_many(jobs, scale=scale)
            for pr in range(HEADS_PER_STEP // 2):
                o_ref[0, :, pr * LANES:(pr + 1) * LANES] = jnp.where(lane < HEAD_DIM, outs[2 * pr], outs[2 * pr + 1])


def _mla_attention(qm, km, vm):
    b, t, _ = qm.shape
    nq = t // TQ
    return pl.pallas_call(
        functools.partial(_mla_kernel, nq=nq),
        grid=(b, MLA_HEADS // HEADS_PER_STEP, nq),
        in_specs=[
            pl.BlockSpec((1, TQ, HEADS_PER_STEP * LANES), lambda bb, p, i: (bb, i, p)),
            pl.BlockSpec((1, t, HEADS_PER_STEP * LANES), lambda bb, p, i: (bb, 0, p)),
            pl.BlockSpec((1, t, HEADS_PER_STEP * HEAD_DIM), lambda bb, p, i: (bb, 0, p)),
        ],
        out_specs=pl.BlockSpec((1, TQ, HEADS_PER_STEP * HEAD_DIM), lambda bb, p, i: (bb, i, p)),
        out_shape=jax.ShapeDtypeStruct((b, t, GROUP_WIDTH), F32),
        scratch_shapes=[pltpu.VMEM((HEADS_PER_STEP, TQ, t), F32), pltpu.VMEM((HEADS_PER_STEP, TQ, t), BF16)],
        compiler_params=_cparams("parallel", "parallel", "arbitrary"),
        name="mla_attention",
    )(qm, km, vm)


def _out_kernel(a_ref, b_ref, c_ref, d_ref, gn_ref, w_ref, x_ref, mod_ref, o_ref):
    acc = None
    for gi, r in enumerate((a_ref, b_ref, c_ref, d_ref)):
        y = r[...]
        ms = jnp.mean(y * y, axis=-1, keepdims=True)
        sl = slice(gi * GROUP_WIDTH, (gi + 1) * GROUP_WIDTH)
        part = _dot((y * lax.rsqrt(ms + EPS) * gn_ref[:, sl]).astype(BF16), w_ref[sl, :])
        acc = part if acc is None else acc + part
    o_ref[...] = x_ref[...] + mod_ref[0, 2:3, :] * acc


def _out_proj(parts, gn, w, layer, x2, mod, seq):
    m_rows, d = x2.shape
    tm = 512
    per_seq = seq // tm
    part = pl.BlockSpec((tm, GROUP_WIDTH), lambda i: (i, 0))
    return pl.pallas_call(
        _out_kernel,
        grid=(m_rows // tm,),
        in_specs=[
            part, part, part, part,
            pl.BlockSpec((1, 4 * GROUP_WIDTH), lambda i: (0, 0)),
            pl.BlockSpec((None, 4 * GROUP_WIDTH, d), lambda i: (layer, 0, 0)),
            pl.BlockSpec((tm, d), lambda i: (i, 0)),
            pl.BlockSpec((1, 6, d), lambda i: (i // per_seq, 0, 0)),
        ],
        out_specs=pl.BlockSpec((tm, d), lambda i: (i, 0)),
        out_shape=jax.ShapeDtypeStruct((m_rows, d), F32),
        compiler_params=_cparams("parallel"),
        name="out_proj",
    )(*parts, gn, w, x2, mod)


FFN_HALO = 16


def _ffn_kernel(x_ref, xh_ref, mod_ref, g_ref, wa_ref, wg_ref, cwa_ref, cwg_ref, cba_ref, cbg_ref, wd_ref, wdl_ref,
                o_ref, h_ref, act_ref, acc_ref, *, per_seq, nj):
    i = pl.program_id(0)
    j = pl.program_id(1)
    m = mod_ref[0]

    def conv(u, cw_ref, cb_ref):
        y = cw_ref[2:3, :] * u + cw_ref[1:2, :] * pltpu.roll(u, 1, 0) + cw_ref[0:1, :] * pltpu.roll(u, 2, 0)
        return y[FFN_HALO:, :] + cb_ref[...]

    def activation():
        h = h_ref[...]
        ya = conv(_dot(h, wa_ref[...]), cwa_ref, cba_ref)
        yg = conv(_dot(h, wg_ref[...]), cwg_ref, cbg_ref)
        return ((yg * jax.nn.sigmoid(yg)) * ya).astype(BF16)

    @pl.when(j == 0)
    def _():
        h_ref[FFN_HALO:, :] = _modulated_norm(x_ref[...], g_ref[...], m[3:4], m[4:5]).astype(BF16)
        halo = _modulated_norm(xh_ref[...], g_ref[...], m[3:4], m[4:5])
        h_ref[0:FFN_HALO, :] = jnp.where(i % per_seq == 0, 0.0, halo).astype(BF16)
        act_ref[...] = activation()

    @pl.when(j == 1)
    def _():
        new = activation()
        acc_ref[...] = _dot(act_ref[...], wd_ref[...])
        act_ref[...] = new

    @pl.when((j > 1) & (j < nj - 1))
    def _():
        new = activation()
        acc_ref[...] += _dot(act_ref[...], wd_ref[...])
        act_ref[...] = new

    @pl.when(j == nj - 1)
    def _():
        new = activation()
        total = acc_ref[...] + _dot(act_ref[...], wd_ref[...]) + _dot(new, wdl_ref[...])
        o_ref[...] = x_ref[...] + m[5:6] * total


def _ffn(x2, mod, g, w_up, conv_w, conv_b, w_down, layer, seq):
    m_rows, d = x2.shape
    d_ff = w_down.shape[1]
    tm, tn = 512, 512
    per_seq = seq // tm
    nj = d_ff // tn
    hb = tm // FFN_HALO
    cb = conv_b.reshape(1, 2 * d_ff)
    assert nj >= 3
    return pl.pallas_call(
        functools.partial(_ffn_kernel, per_seq=per_seq, nj=nj),
        grid=(m_rows // tm, nj),
        in_specs=[
            pl.BlockSpec((tm, d), lambda i, j: (i, 0)),
            pl.BlockSpec((FFN_HALO, d), lambda i, j: (jnp.maximum(i * hb - 1, 0), 0)),
            pl.BlockSpec((1, 6, d), lambda i, j: (i // per_seq, 0, 0)),
            pl.BlockSpec((1, d), lambda i, j: (0, 0)),
            pl.BlockSpec((None, d, tn), lambda i, j: (layer, 0, j)),
            pl.BlockSpec((None, d, tn), lambda i, j: (layer, 0, nj + j)),
            pl.BlockSpec((3, tn), lambda i, j: (0, j)),
            pl.BlockSpec((3, tn), lambda i, j: (0, nj + j)),
            pl.BlockSpec((1, tn), lambda i, j: (0, j)),
            pl.BlockSpec((1, tn), lambda i, j: (0, nj + j)),
            pl.BlockSpec((None, tn, d), lambda i, j: (layer, jnp.maximum(j - 1, 0), 0)),
            pl.BlockSpec((None, tn, d), lambda i, j: (layer, nj - 1, 0)),
        ],
        out_specs=pl.BlockSpec((tm, d), lambda i, j: (i, 0)),
        out_shape=jax.ShapeDtypeStruct((m_rows, d), F32),
        scratch_shapes=[pltpu.VMEM((tm + FFN_HALO, d), BF16), pltpu.VMEM((tm, tn), BF16),
                        pltpu.VMEM((tm, d), F32)],
        compiler_params=_cparams("parallel", "arbitrary"),
        name="conv_glu_ffn",
    )(x2, x2, mod, g, w_up, w_up, conv_w, conv_w, cb, cb, w_down, w_down)


def _reorder_w_in(w):
    n_gate = 24
    off_mla = REF_ALIGNED_COLS + n_gate
    lat = MLA_Q_RANK + MLA_KV_RANK
    z = lambda n: jnp.zeros(w.shape[:-1] + (n,), w.dtype)
    half = MLA_ROPE // 2
    out = jnp.concatenate([
        w[..., :REF_ALIGNED_COLS], w[..., off_mla:off_mla + lat],
        w[..., REF_ALIGNED_COLS:off_mla], z(MLA_NOPE - n_gate),
        w[..., off_mla + lat:off_mla + lat + MLA_ROPE], w[..., off_mla + lat:off_mla + lat + half],
        z(LANES - MLA_QK - half)], axis=-1)
    assert out.shape[-1] == U_COLS
    return out.astype(BF16)


def _rope_tables(seq, dim, lane_cos, lane_sin):
    inv = ROPE_THETA ** (-jnp.arange(0, dim, 2, dtype=F32) / dim)
    ang = jnp.arange(seq).astype(F32)[:, None] * inv[None, :]
    cos, sin = jnp.cos(ang), jnp.sin(ang)
    return lane_cos(cos), lane_sin(sin)


def _nsa_tables(seq):
    return _rope_tables(seq, HEAD_DIM,
                        lambda c: jnp.concatenate([c, c, c, c], axis=1),
                        lambda s: jnp.concatenate([-s, s, -s, s], axis=1))


def _mla_tables(seq):
    ones = jnp.ones((seq, MLA_NOPE), F32)
    zeros = jnp.zeros((seq, MLA_NOPE), F32)
    pad0 = jnp.zeros((seq, LANES - MLA_QK), F32)
    return _rope_tables(seq, MLA_ROPE,
                        lambda c: jnp.concatenate([ones, c, c, pad0], axis=1),
                        lambda s: jnp.concatenate([zeros, -s, s, pad0], axis=1))


def _mla_head_lanes(v):
    half = MLA_ROPE // 2
    pad = jnp.zeros(v.shape[:-1] + (LANES - MLA_QK - half,), v.dtype)
    return jnp.concatenate([v, v[..., MLA_NOPE:MLA_NOPE + half], pad], axis=-1)


def _overlap_matrix(seq):
    n_cmp = (seq - NSA_CMP_LEN) // NSA_CMP_STRIDE + 1
    n_sblk = seq // NSA_SEL_LEN
    starts = np.arange(n_cmp) * NSA_CMP_STRIDE
    sel_start = np.arange(n_sblk) * NSA_SEL_LEN
    ov = np.clip(np.minimum(starts[:, None] + NSA_CMP_LEN, sel_start[None, :] + NSA_SEL_LEN)
                 - np.maximum(starts[:, None], sel_start[None, :]), 0, None) / NSA_CMP_LEN
    full = np.zeros((LANES, LANES), np.float32)
    full[:n_sblk, :n_cmp] = ov.T
    return jnp.asarray(full, BF16)


def _mixer(x2, mod, batch, seq, p, w_in_all, w_o_all, layer):
    m_rows = x2.shape[0]
    usb, u2 = _in_proj(x2, mod, p["norm_mix"].reshape(1, -1), w_in_all, layer, seq)
    u3 = u2.reshape(batch, seq, U_COLS - U_SB_COLS)

    o_sb = _sb_attention(usb.reshape(batch, seq, U_SB_COLS))
    o_conv = _conformer(u3, p["conv_dw_w"], p["conv_dw_b"], p["conv_ln_g"], p["conv_ln_b"],
                        p["conv_pw_w"], p["conv_pw_b"])

    cos_n, sin_n = _nsa_tables(seq)
    q_gain = jnp.tile(p["nsa_q_norm"], 2).reshape(1, LANES)
    k_gain = jnp.tile(p["nsa_k_norm"], (1, 2))
    qn, kc, kse, kso, vsd, kwd, vwd, gates = _nsa_prep(u2, cos_n, sin_n, q_gain, k_gain, seq)
    pe2 = jnp.tile(p["nsa_cmp_pe"], (1, 1, 2))
    w_l = p["nsa_cmp_w"].reshape(2, NSA_CMP_LEN, HEAD_DIM, HEAD_DIM)
    zero = jnp.zeros_like(w_l)
    w_bd = jnp.concatenate([jnp.concatenate([w_l, zero], axis=-1),
                            jnp.concatenate([zero, w_l], axis=-1)], axis=-2).astype(BF16)
    kcmp, vcmp = _compress(kc.reshape(batch, seq, LANES), u3, pe2[0], pe2[1], w_bd[0], w_bd[1])
    o_nsa = _nsa_attention(qn.reshape(batch, seq, -1), kcmp, vcmp,
                           kse.reshape(batch, seq, -1), kso.reshape(batch, seq, -1), vsd.reshape(batch, seq, -1),
                           kwd.reshape(batch, seq, -1), vwd.reshape(batch, seq, -1),
                           gates.reshape(batch, seq, -1), _overlap_matrix(seq))

    cos_m, sin_m = _mla_tables(seq)
    w_uq = p["mla_w_uq"].reshape(MLA_Q_RANK, MLA_HEADS, MLA_QK)
    wq = _mla_head_lanes(w_uq).reshape(MLA_Q_RANK, MLA_HEADS * LANES).astype(BF16)
    w_ukv = p["mla_w_ukv"].reshape(MLA_KV_RANK, MLA_HEADS, 2 * HEAD_DIM)
    wk_m = jnp.pad(w_ukv[:, :, :MLA_NOPE], ((0, 0), (0, 0), (0, LANES - MLA_NOPE)))
    wk_m = wk_m.reshape(MLA_KV_RANK, MLA_HEADS * LANES).astype(BF16)
    wv_m = w_ukv[:, :, MLA_NOPE:].reshape(MLA_KV_RANK, GROUP_WIDTH).astype(BF16)
    qm, km, vm = _mla_prep(u2, cos_m, sin_m, p["mla_q_lat_norm"].reshape(1, -1),
                           p["mla_kv_lat_norm"].reshape(1, -1), wq, wk_m, wv_m,
                           _mla_head_lanes(p["mla_q_norm"]).reshape(1, LANES),
                           _mla_head_lanes(p["mla_k_norm"]).reshape(1, LANES), seq)
    o_mla = _mla_attention(qm.reshape(batch, seq, -1), km.reshape(batch, seq, -1), vm.reshape(batch, seq, -1))

    parts = [o.reshape(m_rows, GROUP_WIDTH) for o in (o_sb, o_conv, o_nsa, o_mla)]
    return _out_proj(parts, p["group_norm"].reshape(1, -1), w_o_all, layer, x2, mod, seq)


def kernel(x, c, ada_w, ada_b, norm_mix, norm_ffn, w_in, conv_dw_w, conv_dw_b, conv_ln_g, conv_ln_b, conv_pw_w, conv_pw_b, nsa_q_norm, nsa_k_norm, nsa_cmp_pe, nsa_cmp_w, mla_q_lat_norm, mla_kv_lat_norm, mla_w_uq, mla_w_ukv, mla_q_norm, mla_k_norm, group_norm, w_o, ffn_up, ffn_conv_w, ffn_conv_b, ffn_down):
    batch, seq, d = x.shape
    depth = ada_w.shape[0]
    per_layer = dict(
        norm_mix=norm_mix, conv_dw_w=conv_dw_w, conv_dw_b=conv_dw_b, conv_ln_g=conv_ln_g,
        conv_ln_b=conv_ln_b, conv_pw_w=conv_pw_w, conv_pw_b=conv_pw_b, nsa_q_norm=nsa_q_norm,
        nsa_k_norm=nsa_k_norm, nsa_cmp_pe=nsa_cmp_pe, nsa_cmp_w=nsa_cmp_w, mla_q_lat_norm=mla_q_lat_norm,
        mla_kv_lat_norm=mla_kv_lat_norm, mla_w_uq=mla_w_uq, mla_w_ukv=mla_w_ukv, mla_q_norm=mla_q_norm,
        mla_k_norm=mla_k_norm, group_norm=group_norm)
    w_in_all = _reorder_w_in(w_in)
    w_o_all = w_o.astype(BF16)
    ffn_up_all = ffn_up.astype(BF16)
    ffn_down_all = ffn_down.astype(BF16)
    mods = _ada(c, ada_w, ada_b)
    x2 = x.reshape(batch * seq, d)
    for l in range(depth):
        p = {k: v[l] for k, v in per_layer.items()}
        x2 = _mixer(x2, mods[l], batch, seq, p, w_in_all, w_o_all, l)
        x2 = _ffn(x2, mods[l], norm_ffn[l].reshape(1, -1), ffn_up_all, ffn_conv_w[l], ffn_conv_b[l],
                  ffn_down_all, l, seq)
    return x2.reshape(batch, seq, d)
```

```python
import functools

import numpy as np
import jax
import jax.numpy as jnp
from jax import lax
from jax.experimental import pallas as pl
from jax.experimental.pallas import tpu as pltpu

F32 = jnp.float32
BF16 = jnp.bfloat16

LANES = 128
VMEM_LIMIT = 56 * 1024 * 1024

HEAD_DIM = 64
HEAD_SHIFT = 6
ROPE_THETA = 10000.0
EPS = 1e-6
GROUP_WIDTH = 512
CONV_WIDTH = 31
CONV_LN_EPS = 1e-5
NSA_CMP_LEN = 32
NSA_CMP_STRIDE = 16
NSA_SEL_LEN = 64
SEL_SHIFT = 6
NSA_N_SEL = 16
NSA_WINDOW = 512
NSA_FORCE_BONUS = 1e3
MLA_Q_RANK = 384
MLA_KV_RANK = 256
MLA_NOPE = 64
MLA_ROPE = 32
MLA_QK = MLA_NOPE + MLA_ROPE
MLA_HEADS = 8
HEADS_PER_STEP = 4

U_SBQ, U_SBK, U_SBV = 0, 512, 1024
U_SB_COLS = 1536
U_CA, U_CG = 0, 512
U_NQ = 1024
U_NKV = 1536
U_MLA = 2304
U_COLS = 4608
REF_ALIGNED_COLS = 3840

TQ = 256
TK = 256
NEG = -1e30
LOG2E = 1.4426950408889634


def _cparams(*sem):
    return pltpu.CompilerParams(dimension_semantics=sem, vmem_limit_bytes=VMEM_LIMIT)


def _dot(a, b):
    return jnp.dot(a, b, preferred_element_type=F32)


def _dot_nt(a, b):
    return lax.dot_general(a, b, (((1,), (1,)), ((), ())), preferred_element_type=F32)


def _split_dot(x, w):
    hi = x.astype(BF16)
    lo = (x - hi.astype(F32)).astype(BF16)
    return _dot(hi, w) + _dot(lo, w)


def _lane_iota(n=LANES):
    return lax.broadcasted_iota(jnp.int32, (1, n), 1)


def _ada_kernel(c_ref, w_ref, b_ref, o_ref):
    c = c_ref[...]
    cond = (c * jax.nn.sigmoid(c)).astype(BF16)
    o_ref[0] = _dot(cond, w_ref[0].astype(BF16)) + b_ref[0]


def _ada(c, ada_w, ada_b):
    n_layers, d, n = ada_w.shape
    b = c.shape[0]
    rows = 8
    tn = 1024
    cpad = jnp.pad(c, ((0, rows - b), (0, 0)))
    out = pl.pallas_call(
        _ada_kernel,
        grid=(n_layers, n // tn),
        in_specs=[
            pl.BlockSpec((rows, d), lambda l, j: (0, 0)),
            pl.BlockSpec((1, d, tn), lambda l, j: (l, 0, j)),
            pl.BlockSpec((1, 1, tn), lambda l, j: (l, 0, j)),
        ],
        out_specs=pl.BlockSpec((1, rows, tn), lambda l, j: (l, 0, j)),
        out_shape=jax.ShapeDtypeStruct((n_layers, rows, n), F32),
        compiler_params=_cparams("parallel", "parallel"),
        name="ada_mod",
    )(cpad, ada_w, ada_b.reshape(n_layers, 1, n))
    return out[:, :b].reshape(n_layers, b, 6, d)


def _modulated_norm(x, g, shift, scale):
    ms = jnp.mean(x * x, axis=-1, keepdims=True)
    return x * lax.rsqrt(ms + EPS) * g * (1.0 + scale) + shift


def _in_kernel(x_ref, mod_ref, g_ref, w_ref, sb_ref, o_ref, h_ref, *, n_sb):
    j = pl.program_id(1)

    @pl.when(j == 0)
    def _():
        m = mod_ref[0]
        h_ref[...] = _modulated_norm(x_ref[...], g_ref[...], m[0:1], m[1:2]).astype(BF16)

    @pl.when(j < n_sb)
    def _():
        sb_ref[...] = _dot(h_ref[...], w_ref[...]).astype(BF16)

    @pl.when(j >= n_sb)
    def _():
        o_ref[...] = _dot(h_ref[...], w_ref[...])


def _in_proj(x2, mod, g, w, layer, seq):
    m_rows, d = x2.shape
    n = w.shape[2]
    tm, tn = 1024, 768
    per_seq = seq // tm
    n_sb = U_SB_COLS // tn
    return pl.pallas_call(
        functools.partial(_in_kernel, n_sb=n_sb),
        grid=(m_rows // tm, n // tn),
        in_specs=[
            pl.BlockSpec((tm, d), lambda i, j: (i, 0)),
            pl.BlockSpec((1, 6, d), lambda i, j: (i // per_seq, 0, 0)),
            pl.BlockSpec((1, d), lambda i, j: (0, 0)),
            pl.BlockSpec((None, d, tn), lambda i, j: (layer, 0, j)),
        ],
        out_specs=[
            pl.BlockSpec((tm, tn), lambda i, j: (i, jnp.minimum(j, n_sb - 1))),
            pl.BlockSpec((tm, tn), lambda i, j: (i, jnp.maximum(j - n_sb, 0))),
        ],
        out_shape=[jax.ShapeDtypeStruct((m_rows, U_SB_COLS), BF16),
                   jax.ShapeDtypeStruct((m_rows, n - U_SB_COLS), F32)],
        scratch_shapes=[pltpu.VMEM((tm, d), BF16)],
        compiler_params=_cparams("parallel", "arbitrary"),
        name="in_proj",
    )(x2, mod, g, w)


def _diag_masks():
    r = lax.broadcasted_iota(jnp.int32, (TQ, TK), 0)
    c = lax.broadcasted_iota(jnp.int32, (TQ, TK), 1)
    return c <= r, c < r


def _sb_kernel(q_ref, k_ref, v_ref, o_ref, z_ref, a_ref, *, nq):
    i = pl.program_id(2)
    lane = _lane_iota()
    heads = range(HEADS_PER_STEP)
    pair = lambda hd: slice((hd // 2) * LANES, (hd // 2 + 1) * LANES)
    qs = []
    for pr in range(HEADS_PER_STEP // 2):
        q2 = q_ref[0, :, pr * LANES:(pr + 1) * LANES] * (HEAD_DIM ** -0.5)
        qs.append(jnp.where(lane < HEAD_DIM, q2, jnp.zeros_like(q2)))
        qs.append(jnp.where(lane >= HEAD_DIM, q2, jnp.zeros_like(q2)))
    tri = (lax.broadcasted_iota(jnp.int32, (TK, TK), 0)
           > lax.broadcasted_iota(jnp.int32, (TK, TK), 1)).astype(BF16)
    tri2 = jnp.concatenate([tri, tri], axis=0)
    _, strict = _diag_masks()

    for c in range(nq):
        @pl.when(i == c)
        def _():
            kv_len = (c + 1) * TK
            for hd in heads:
                z_ref[hd, :, :kv_len] = _dot_nt(qs[hd], k_ref[0, :kv_len, pair(hd)])
            carry = [jnp.zeros((TQ, 1), F32) for _ in heads]
            for j in reversed(range(c + 1)):
                sl = slice(j * TK, (j + 1) * TK)
                for hd in heads:
                    z = z_ref[hd, :, sl]
                    sp = jnp.log(1.0 + jnp.exp(-jnp.abs(z)))
                    log_beta = jnp.minimum(z, 0.0) - sp
                    neg_1m = jnp.maximum(z, 0.0) + sp
                    if j == c:
                        neg_1m = jnp.where(strict, neg_1m, 0.0)
                    hi = neg_1m.astype(BF16)
                    pieces = jnp.concatenate([hi, (neg_1m - hi.astype(F32)).astype(BF16)], axis=1)
                    later = _dot(pieces, tri2)
                    a = jnp.exp(log_beta - later - carry[hd])
                    if j == c:
                        a = jnp.where(strict, a, 0.0)
                    a_ref[hd, :, sl] = a.astype(BF16)
                    carry[hd] = carry[hd] + jnp.sum(neg_1m, axis=-1, keepdims=True)
            outs = [_dot(a_ref[hd, :, :kv_len], v_ref[0, :kv_len, pair(hd)]) for hd in heads]
            for pr in range(HEADS_PER_STEP // 2):
                o_ref[0, :, pr * LANES:(pr + 1) * LANES] = jnp.where(lane < HEAD_DIM, outs[2 * pr], outs[2 * pr + 1])


def _sb_attention(usb3):
    b, t, _ = usb3.shape
    nq = t // TQ
    wide = HEADS_PER_STEP * HEAD_DIM
    return pl.pallas_call(
        functools.partial(_sb_kernel, nq=nq),
        grid=(b, GROUP_WIDTH // wide, nq),
        in_specs=[
            pl.BlockSpec((1, TQ, wide), lambda bb, p, i: (bb, i, U_SBQ // wide + p)),
            pl.BlockSpec((1, t, wide), lambda bb, p, i: (bb, 0, U_SBK // wide + p)),
            pl.BlockSpec((1, t, wide), lambda bb, p, i: (bb, 0, U_SBV // wide + p)),
        ],
        out_specs=pl.BlockSpec((1, TQ, wide), lambda bb, p, i: (bb, i, p)),
        out_shape=jax.ShapeDtypeStruct((b, t, GROUP_WIDTH), F32),
        scratch_shapes=[pltpu.VMEM((HEADS_PER_STEP, TQ, t), F32), pltpu.VMEM((HEADS_PER_STEP, TQ, t), BF16)],
        compiler_params=_cparams("parallel", "parallel", "arbitrary"),
        name="sb_attention",
    )(usb3, usb3, usb3)


CONV_HALO = 32


def _conv_kernel(a_ref, g_ref, ah_ref, gh_ref, dww_ref, dwb_ref, lng_ref, lnb_ref, pw_ref, pwb_ref,
                 o_ref, h_ref, s_ref, *, tm):
    i = pl.program_id(1)
    h_ref[CONV_HALO:, :] = a_ref[0] * jax.nn.sigmoid(g_ref[0])
    halo = ah_ref[0] * jax.nn.sigmoid(gh_ref[0])
    h_ref[0:CONV_HALO, :] = jnp.where(i == 0, 0.0, halo)
    acc = jnp.zeros((tm, GROUP_WIDTH), F32) + dwb_ref[...]
    base = CONV_HALO - (CONV_WIDTH - 1)
    sub = 8
    for r in range(sub):
        taps = [k for k in range(CONV_WIDTH) if (base + k) % sub == r]
        if not taps:
            continue
        lo = base + taps[0]
        span = (taps[-1] - taps[0]) + tm
        s_ref[r, 0:span, :] = h_ref[lo:lo + span, :]
        for k in taps:
            acc = acc + dww_ref[k:k + 1, :] * s_ref[r, k - taps[0]:k - taps[0] + tm, :]
    mu = jnp.mean(acc, axis=-1, keepdims=True)
    cen = acc - mu
    var = jnp.mean(cen * cen, axis=-1, keepdims=True)
    hn = cen * lax.rsqrt(var + CONV_LN_EPS) * lng_ref[...] + lnb_ref[...]
    act = hn * jax.nn.sigmoid(hn)
    o_ref[0] = _dot(act.astype(BF16), pw_ref[...]) + pwb_ref[...]


def _conformer(u3, dw_w, dw_b, ln_g, ln_b, pw_w, pw_b):
    b, t, _ = u3.shape
    tm = 512
    cw = GROUP_WIDTH
    hb = tm // CONV_HALO

    def halo_map(col):
        return lambda bb, i: (bb, jnp.maximum(i * hb - 1, 0), col)

    row = lambda bb, i: (0, 0)
    return pl.pallas_call(
        functools.partial(_conv_kernel, tm=tm),
        grid=(b, t // tm),
        in_specs=[
            pl.BlockSpec((1, tm, cw), lambda bb, i: (bb, i, U_CA // cw)),
            pl.BlockSpec((1, tm, cw), lambda bb, i: (bb, i, U_CG // cw)),
            pl.BlockSpec((1, CONV_HALO, cw), halo_map(U_CA // cw)),
            pl.BlockSpec((1, CONV_HALO, cw), halo_map(U_CG // cw)),
            pl.BlockSpec((CONV_WIDTH, cw), row),
            pl.BlockSpec((1, cw), row),
            pl.BlockSpec((1, cw), row),
            pl.BlockSpec((1, cw), row),
            pl.BlockSpec((cw, cw), row),
            pl.BlockSpec((1, cw), row),
        ],
        out_specs=pl.BlockSpec((1, tm, cw), lambda bb, i: (bb, i, 0)),
        out_shape=jax.ShapeDtypeStruct((b, t, cw), F32),
        scratch_shapes=[pltpu.VMEM((tm + CONV_HALO, cw), F32), pltpu.VMEM((8, tm + CONV_HALO, cw), F32)],
        compiler_params=_cparams("parallel", "parallel"),
        name="conformer_conv",
    )(u3, u3, u3, u3, dw_w, dw_b.reshape(1, cw), ln_g.reshape(1, cw), ln_b.reshape(1, cw),
      pw_w.astype(BF16), pw_b.reshape(1, cw))


def _head_norm_rope(x, gain, cos, sin_signed, seg_mean):
    ms = _split_dot(x * x, seg_mean)
    y = x * lax.rsqrt(ms + EPS) * gain
    lane = _lane_iota()
    half = HEAD_DIM // 2
    first = (lane & (HEAD_DIM - 1)) < half
    partner = jnp.where(first, pltpu.roll(y, LANES - half, 1), pltpu.roll(y, half, 1))
    return y * cos + partner * sin_signed


def _dup(x, g):
    lane = _lane_iota()
    sw = pltpu.roll(x, HEAD_DIM, 1)
    if g == 0:
        return jnp.where(lane < HEAD_DIM, x, sw)
    return jnp.where(lane < HEAD_DIM, sw, x)


def _nsa_prep_kernel(q_ref, kv_ref, gk_ref, cos_ref, sin_ref, qg_ref, kg_ref,
                     qn_ref, kc_ref, kse_ref, kso_ref, vs_ref, kw_ref, vw_ref, gt_ref, *, per_seq):
    cos = cos_ref[...]
    sin = sin_ref[...]
    seg = lax.broadcasted_iota(jnp.int32, (LANES, LANES), 0) >> HEAD_SHIFT
    seg_mean = jnp.where(seg == lax.broadcasted_iota(jnp.int32, (LANES, LANES), 1) >> HEAD_SHIFT,
                         1.0 / HEAD_DIM, 0.0).astype(BF16)
    scale = HEAD_DIM ** -0.5
    for p in range(GROUP_WIDTH // LANES):
        x = q_ref[:, p * LANES:(p + 1) * LANES]
        qn_ref[:, p * LANES:(p + 1) * LANES] = (
            _head_norm_rope(x, qg_ref[...], cos, sin, seg_mean) * scale).astype(BF16)

    def blk(n):
        return kv_ref[:, n * LANES:(n + 1) * LANES]

    kc_ref[...] = _head_norm_rope(blk(0), kg_ref[0:1, :], cos, sin, seg_mean)
    ks = _head_norm_rope(blk(2), kg_ref[1:2, :], cos, sin, seg_mean)
    kw = _head_norm_rope(blk(4), kg_ref[2:3, :], cos, sin, seg_mean)
    vs = blk(3)
    vw = blk(5)
    tm = ks.shape[0]
    lane = _lane_iota()
    t = (pl.program_id(0) % per_seq) * tm + lax.broadcasted_iota(jnp.int32, (tm, 1), 0)
    sblk = t >> SEL_SHIFT
    hot_lo = jnp.where(lane == sblk, 1.0, 0.0)
    hot_hi = jnp.where(lane - HEAD_DIM == sblk, 1.0, 0.0)
    ks_sw = pltpu.roll(ks, HEAD_DIM, 1)
    for g in range(2):
        sl = slice(g * LANES, (g + 1) * LANES)
        kse_ref[:, sl] = jnp.where(lane < HEAD_DIM, ks if g == 0 else ks_sw, hot_hi).astype(BF16)
        kso_ref[:, sl] = jnp.where(lane >= HEAD_DIM, ks_sw if g == 0 else ks, hot_lo).astype(BF16)
        vs_ref[:, sl] = _dup(vs, g).astype(BF16)
        kw_ref[:, sl] = _dup(kw, g).astype(BF16)
        vw_ref[:, sl] = _dup(vw, g).astype(BF16)
    gates = jax.nn.sigmoid(gk_ref[...])
    gt_ref[:, 0:LANES] = gates
    gt_ref[:, LANES:2 * LANES] = pltpu.roll(gates, LANES - 12, 1)


def _nsa_prep(u2, cos, sin, q_gain, k_gain, seq):
    m_rows = u2.shape[0]
    tm = 512
    per_seq = seq // tm
    rowmap = lambda i: (i, 0)
    tab = lambda i: (i % per_seq, 0)
    const = lambda i: (0, 0)
    outs = [
        jax.ShapeDtypeStruct((m_rows, GROUP_WIDTH), BF16),
        jax.ShapeDtypeStruct((m_rows, LANES), F32),
        jax.ShapeDtypeStruct((m_rows, 2 * LANES), BF16),
        jax.ShapeDtypeStruct((m_rows, 2 * LANES), BF16),
        jax.ShapeDtypeStruct((m_rows, 2 * LANES), BF16),
        jax.ShapeDtypeStruct((m_rows, 2 * LANES), BF16),
        jax.ShapeDtypeStruct((m_rows, 2 * LANES), BF16),
        jax.ShapeDtypeStruct((m_rows, 2 * LANES), F32),
    ]
    return pl.pallas_call(
        functools.partial(_nsa_prep_kernel, per_seq=per_seq),
        grid=(m_rows // tm,),
        in_specs=[
            pl.BlockSpec((tm, GROUP_WIDTH), lambda i: (i, U_NQ // GROUP_WIDTH)),
            pl.BlockSpec((tm, 6 * LANES), lambda i: (i, U_NKV // (6 * LANES))),
            pl.BlockSpec((tm, LANES), lambda i: (i, (U_MLA + MLA_Q_RANK + MLA_KV_RANK) // LANES)),
            pl.BlockSpec((tm, LANES), tab),
            pl.BlockSpec((tm, LANES), tab),
            pl.BlockSpec((1, LANES), const),
            pl.BlockSpec((3, LANES), const),
        ],
        out_specs=[
            pl.BlockSpec((tm, GROUP_WIDTH), rowmap),
            pl.BlockSpec((tm, LANES), rowmap),
            pl.BlockSpec((tm, 2 * LANES), rowmap),
            pl.BlockSpec((tm, 2 * LANES), rowmap),
            pl.BlockSpec((tm, 2 * LANES), rowmap),
            pl.BlockSpec((tm, 2 * LANES), rowmap),
            pl.BlockSpec((tm, 2 * LANES), rowmap),
            pl.BlockSpec((tm, 2 * LANES), rowmap),
        ],
        out_shape=outs,
        compiler_params=_cparams("parallel"),
        name="nsa_prep",
    )(u2, u2, u2, cos, sin, q_gain, k_gain)


def _compress_kernel(xk_ref, xv_ref, pek_ref, pev_ref, wk_ref, wv_ref, kc_ref, vc_ref, *, nrow):
    stride = NSA_CMP_STRIDE

    def comp(x_ref, pe_ref, w_ref):
        first = jnp.zeros((nrow, LANES), F32)
        second = jnp.zeros((nrow, LANES), F32)
        for r in range(stride):
            x = x_ref[0, pl.ds(r, nrow, stride=stride), :]
            first = first + _dot((x + pe_ref[r:r + 1, :]).astype(BF16), w_ref[r])
            second = second + _dot((x + pe_ref[stride + r:stride + r + 1, :]).astype(BF16), w_ref[stride + r])
        return first + pltpu.roll(second, nrow - 1, 0)

    ck = comp(xk_ref, pek_ref, wk_ref)
    cv = comp(xv_ref, pev_ref, wv_ref)
    for g in range(2):
        kc_ref[0, g] = _dup(ck, g).astype(BF16)
        vc_ref[0, g] = _dup(cv, g).astype(BF16)


def _compress(kc3, u3, pe_k, pe_v, wk, wv):
    b, t, _ = kc3.shape
    nrow = t // NSA_CMP_STRIDE
    const2 = lambda bb: (0, 0)
    const3 = lambda bb: (0, 0, 0)
    out = jax.ShapeDtypeStruct((b, 2, nrow, LANES), BF16)
    return pl.pallas_call(
        functools.partial(_compress_kernel, nrow=nrow),
        grid=(b,),
        in_specs=[
            pl.BlockSpec((1, t, LANES), lambda bb: (bb, 0, 0)),
            pl.BlockSpec((1, t, LANES), lambda bb: (bb, 0, U_NKV // LANES + 1)),
            pl.BlockSpec(pe_k.shape, const2),
            pl.BlockSpec(pe_v.shape, const2),
            pl.BlockSpec(wk.shape, const3),
            pl.BlockSpec(wv.shape, const3),
        ],
        out_specs=[pl.BlockSpec((1, 2, nrow, LANES), lambda bb: (bb, 0, 0, 0))] * 2,
        out_shape=[out, out],
        compiler_params=_cparams("parallel"),
        name="nsa_compress",
    )(kc3, u3, pe_k, pe_v, wk, wv)


def _attend_many(jobs, scale=None):
    tiles = [k.shape[0] // TK for (_, k, _, _, _, _) in jobs]
    for (q, k, _, s_ref, _, _), n in zip(jobs, tiles):
        s_ref[:, :n * TK] = _dot_nt(q, k)
    maxima = []
    for (_, _, _, s_ref, _, masks), n in zip(jobs, tiles):
        fold = None
        for j in range(n):
            sl = slice(j * TK, (j + 1) * TK)
            s = s_ref[:, sl]
            if j in masks:
                s = jnp.where(masks[j], s, NEG)
                s_ref[:, sl] = s
            f = jnp.maximum(s[:, :LANES], s[:, LANES:])
            fold = f if fold is None else jnp.maximum(fold, f)
        maxima.append(jnp.max(fold, axis=-1, keepdims=True))
    sums = []
    for (_, _, _, s_ref, p_ref, _), n, m in zip(jobs, tiles, maxima):
        fold = None
        for j in range(n):
            sl = slice(j * TK, (j + 1) * TK)
            p = jnp.exp2((s_ref[:, sl] - m) * (LOG2E if scale is None else scale * LOG2E))
            f = p[:, :LANES] + p[:, LANES:]
            fold = f if fold is None else fold + f
            p_ref[:, sl] = p.astype(BF16)
        sums.append(jnp.sum(fold, axis=-1, keepdims=True))
    return [_dot(p_ref[:, :n * TK], v) / l for (_, _, v, _, p_ref, _), n, l in zip(jobs, tiles, sums)]


def _nsa_kernel(q_ref, kc_ref, vc_ref, kse_ref, kso_ref, vs_ref, kw_ref, vw_ref, gt_ref, ov_ref,
                o_ref, s_ref, p_ref, sw_ref, pw_ref, *, nq, n_cmp, n_sblk, n_sel):
    i = pl.program_id(2)
    lane = _lane_iota()
    row = i * TQ + lax.broadcasted_iota(jnp.int32, (TQ, 1), 0)
    rep = 4
    q2s = [q_ref[0, :, p * LANES:(p + 1) * LANES] for p in range(2)]
    qs = []
    for q2 in q2s:
        qs.append(jnp.where(lane < HEAD_DIM, q2, jnp.zeros_like(q2)))
        qs.append(jnp.where(lane >= HEAD_DIM, q2, jnp.zeros_like(q2)))

    kc = kc_ref[0, 0]
    vc = vc_ref[0, 0]
    cmask = ((lane * NSA_CMP_STRIDE + (NSA_CMP_LEN - 1)) <= row) & (lane < n_cmp)
    psum = jnp.zeros((TQ, LANES), F32)
    s_cmp = [jnp.where(cmask, _dot_nt(qs[r], kc), NEG) for r in range(rep)]
    p_cmp = []
    for r in range(rep):
        mx = jnp.max(s_cmp[r], axis=-1, keepdims=True)
        p = jnp.where(cmask, jnp.exp(s_cmp[r] - mx), 0.0)
        p = p / jnp.maximum(jnp.sum(p, axis=-1, keepdims=True), 1e-30)
        p_cmp.append(p.astype(BF16))
        psum = psum + p
    o_cmp = [_dot(p_cmp[r], vc) for r in range(rep)]
    ov_t = ov_ref[0:n_sblk, :]
    p_hi = psum.astype(BF16)
    p_lo = (psum - p_hi.astype(F32)).astype(BF16)
    imp = _dot_nt(ov_t, p_hi) + _dot_nt(ov_t, p_lo)
    cur = (i * TQ + _lane_iota(TQ)) >> SEL_SHIFT
    blk = lax.broadcasted_iota(jnp.int32, (n_sblk, 1), 0)
    eligible = blk <= cur
    forced = (blk == 0) | (blk == cur) | (blk == cur - 1)
    score = jnp.where(eligible, imp + jnp.where(forced, NSA_FORCE_BONUS, 0.0), NEG)
    rank = jnp.zeros((n_sblk, TQ), F32)
    for sp in range(n_sblk):
        other = score[sp:sp + 1, :]
        beats = (other > score) | ((other == score) & (blk > sp))
        rank = rank + jnp.where(beats, 1.0, 0.0)
    bias_t = jnp.where(eligible & (rank < n_sel), 0.0, NEG)

    bias_lo = jnp.concatenate([bias_t, jnp.zeros((LANES - n_sblk, TQ), F32)], axis=0).T
    bias_hi = pltpu.roll(bias_lo, HEAD_DIM, 1).astype(BF16)
    bias_lo = bias_lo.astype(BF16)
    q_aug = []
    for q2 in q2s:
        q_aug.append(jnp.where(lane < HEAD_DIM, q2, bias_hi))
        q_aug.append(jnp.where(lane >= HEAD_DIM, q2, bias_lo))
    causal, strict = _diag_masks()
    wtiles = NSA_WINDOW // TK
    gt = gt_ref[0]

    for c in range(nq):
        @pl.when(i == c)
        def _():
            kv_len = (c + 1) * TK
            lo = max(c - wtiles, 0) * TK
            wmasks = {c - max(c - wtiles, 0): causal}
            if c >= wtiles:
                wmasks[0] = jnp.logical_not(causal)
            jobs = []
            for r in range(rep):
                ks_ref = kse_ref if r % 2 == 0 else kso_ref
                jobs.append((q_aug[r], ks_ref[0, :kv_len, :], vs_ref[0, :kv_len, :],
                             s_ref.at[r], p_ref.at[r], {c: causal}))
            for r in range(rep):
                jobs.append((qs[r], kw_ref[0, lo:kv_len, :], vw_ref[0, lo:kv_len, :],
                             sw_ref.at[r], pw_ref.at[r], wmasks))
            res = _attend_many(jobs)
            outs = []
            for r in range(rep):
                outs.append(gt[:, 3 * r:3 * r + 1] * o_cmp[r] + gt[:, 3 * r + 1:3 * r + 2] * res[r]
                            + gt[:, 3 * r + 2:3 * r + 3] * res[rep + r])
            o_ref[0, :, 0:LANES] = jnp.where(lane < HEAD_DIM, outs[0], outs[1])
            o_ref[0, :, LANES:2 * LANES] = jnp.where(lane < HEAD_DIM, outs[2], outs[3])


def _nsa_attention(qn, kcmp, vcmp, kse, kso, vsd, kwd, vwd, gates, overlap):
    b, t, _ = qn.shape
    nq = t // TQ
    n_cmp = (t - NSA_CMP_LEN) // NSA_CMP_STRIDE + 1
    n_sblk = t // NSA_SEL_LEN
    nrow = kcmp.shape[2]
    wlen = NSA_WINDOW + TK
    kv = lambda bb, g, i: (bb, 0, g)
    cm = lambda bb, g, i: (bb, g, 0, 0)
    return pl.pallas_call(
        functools.partial(_nsa_kernel, nq=nq, n_cmp=n_cmp, n_sblk=n_sblk, n_sel=min(NSA_N_SEL, n_sblk)),
        grid=(b, 2, nq),
        in_specs=[
            pl.BlockSpec((1, TQ, 2 * LANES), lambda bb, g, i: (bb, i, g)),
            pl.BlockSpec((1, 1, nrow, LANES), cm),
            pl.BlockSpec((1, 1, nrow, LANES), cm),
            pl.BlockSpec((1, t, LANES), kv),
            pl.BlockSpec((1, t, LANES), kv),
            pl.BlockSpec((1, t, LANES), kv),
            pl.BlockSpec((1, t, LANES), kv),
            pl.BlockSpec((1, t, LANES), kv),
            pl.BlockSpec((1, TQ, LANES), lambda bb, g, i: (bb, i, g)),
            pl.BlockSpec((LANES, LANES), lambda bb, g, i: (0, 0)),
        ],
        out_specs=pl.BlockSpec((1, TQ, 2 * LANES), lambda bb, g, i: (bb, i, g)),
        out_shape=jax.ShapeDtypeStruct((b, t, GROUP_WIDTH), F32),
        scratch_shapes=[pltpu.VMEM((4, TQ, t), F32), pltpu.VMEM((4, TQ, t), BF16),
                        pltpu.VMEM((4, TQ, wlen), F32), pltpu.VMEM((4, TQ, wlen), BF16)],
        compiler_params=_cparams("parallel", "parallel", "arbitrary"),
        name="nsa_attention",
    )(qn, kcmp, vcmp, kse, kso, vsd, kwd, vwd, gates, overlap)


def _mla_prep_kernel(u_ref, cos_ref, sin_ref, qlg_ref, kvlg_ref, wq_ref, wk_ref, wv_ref, qg_ref, kg_ref,
                     q_ref, k_ref, v_ref):
    lane = _lane_iota()
    cos = cos_ref[...]
    sin = sin_ref[...]
    half = MLA_ROPE // 2

    def lat_norm(x, g):
        ms = jnp.mean(x * x, axis=-1, keepdims=True)
        return (x * lax.rsqrt(ms + EPS) * g).astype(BF16)

    live = lane < MLA_QK

    def head_norm_rope(x, g):
        ms = jnp.sum(jnp.where(live, x * x, 0.0), axis=-1, keepdims=True) * (1.0 / MLA_QK)
        y = x * lax.rsqrt(ms + EPS) * g
        return y * cos + pltpu.roll(y, LANES - half, 1) * sin

    q = _dot(lat_norm(u_ref[:, 0:MLA_Q_RANK], qlg_ref[...]), wq_ref[...])
    kvn = lat_norm(u_ref[:, MLA_Q_RANK:MLA_Q_RANK + MLA_KV_RANK], kvlg_ref[...])
    kn = _dot(kvn, wk_ref[...])
    v_ref[...] = _dot(kvn, wv_ref[...]).astype(BF16)
    gk = u_ref[:, MLA_Q_RANK + MLA_KV_RANK:MLA_Q_RANK + MLA_KV_RANK + LANES]
    k_rope = jnp.where(lane >= MLA_NOPE, gk, 0.0)
    for h in range(MLA_HEADS):
        sl = slice(h * LANES, (h + 1) * LANES)
        q_ref[:, sl] = head_norm_rope(q[:, sl], qg_ref[...]).astype(BF16)
        k_ref[:, sl] = head_norm_rope(kn[:, sl] + k_rope, kg_ref[...]).astype(BF16)


def _mla_prep(u2, cos, sin, q_lat_gain, kv_lat_gain, wq, wk, wv, q_gain, k_gain, seq):
    m_rows = u2.shape[0]
    tm = 512
    per_seq = seq // tm
    width = MLA_Q_RANK + MLA_KV_RANK + LANES
    rowmap = lambda i: (i, 0)
    tab = lambda i: (i % per_seq, 0)
    const = lambda i: (0, 0)
    return pl.pallas_call(
        _mla_prep_kernel,
        grid=(m_rows // tm,),
        in_specs=[
            pl.BlockSpec((tm, width), lambda i: (i, U_MLA // width)),
            pl.BlockSpec((tm, LANES), tab),
            pl.BlockSpec((tm, LANES), tab),
            pl.BlockSpec((1, MLA_Q_RANK), const),
            pl.BlockSpec((1, MLA_KV_RANK), const),
            pl.BlockSpec(wq.shape, const),
            pl.BlockSpec(wk.shape, const),
            pl.BlockSpec(wv.shape, const),
            pl.BlockSpec((1, LANES), const),
            pl.BlockSpec((1, LANES), const),
        ],
        out_specs=[
            pl.BlockSpec((tm, MLA_HEADS * LANES), rowmap),
            pl.BlockSpec((tm, MLA_HEADS * LANES), rowmap),
            pl.BlockSpec((tm, GROUP_WIDTH), rowmap),
        ],
        out_shape=[
            jax.ShapeDtypeStruct((m_rows, MLA_HEADS * LANES), BF16),
            jax.ShapeDtypeStruct((m_rows, MLA_HEADS * LANES), BF16),
            jax.ShapeDtypeStruct((m_rows, GROUP_WIDTH), BF16),
        ],
        compiler_params=_cparams("parallel"),
        name="mla_prep",
    )(u2, cos, sin, q_lat_gain, kv_lat_gain, wq, wk, wv, q_gain, k_gain)


def _mla_kernel(q_ref, k_ref, v_ref, o_ref, s_ref, p_ref, *, nq):
    i = pl.program_id(2)
    lane = _lane_iota()
    causal, _ = _diag_masks()
    scale = MLA_QK ** -0.5
    for c in range(nq):
        @pl.when(i == c)
        def _():
            kv_len = (c + 1) * TK
            jobs = []
            for hd in range(HEADS_PER_STEP):
                sl = slice(hd * LANES, (hd + 1) * LANES)
                vs = slice((hd // 2) * LANES, (hd // 2 + 1) * LANES)
                jobs.append((q_ref[0, :, sl], k_ref[0, :kv_len, sl], v_ref[0, :kv_len, vs],
                             s_ref.at[hd], p_ref.at[hd], {c: causal}))
            outs = _attend_many(jobs, scale=scale)
            for pr in range(HEADS_PER_STEP // 2):
                o_ref[0, :, pr * LANES:(pr + 1) * LANES] = jnp.where(lane < HEAD_DIM, outs[2 * pr], outs[2 * pr + 1])


def _mla_attention(qm, km, vm):
    b, t, _ = qm.shape
    nq = t // TQ
    return pl.pallas_call(
        functools.partial(_mla_kernel, nq=nq),
        grid=(b, MLA_HEADS // HEADS_PER_STEP, nq),
        in_specs=[
            pl.BlockSpec((1, TQ, HEADS_PER_STEP * LANES), lambda bb, p, i: (bb, i, p)),
            pl.BlockSpec((1, t, HEADS_PER_STEP * LANES), lambda bb, p, i: (bb, 0, p)),
            pl.BlockSpec((1, t, HEADS_PER_STEP * HEAD_DIM), lambda bb, p, i: (bb, 0, p)),
        ],
        out_specs=pl.BlockSpec((1, TQ, HEADS_PER_STEP * HEAD_DIM), lambda bb, p, i: (bb, i, p)),
        out_shape=jax.ShapeDtypeStruct((b, t, GROUP_WIDTH), F32),
        scratch_shapes=[pltpu.VMEM((HEADS_PER_STEP, TQ, t), F32), pltpu.VMEM((HEADS_PER_STEP, TQ, t), BF16)],
        compiler_params=_cparams("parallel", "parallel", "arbitrary"),
        name="mla_attention",
    )(qm, km, vm)


def _out_kernel(a_ref, b_ref, c_ref, d_ref, gn_ref, w_ref, x_ref, mod_ref, o_ref):
    acc = None
    for gi, r in enumerate((a_ref, b_ref, c_ref, d_ref)):
        y = r[...]
        ms = jnp.mean(y * y, axis=-1, keepdims=True)
        sl = slice(gi * GROUP_WIDTH, (gi + 1) * GROUP_WIDTH)
        part = _dot((y * lax.rsqrt(ms + EPS) * gn_ref[:, sl]).astype(BF16), w_ref[sl, :])
        acc = part if acc is None else acc + part
    o_ref[...] = x_ref[...] + mod_ref[0, 2:3, :] * acc


def _out_proj(parts, gn, w, layer, x2, mod, seq):
    m_rows, d = x2.shape
    tm = 512
    per_seq = seq // tm
    part = pl.BlockSpec((tm, GROUP_WIDTH), lambda i: (i, 0))
    return pl.pallas_call(
        _out_kernel,
        grid=(m_rows // tm,),
        in_specs=[
            part, part, part, part,
            pl.BlockSpec((1, 4 * GROUP_WIDTH), lambda i: (0, 0)),
            pl.BlockSpec((None, 4 * GROUP_WIDTH, d), lambda i: (layer, 0, 0)),
            pl.BlockSpec((tm, d), lambda i: (i, 0)),
            pl.BlockSpec((1, 6, d), lambda i: (i // per_seq, 0, 0)),
        ],
        out_specs=pl.BlockSpec((tm, d), lambda i: (i, 0)),
        out_shape=jax.ShapeDtypeStruct((m_rows, d), F32),
        compiler_params=_cparams("parallel"),
        name="out_proj",
    )(*parts, gn, w, x2, mod)


FFN_HALO = 16


def _ffn_kernel(x_ref, xh_ref, mod_ref, g_ref, wa_ref, wg_ref, cwa_ref, cwg_ref, cba_ref, cbg_ref, wd_ref, wdl_ref,
                o_ref, h_ref, act_ref, acc_ref, *, per_seq, nj):
    i = pl.program_id(0)
    j = pl.program_id(1)
    m = mod_ref[0]

    def conv(u, cw_ref, cb_ref):
        y = cw_ref[2:3, :] * u + cw_ref[1:2, :] * pltpu.roll(u, 1, 0) + cw_ref[0:1, :] * pltpu.roll(u, 2, 0)
        return y[FFN_HALO:, :] + cb_ref[...]

    def activation():
        h = h_ref[...]
        ya = conv(_dot(h, wa_ref[...]), cwa_ref, cba_ref)
        yg = conv(_dot(h, wg_ref[...]), cwg_ref, cbg_ref)
        return ((yg * jax.nn.sigmoid(yg)) * ya).astype(BF16)

    @pl.when(j == 0)
    def _():
        h_ref[FFN_HALO:, :] = _modulated_norm(x_ref[...], g_ref[...], m[3:4], m[4:5]).astype(BF16)
        halo = _modulated_norm(xh_ref[...], g_ref[...], m[3:4], m[4:5])
        h_ref[0:FFN_HALO, :] = jnp.where(i % per_seq == 0, 0.0, halo).astype(BF16)
        act_ref[...] = activation()

    @pl.when(j == 1)
    def _():
        new = activation()
        acc_ref[...] = _dot(act_ref[...], wd_ref[...])
        act_ref[...] = new

    @pl.when((j > 1) & (j < nj - 1))
    def _():
        new = activation()
        acc_ref[...] += _dot(act_ref[...], wd_ref[...])
        act_ref[...] = new

    @pl.when(j == nj - 1)
    def _():
        new = activation()
        total = acc_ref[...] + _dot(act_ref[...], wd_ref[...]) + _dot(new, wdl_ref[...])
        o_ref[...] = x_ref[...] + m[5:6] * total


def _ffn(x2, mod, g, w_up, conv_w, conv_b, w_down, layer, seq):
    m_rows, d = x2.shape
    d_ff = w_down.shape[1]
    tm, tn = 512, 512
    per_seq = seq // tm
    nj = d_ff // tn
    hb = tm // FFN_HALO
    cb = conv_b.reshape(1, 2 * d_ff)
    assert nj >= 3
    return pl.pallas_call(
        functools.partial(_ffn_kernel, per_seq=per_seq, nj=nj),
        grid=(m_rows // tm, nj),
        in_specs=[
            pl.BlockSpec((tm, d), lambda i, j: (i, 0)),
            pl.BlockSpec((FFN_HALO, d), lambda i, j: (jnp.maximum(i * hb - 1, 0), 0)),
            pl.BlockSpec((1, 6, d), lambda i, j: (i // per_seq, 0, 0)),
            pl.BlockSpec((1, d), lambda i, j: (0, 0)),
            pl.BlockSpec((None, d, tn), lambda i, j: (layer, 0, j)),
            pl.BlockSpec((None, d, tn), lambda i, j: (layer, 0, nj + j)),
            pl.BlockSpec((3, tn), lambda i, j: (0, j)),
            pl.BlockSpec((3, tn), lambda i, j: (0, nj + j)),
            pl.BlockSpec((1, tn), lambda i, j: (0, j)),
            pl.BlockSpec((1, tn), lambda i, j: (0, nj + j)),
            pl.BlockSpec((None, tn, d), lambda i, j: (layer, jnp.maximum(j - 1, 0), 0)),
            pl.BlockSpec((None, tn, d), lambda i, j: (layer, nj - 1, 0)),
        ],
        out_specs=pl.BlockSpec((tm, d), lambda i, j: (i, 0)),
        out_shape=jax.ShapeDtypeStruct((m_rows, d), F32),
        scratch_shapes=[pltpu.VMEM((tm + FFN_HALO, d), BF16), pltpu.VMEM((tm, tn), BF16),
                        pltpu.VMEM((tm, d), F32)],
        compiler_params=_cparams("parallel", "arbitrary"),
        name="conv_glu_ffn",
    )(x2, x2, mod, g, w_up, w_up, conv_w, conv_w, cb, cb, w_down, w_down)


def _tail_placement(n_in):
    n_gate, lat, half = 24, MLA_Q_RANK + MLA_KV_RANK, MLA_ROPE // 2
    width = U_COLS - REF_ALIGNED_COLS
    place = np.zeros((width, width), np.float32)
    src_lat, src_rope = n_gate, n_gate + lat
    for c in range(lat):
        place[src_lat + c, c] = 1.0
    for c in range(n_gate):
        place[c, lat + c] = 1.0
    for c in range(MLA_ROPE):
        place[src_rope + c, lat + MLA_NOPE + c] = 1.0
    for c in range(half):
        place[src_rope + c, lat + MLA_QK + c] = 1.0
    assert src_rope + MLA_ROPE == n_in - REF_ALIGNED_COLS
    return jnp.asarray(place, BF16)


def _reorder_kernel(w_ref, place_ref, o_ref, *, n_in):
    o_ref[0, :, :REF_ALIGNED_COLS] = w_ref[0, :, :REF_ALIGNED_COLS].astype(BF16)
    tail = w_ref[0, :, REF_ALIGNED_COLS:]
    col = lax.broadcasted_iota(jnp.int32, tail.shape, 1)
    tail = jnp.where(col < n_in - REF_ALIGNED_COLS, tail, 0.0).astype(BF16)
    o_ref[0, :, REF_ALIGNED_COLS:] = _dot(tail, place_ref[...]).astype(BF16)


def _reorder_w_in(w):
    n_layers, d, n_in = w.shape
    tr = 256
    width = U_COLS - REF_ALIGNED_COLS
    return pl.pallas_call(
        functools.partial(_reorder_kernel, n_in=n_in),
        grid=(n_layers, d // tr),
        in_specs=[
            pl.BlockSpec((1, tr, U_COLS), lambda l, i: (l, i, 0)),
            pl.BlockSpec((width, width), lambda l, i: (0, 0)),
        ],
        out_specs=pl.BlockSpec((1, tr, U_COLS), lambda l, i: (l, i, 0)),
        out_shape=jax.ShapeDtypeStruct((n_layers, d, U_COLS), BF16),
        compiler_params=_cparams("parallel", "parallel"),
        name="reorder_w_in",
    )(w, _tail_placement(n_in))


def _rope_tables(seq, dim, lane_cos, lane_sin):
    inv = ROPE_THETA ** (-jnp.arange(0, dim, 2, dtype=F32) / dim)
    ang = jnp.arange(seq).astype(F32)[:, None] * inv[None, :]
    cos, sin = jnp.cos(ang), jnp.sin(ang)
    return lane_cos(cos), lane_sin(sin)


def _nsa_tables(seq):
    return _rope_tables(seq, HEAD_DIM,
                        lambda c: jnp.concatenate([c, c, c, c], axis=1),
                        lambda s: jnp.concatenate([-s, s, -s, s], axis=1))


def _mla_tables(seq):
    ones = jnp.ones((seq, MLA_NOPE), F32)
    zeros = jnp.zeros((seq, MLA_NOPE), F32)
    pad0 = jnp.zeros((seq, LANES - MLA_QK), F32)
    return _rope_tables(seq, MLA_ROPE,
                        lambda c: jnp.concatenate([ones, c, c, pad0], axis=1),
                        lambda s: jnp.concatenate([zeros, -s, s, pad0], axis=1))


def _mla_head_lanes(v):
    half = MLA_ROPE // 2
    pad = jnp.zeros(v.shape[:-1] + (LANES - MLA_QK - half,), v.dtype)
    return jnp.concatenate([v, v[..., MLA_NOPE:MLA_NOPE + half], pad], axis=-1)


def _overlap_matrix(seq):
    n_cmp = (seq - NSA_CMP_LEN) // NSA_CMP_STRIDE + 1
    n_sblk = seq // NSA_SEL_LEN
    starts = np.arange(n_cmp) * NSA_CMP_STRIDE
    sel_start = np.arange(n_sblk) * NSA_SEL_LEN
    ov = np.clip(np.minimum(starts[:, None] + NSA_CMP_LEN, sel_start[None, :] + NSA_SEL_LEN)
                 - np.maximum(starts[:, None], sel_start[None, :]), 0, None) / NSA_CMP_LEN
    full = np.zeros((LANES, LANES), np.float32)
    full[:n_sblk, :n_cmp] = ov.T
    return jnp.asarray(full, BF16)


def _mixer(x2, mod, batch, seq, p, w_in_all, w_o_all, layer):
    m_rows = x2.shape[0]
    usb, u2 = _in_proj(x2, mod, p["norm_mix"].reshape(1, -1), w_in_all, layer, seq)
    u3 = u2.reshape(batch, seq, U_COLS - U_SB_COLS)

    o_sb = _sb_attention(usb.reshape(batch, seq, U_SB_COLS))
    o_conv = _conformer(u3, p["conv_dw_w"], p["conv_dw_b"], p["conv_ln_g"], p["conv_ln_b"],
                        p["conv_pw_w"], p["conv_pw_b"])

    cos_n, sin_n = _nsa_tables(seq)
    q_gain = jnp.tile(p["nsa_q_norm"], 2).reshape(1, LANES)
    k_gain = jnp.tile(p["nsa_k_norm"], (1, 2))
    qn, kc, kse, kso, vsd, kwd, vwd, gates = _nsa_prep(u2, cos_n, sin_n, q_gain, k_gain, seq)
    pe2 = jnp.tile(p["nsa_cmp_pe"], (1, 1, 2))
    w_l = p["nsa_cmp_w"].reshape(2, NSA_CMP_LEN, HEAD_DIM, HEAD_DIM)
    zero = jnp.zeros_like(w_l)
    w_bd = jnp.concatenate([jnp.concatenate([w_l, zero], axis=-1),
                            jnp.concatenate([zero, w_l], axis=-1)], axis=-2).astype(BF16)
    kcmp, vcmp = _compress(kc.reshape(batch, seq, LANES), u3, pe2[0], pe2[1], w_bd[0], w_bd[1])
    o_nsa = _nsa_attention(qn.reshape(batch, seq, -1), kcmp, vcmp,
                           kse.reshape(batch, seq, -1), kso.reshape(batch, seq, -1), vsd.reshape(batch, seq, -1),
                           kwd.reshape(batch, seq, -1), vwd.reshape(batch, seq, -1),
                           gates.reshape(batch, seq, -1), _overlap_matrix(seq))

    cos_m, sin_m = _mla_tables(seq)
    w_uq = p["mla_w_uq"].reshape(MLA_Q_RANK, MLA_HEADS, MLA_QK)
    wq = _mla_head_lanes(w_uq).reshape(MLA_Q_RANK, MLA_HEADS * LANES).astype(BF16)
    w_ukv = p["mla_w_ukv"].reshape(MLA_KV_RANK, MLA_HEADS, 2 * HEAD_DIM)
    wk_m = jnp.pad(w_ukv[:, :, :MLA_NOPE], ((0, 0), (0, 0), (0, LANES - MLA_NOPE)))
    wk_m = wk_m.reshape(MLA_KV_RANK, MLA_HEADS * LANES).astype(BF16)
    wv_m = w_ukv[:, :, MLA_NOPE:].reshape(MLA_KV_RANK, GROUP_WIDTH).astype(BF16)
    qm, km, vm = _mla_prep(u2, cos_m, sin_m, p["mla_q_lat_norm"].reshape(1, -1),
                           p["mla_kv_lat_norm"].reshape(1, -1), wq, wk_m, wv_m,
                           _mla_head_lanes(p["mla_q_norm"]).reshape(1, LANES),
                           _mla_head_lanes(p["mla_k_norm"]).reshape(1, LANES), seq)
    o_mla = _mla_attention(qm.reshape(batch, seq, -1), km.reshape(batch, seq, -1), vm.reshape(batch, seq, -1))

    parts = [o.reshape(m_rows, GROUP_WIDTH) for o in (o_sb, o_conv, o_nsa, o_mla)]
    return _out_proj(parts, p["group_norm"].reshape(1, -1), w_o_all, layer, x2, mod, seq)


def kernel(x, c, ada_w, ada_b, norm_mix, norm_ffn, w_in, conv_dw_w, conv_dw_b, conv_ln_g, conv_ln_b, conv_pw_w, conv_pw_b, nsa_q_norm, nsa_k_norm, nsa_cmp_pe, nsa_cmp_w, mla_q_lat_norm, mla_kv_lat_norm, mla_w_uq, mla_w_ukv, mla_q_norm, mla_k_norm, group_norm, w_o, ffn_up, ffn_conv_w, ffn_conv_b, ffn_down):
    batch, seq, d = x.shape
    depth = ada_w.shape[0]
    per_layer = dict(
        norm_mix=norm_mix, conv_dw_w=conv_dw_w, conv_dw_b=conv_dw_b, conv_ln_g=conv_ln_g,
        conv_ln_b=conv_ln_b, conv_pw_w=conv_pw_w, conv_pw_b=conv_pw_b, nsa_q_norm=nsa_q_norm,
        nsa_k_norm=nsa_k_norm, nsa_cmp_pe=nsa_cmp_pe, nsa_cmp_w=nsa_cmp_w, mla_q_lat_norm=mla_q_lat_norm,
        mla_kv_lat_norm=mla_kv_lat_norm, mla_w_uq=mla_w_uq, mla_w_ukv=mla_w_ukv, mla_q_norm=mla_q_norm,
        mla_k_norm=mla_k_norm, group_norm=group_norm)
    w_in_all = _reorder_w_in(w_in)
    w_o_all = w_o.astype(BF16)
    ffn_up_all = ffn_up.astype(BF16)
    ffn_down_all = ffn_down.astype(BF16)
    mods = _ada(c, ada_w, ada_b)
    x2 = x.reshape(batch * seq, d)
    for l in range(depth):
        p = {k: v[l] for k, v in per_layer.items()}
        x2 = _mixer(x2, mods[l], batch, seq, p, w_in_all, w_o_all, l)
        x2 = _ffn(x2, mods[l], norm_ffn[l].reshape(1, -1), ffn_up_all, ffn_conv_w[l], ffn_conv_b[l],
                  ffn_down_all, l, seq)
    return x2.reshape(batch, seq, d)
```

```python
import functools

import numpy as np
import jax
import jax.numpy as jnp
from jax import lax
from jax.experimental import pallas as pl
from jax.experimental.pallas import tpu as pltpu

F32 = jnp.float32
BF16 = jnp.bfloat16

LANES = 128
VMEM_LIMIT = 56 * 1024 * 1024

HEAD_DIM = 64
HEAD_SHIFT = 6
ROPE_THETA = 10000.0
EPS = 1e-6
GROUP_WIDTH = 512
CONV_WIDTH = 31
CONV_LN_EPS = 1e-5
NSA_CMP_LEN = 32
NSA_CMP_STRIDE = 16
NSA_SEL_LEN = 64
SEL_SHIFT = 6
NSA_N_SEL = 16
NSA_WINDOW = 512
NSA_FORCE_BONUS = 1e3
MLA_Q_RANK = 384
MLA_KV_RANK = 256
MLA_NOPE = 64
MLA_ROPE = 32
MLA_QK = MLA_NOPE + MLA_ROPE
MLA_HEADS = 8
HEADS_PER_STEP = 4

U_SBQ, U_SBK, U_SBV = 0, 512, 1024
U_SB_COLS = 1536
U_CA, U_CG = 0, 512
U_NQ = 1024
U_NKV = 1536
U_MLA = 2304
U_COLS = 4608
REF_ALIGNED_COLS = 3840

TQ = 256
TK = 256
NEG = -1e30
LOG2E = 1.4426950408889634


def _cparams(*sem):
    return pltpu.CompilerParams(dimension_semantics=sem, vmem_limit_bytes=VMEM_LIMIT)


def _dot(a, b):
    return jnp.dot(a, b, preferred_element_type=F32)


def _dot_nt(a, b):
    return lax.dot_general(a, b, (((1,), (1,)), ((), ())), preferred_element_type=F32)


def _split_dot(x, w):
    hi = x.astype(BF16)
    lo = (x - hi.astype(F32)).astype(BF16)
    return _dot(hi, w) + _dot(lo, w)


def _lane_iota(n=LANES):
    return lax.broadcasted_iota(jnp.int32, (1, n), 1)


def _ada_kernel(c_ref, w_ref, b_ref, o_ref):
    c = c_ref[...]
    cond = (c * jax.nn.sigmoid(c)).astype(BF16)
    o_ref[0] = _dot(cond, w_ref[0].astype(BF16)) + b_ref[0]


def _ada(c, ada_w, ada_b):
    n_layers, d, n = ada_w.shape
    b = c.shape[0]
    rows = 8
    tn = 1024
    cpad = jnp.pad(c, ((0, rows - b), (0, 0)))
    out = pl.pallas_call(
        _ada_kernel,
        grid=(n_layers, n // tn),
        in_specs=[
            pl.BlockSpec((rows, d), lambda l, j: (0, 0)),
            pl.BlockSpec((1, d, tn), lambda l, j: (l, 0, j)),
            pl.BlockSpec((1, 1, tn), lambda l, j: (l, 0, j)),
        ],
        out_specs=pl.BlockSpec((1, rows, tn), lambda l, j: (l, 0, j)),
        out_shape=jax.ShapeDtypeStruct((n_layers, rows, n), F32),
        compiler_params=_cparams("parallel", "parallel"),
        name="ada_mod",
    )(cpad, ada_w, ada_b.reshape(n_layers, 1, n))
    return out[:, :b].reshape(n_layers, b, 6, d)


def _modulated_norm(x, g, shift, scale):
    ms = jnp.mean(x * x, axis=-1, keepdims=True)
    return x * lax.rsqrt(ms + EPS) * g * (1.0 + scale) + shift


def _in_kernel(x_ref, mod_ref, g_ref, w_ref, sb_ref, o_ref, h_ref, *, n_sb):
    j = pl.program_id(1)

    @pl.when(j == 0)
    def _():
        m = mod_ref[0]
        h_ref[...] = _modulated_norm(x_ref[...], g_ref[...], m[0:1], m[1:2]).astype(BF16)

    @pl.when(j < n_sb)
    def _():
        sb_ref[...] = _dot(h_ref[...], w_ref[...]).astype(BF16)

    @pl.when(j >= n_sb)
    def _():
        o_ref[...] = _dot(h_ref[...], w_ref[...])


def _in_proj(x2, mod, g, w, layer, seq):
    m_rows, d = x2.shape
    n = w.shape[2]
    tm, tn = 1024, 768
    per_seq = seq // tm
    n_sb = U_SB_COLS // tn
    return pl.pallas_call(
        functools.partial(_in_kernel, n_sb=n_sb),
        grid=(m_rows // tm, n // tn),
        in_specs=[
            pl.BlockSpec((tm, d), lambda i, j: (i, 0)),
            pl.BlockSpec((1, 6, d), lambda i, j: (i // per_seq, 0, 0)),
            pl.BlockSpec((1, d), lambda i, j: (0, 0)),
            pl.BlockSpec((None, d, tn), lambda i, j: (layer, 0, j)),
        ],
        out_specs=[
            pl.BlockSpec((tm, tn), lambda i, j: (i, jnp.minimum(j, n_sb - 1))),
            pl.BlockSpec((tm, tn), lambda i, j: (i, jnp.maximum(j - n_sb, 0))),
        ],
        out_shape=[jax.ShapeDtypeStruct((m_rows, U_SB_COLS), BF16),
                   jax.ShapeDtypeStruct((m_rows, n - U_SB_COLS), F32)],
        scratch_shapes=[pltpu.VMEM((tm, d), BF16)],
        compiler_params=_cparams("parallel", "arbitrary"),
        name="in_proj",
    )(x2, mod, g, w)


def _diag_masks():
    r = lax.broadcasted_iota(jnp.int32, (TQ, TK), 0)
    c = lax.broadcasted_iota(jnp.int32, (TQ, TK), 1)
    return c <= r, c < r


def _sb_kernel(q_ref, k_ref, v_ref, o_ref, z_ref, a_ref, *, nq):
    i = pl.program_id(2)
    lane = _lane_iota()
    heads = range(HEADS_PER_STEP)
    pair = lambda hd: slice((hd // 2) * LANES, (hd // 2 + 1) * LANES)
    qs = []
    for pr in range(HEADS_PER_STEP // 2):
        q2 = q_ref[0, :, pr * LANES:(pr + 1) * LANES] * (HEAD_DIM ** -0.5)
        qs.append(jnp.where(lane < HEAD_DIM, q2, jnp.zeros_like(q2)))
        qs.append(jnp.where(lane >= HEAD_DIM, q2, jnp.zeros_like(q2)))
    tri = (lax.broadcasted_iota(jnp.int32, (TK, TK), 0)
           > lax.broadcasted_iota(jnp.int32, (TK, TK), 1)).astype(BF16)
    tri2 = jnp.concatenate([tri, tri], axis=0)
    _, strict = _diag_masks()

    for c in range(nq):
        @pl.when(i == c)
        def _():
            kv_len = (c + 1) * TK
            for hd in heads:
                z_ref[hd, :, :kv_len] = _dot_nt(qs[hd], k_ref[0, :kv_len, pair(hd)])
            carry = [jnp.zeros((TQ, 1), F32) for _ in heads]
            for j in reversed(range(c + 1)):
                sl = slice(j * TK, (j + 1) * TK)
                for hd in heads:
                    z = z_ref[hd, :, sl]
                    sp = jnp.log(1.0 + jnp.exp(-jnp.abs(z)))
                    log_beta = jnp.minimum(z, 0.0) - sp
                    neg_1m = jnp.maximum(z, 0.0) + sp
                    if j == c:
                        neg_1m = jnp.where(strict, neg_1m, 0.0)
                    hi = neg_1m.astype(BF16)
                    pieces = jnp.concatenate([hi, (neg_1m - hi.astype(F32)).astype(BF16)], axis=1)
                    later = _dot(pieces, tri2)
                    a = jnp.exp(log_beta - later - carry[hd])
                    if j == c:
                        a = jnp.where(strict, a, 0.0)
                    a_ref[hd, :, sl] = a.astype(BF16)
                    carry[hd] = carry[hd] + jnp.sum(neg_1m, axis=-1, keepdims=True)
            outs = [_dot(a_ref[hd, :, :kv_len], v_ref[0, :kv_len, pair(hd)]) for hd in heads]
            for pr in range(HEADS_PER_STEP // 2):
                o_ref[0, :, pr * LANES:(pr + 1) * LANES] = jnp.where(lane < HEAD_DIM, outs[2 * pr], outs[2 * pr + 1])


def _sb_attention(usb3):
    b, t, _ = usb3.shape
    nq = t // TQ
    wide = HEADS_PER_STEP * HEAD_DIM
    return pl.pallas_call(
        functools.partial(_sb_kernel, nq=nq),
        grid=(b, GROUP_WIDTH // wide, nq),
        in_specs=[
            pl.BlockSpec((1, TQ, wide), lambda bb, p, i: (bb, i, U_SBQ // wide + p)),
            pl.BlockSpec((1, t, wide), lambda bb, p, i: (bb, 0, U_SBK // wide + p)),
            pl.BlockSpec((1, t, wide), lambda bb, p, i: (bb, 0, U_SBV // wide + p)),
        ],
        out_specs=pl.BlockSpec((1, TQ, wide), lambda bb, p, i: (bb, i, p)),
        out_shape=jax.ShapeDtypeStruct((b, t, GROUP_WIDTH), F32),
        scratch_shapes=[pltpu.VMEM((HEADS_PER_STEP, TQ, t), F32), pltpu.VMEM((HEADS_PER_STEP, TQ, t), BF16)],
        compiler_params=_cparams("parallel", "parallel", "arbitrary"),
        name="sb_attention",
    )(usb3, usb3, usb3)


CONV_HALO = 32


def _conv_kernel(a_ref, g_ref, ah_ref, gh_ref, dww_ref, dwb_ref, lng_ref, lnb_ref, pw_ref, pwb_ref,
                 o_ref, h_ref, s_ref, *, tm):
    i = pl.program_id(1)
    h_ref[CONV_HALO:, :] = a_ref[0] * jax.nn.sigmoid(g_ref[0])
    halo = ah_ref[0] * jax.nn.sigmoid(gh_ref[0])
    h_ref[0:CONV_HALO, :] = jnp.where(i == 0, 0.0, halo)
    acc = jnp.zeros((tm, GROUP_WIDTH), F32) + dwb_ref[...]
    base = CONV_HALO - (CONV_WIDTH - 1)
    sub = 8
    for r in range(sub):
        taps = [k for k in range(CONV_WIDTH) if (base + k) % sub == r]
        if not taps:
            continue
        lo = base + taps[0]
        span = (taps[-1] - taps[0]) + tm
        s_ref[r, 0:span, :] = h_ref[lo:lo + span, :]
        for k in taps:
            acc = acc + dww_ref[k:k + 1, :] * s_ref[r, k - taps[0]:k - taps[0] + tm, :]
    mu = jnp.mean(acc, axis=-1, keepdims=True)
    cen = acc - mu
    var = jnp.mean(cen * cen, axis=-1, keepdims=True)
    hn = cen * lax.rsqrt(var + CONV_LN_EPS) * lng_ref[...] + lnb_ref[...]
    act = hn * jax.nn.sigmoid(hn)
    o_ref[0] = _dot(act.astype(BF16), pw_ref[...]) + pwb_ref[...]


def _conformer(u3, dw_w, dw_b, ln_g, ln_b, pw_w, pw_b):
    b, t, _ = u3.shape
    tm = 512
    cw = GROUP_WIDTH
    hb = tm // CONV_HALO

    def halo_map(col):
        return lambda bb, i: (bb, jnp.maximum(i * hb - 1, 0), col)

    row = lambda bb, i: (0, 0)
    return pl.pallas_call(
        functools.partial(_conv_kernel, tm=tm),
        grid=(b, t // tm),
        in_specs=[
            pl.BlockSpec((1, tm, cw), lambda bb, i: (bb, i, U_CA // cw)),
            pl.BlockSpec((1, tm, cw), lambda bb, i: (bb, i, U_CG // cw)),
            pl.BlockSpec((1, CONV_HALO, cw), halo_map(U_CA // cw)),
            pl.BlockSpec((1, CONV_HALO, cw), halo_map(U_CG // cw)),
            pl.BlockSpec((CONV_WIDTH, cw), row),
            pl.BlockSpec((1, cw), row),
            pl.BlockSpec((1, cw), row),
            pl.BlockSpec((1, cw), row),
            pl.BlockSpec((cw, cw), row),
            pl.BlockSpec((1, cw), row),
        ],
        out_specs=pl.BlockSpec((1, tm, cw), lambda bb, i: (bb, i, 0)),
        out_shape=jax.ShapeDtypeStruct((b, t, cw), F32),
        scratch_shapes=[pltpu.VMEM((tm + CONV_HALO, cw), F32), pltpu.VMEM((8, tm + CONV_HALO, cw), F32)],
        compiler_params=_cparams("parallel", "parallel"),
        name="conformer_conv",
    )(u3, u3, u3, u3, dw_w, dw_b.reshape(1, cw), ln_g.reshape(1, cw), ln_b.reshape(1, cw),
      pw_w.astype(BF16), pw_b.reshape(1, cw))


def _head_norm_rope(x, gain, cos, sin_signed, seg_mean):
    ms = _split_dot(x * x, seg_mean)
    y = x * lax.rsqrt(ms + EPS) * gain
    lane = _lane_iota()
    half = HEAD_DIM // 2
    first = (lane & (HEAD_DIM - 1)) < half
    partner = jnp.where(first, pltpu.roll(y, LANES - half, 1), pltpu.roll(y, half, 1))
    return y * cos + partner * sin_signed


def _dup(x, g):
    lane = _lane_iota()
    sw = pltpu.roll(x, HEAD_DIM, 1)
    if g == 0:
        return jnp.where(lane < HEAD_DIM, x, sw)
    return jnp.where(lane < HEAD_DIM, sw, x)


def _nsa_prep_kernel(q_ref, kv_ref, gk_ref, cos_ref, sin_ref, qg_ref, kg_ref,
                     qn_ref, kc_ref, kse_ref, kso_ref, vs_ref, kw_ref, vw_ref, gt_ref, *, per_seq):
    cos = cos_ref[...]
    sin = sin_ref[...]
    seg = lax.broadcasted_iota(jnp.int32, (LANES, LANES), 0) >> HEAD_SHIFT
    seg_mean = jnp.where(seg == lax.broadcasted_iota(jnp.int32, (LANES, LANES), 1) >> HEAD_SHIFT,
                         1.0 / HEAD_DIM, 0.0).astype(BF16)
    scale = HEAD_DIM ** -0.5
    for p in range(GROUP_WIDTH // LANES):
        x = q_ref[:, p * LANES:(p + 1) * LANES]
        qn_ref[:, p * LANES:(p + 1) * LANES] = (
            _head_norm_rope(x, qg_ref[...], cos, sin, seg_mean) * scale).astype(BF16)

    def blk(n):
        return kv_ref[:, n * LANES:(n + 1) * LANES]

    kc_ref[...] = _head_norm_rope(blk(0), kg_ref[0:1, :], cos, sin, seg_mean)
    ks = _head_norm_rope(blk(2), kg_ref[1:2, :], cos, sin, seg_mean)
    kw = _head_norm_rope(blk(4), kg_ref[2:3, :], cos, sin, seg_mean)
    vs = blk(3)
    vw = blk(5)
    tm = ks.shape[0]
    lane = _lane_iota()
    t = (pl.program_id(0) % per_seq) * tm + lax.broadcasted_iota(jnp.int32, (tm, 1), 0)
    sblk = t >> SEL_SHIFT
    hot_lo = jnp.where(lane == sblk, 1.0, 0.0)
    hot_hi = jnp.where(lane - HEAD_DIM == sblk, 1.0, 0.0)
    ks_sw = pltpu.roll(ks, HEAD_DIM, 1)
    for g in range(2):
        sl = slice(g * LANES, (g + 1) * LANES)
        kse_ref[:, sl] = jnp.where(lane < HEAD_DIM, ks if g == 0 else ks_sw, hot_hi).astype(BF16)
        kso_ref[:, sl] = jnp.where(lane >= HEAD_DIM, ks_sw if g == 0 else ks, hot_lo).astype(BF16)
        vs_ref[:, sl] = _dup(vs, g).astype(BF16)
        kw_ref[:, sl] = _dup(kw, g).astype(BF16)
        vw_ref[:, sl] = _dup(vw, g).astype(BF16)
    gates = jax.nn.sigmoid(gk_ref[...])
    gt_ref[:, 0:LANES] = gates
    gt_ref[:, LANES:2 * LANES] = pltpu.roll(gates, LANES - 12, 1)


def _nsa_prep(u2, cos, sin, q_gain, k_gain, seq):
    m_rows = u2.shape[0]
    tm = 512
    per_seq = seq // tm
    rowmap = lambda i: (i, 0)
    tab = lambda i: (i % per_seq, 0)
    const = lambda i: (0, 0)
    outs = [
        jax.ShapeDtypeStruct((m_rows, GROUP_WIDTH), BF16),
        jax.ShapeDtypeStruct((m_rows, LANES), F32),
        jax.ShapeDtypeStruct((m_rows, 2 * LANES), BF16),
        jax.ShapeDtypeStruct((m_rows, 2 * LANES), BF16),
        jax.ShapeDtypeStruct((m_rows, 2 * LANES), BF16),
        jax.ShapeDtypeStruct((m_rows, 2 * LANES), BF16),
        jax.ShapeDtypeStruct((m_rows, 2 * LANES), BF16),
        jax.ShapeDtypeStruct((m_rows, 2 * LANES), F32),
    ]
    return pl.pallas_call(
        functools.partial(_nsa_prep_kernel, per_seq=per_seq),
        grid=(m_rows // tm,),
        in_specs=[
            pl.BlockSpec((tm, GROUP_WIDTH), lambda i: (i, U_NQ // GROUP_WIDTH)),
            pl.BlockSpec((tm, 6 * LANES), lambda i: (i, U_NKV // (6 * LANES))),
            pl.BlockSpec((tm, LANES), lambda i: (i, (U_MLA + MLA_Q_RANK + MLA_KV_RANK) // LANES)),
            pl.BlockSpec((tm, LANES), tab),
            pl.BlockSpec((tm, LANES), tab),
            pl.BlockSpec((1, LANES), const),
            pl.BlockSpec((3, LANES), const),
        ],
        out_specs=[
            pl.BlockSpec((tm, GROUP_WIDTH), rowmap),
            pl.BlockSpec((tm, LANES), rowmap),
            pl.BlockSpec((tm, 2 * LANES), rowmap),
            pl.BlockSpec((tm, 2 * LANES), rowmap),
            pl.BlockSpec((tm, 2 * LANES), rowmap),
            pl.BlockSpec((tm, 2 * LANES), rowmap),
            pl.BlockSpec((tm, 2 * LANES), rowmap),
            pl.BlockSpec((tm, 2 * LANES), rowmap),
        ],
        out_shape=outs,
        compiler_params=_cparams("parallel"),
        name="nsa_prep",
    )(u2, u2, u2, cos, sin, q_gain, k_gain)


def _compress_kernel(xk_ref, xv_ref, pek_ref, pev_ref, wk_ref, wv_ref, kc_ref, vc_ref, *, nrow):
    stride = NSA_CMP_STRIDE

    def comp(x_ref, pe_ref, w_ref):
        first = jnp.zeros((nrow, LANES), F32)
        second = jnp.zeros((nrow, LANES), F32)
        for r in range(stride):
            x = x_ref[0, pl.ds(r, nrow, stride=stride), :]
            first = first + _dot((x + pe_ref[r:r + 1, :]).astype(BF16), w_ref[r])
            second = second + _dot((x + pe_ref[stride + r:stride + r + 1, :]).astype(BF16), w_ref[stride + r])
        return first + pltpu.roll(second, nrow - 1, 0)

    ck = comp(xk_ref, pek_ref, wk_ref)
    cv = comp(xv_ref, pev_ref, wv_ref)
    for g in range(2):
        kc_ref[0, g] = _dup(ck, g).astype(BF16)
        vc_ref[0, g] = _dup(cv, g).astype(BF16)


def _compress(kc3, u3, pe_k, pe_v, wk, wv):
    b, t, _ = kc3.shape
    nrow = t // NSA_CMP_STRIDE
    const2 = lambda bb: (0, 0)
    const3 = lambda bb: (0, 0, 0)
    out = jax.ShapeDtypeStruct((b, 2, nrow, LANES), BF16)
    return pl.pallas_call(
        functools.partial(_compress_kernel, nrow=nrow),
        grid=(b,),
        in_specs=[
            pl.BlockSpec((1, t, LANES), lambda bb: (bb, 0, 0)),
            pl.BlockSpec((1, t, LANES), lambda bb: (bb, 0, U_NKV // LANES + 1)),
            pl.BlockSpec(pe_k.shape, const2),
            pl.BlockSpec(pe_v.shape, const2),
            pl.BlockSpec(wk.shape, const3),
            pl.BlockSpec(wv.shape, const3),
        ],
        out_specs=[pl.BlockSpec((1, 2, nrow, LANES), lambda bb: (bb, 0, 0, 0))] * 2,
        out_shape=[out, out],
        compiler_params=_cparams("parallel"),
        name="nsa_compress",
    )(kc3, u3, pe_k, pe_v, wk, wv)


def _attend_many(jobs, scale=None):
    tiles = [k.shape[0] // TK for (_, k, _, _, _, _) in jobs]
    for (q, k, _, s_ref, _, _), n in zip(jobs, tiles):
        s_ref[:, :n * TK] = _dot_nt(q, k)
    maxima = []
    for (_, _, _, s_ref, _, masks), n in zip(jobs, tiles):
        fold = None
        for j in range(n):
            sl = slice(j * TK, (j + 1) * TK)
            s = s_ref[:, sl]
            if j in masks:
                s = jnp.where(masks[j], s, NEG)
                s_ref[:, sl] = s
            f = jnp.maximum(s[:, :LANES], s[:, LANES:])
            fold = f if fold is None else jnp.maximum(fold, f)
        maxima.append(jnp.max(fold, axis=-1, keepdims=True))
    sums = []
    for (_, _, _, s_ref, p_ref, _), n, m in zip(jobs, tiles, maxima):
        fold = None
        for j in range(n):
            sl = slice(j * TK, (j + 1) * TK)
            p = jnp.exp2((s_ref[:, sl] - m) * (LOG2E if scale is None else scale * LOG2E))
            f = p[:, :LANES] + p[:, LANES:]
            fold = f if fold is None else fold + f
            p_ref[:, sl] = p.astype(BF16)
        sums.append(jnp.sum(fold, axis=-1, keepdims=True))
    return [_dot(p_ref[:, :n * TK], v) / l for (_, _, v, _, p_ref, _), n, l in zip(jobs, tiles, sums)]


def _nsa_kernel(q_ref, kc_ref, vc_ref, kse_ref, kso_ref, vs_ref, kw_ref, vw_ref, gt_ref, ov_ref,
                o_ref, s_ref, p_ref, sw_ref, pw_ref, *, nq, n_cmp, n_sblk, n_sel):
    i = pl.program_id(2)
    lane = _lane_iota()
    row = i * TQ + lax.broadcasted_iota(jnp.int32, (TQ, 1), 0)
    rep = 4
    q2s = [q_ref[0, :, p * LANES:(p + 1) * LANES] for p in range(2)]
    qs = []
    for q2 in q2s:
        qs.append(jnp.where(lane < HEAD_DIM, q2, jnp.zeros_like(q2)))
        qs.append(jnp.where(lane >= HEAD_DIM, q2, jnp.zeros_like(q2)))

    kc = kc_ref[0, 0]
    vc = vc_ref[0, 0]
    cmask = ((lane * NSA_CMP_STRIDE + (NSA_CMP_LEN - 1)) <= row) & (lane < n_cmp)
    psum = jnp.zeros((TQ, LANES), F32)
    s_cmp = [jnp.where(cmask, _dot_nt(qs[r], kc), NEG) for r in range(rep)]
    p_cmp = []
    for r in range(rep):
        mx = jnp.max(s_cmp[r], axis=-1, keepdims=True)
        p = jnp.where(cmask, jnp.exp(s_cmp[r] - mx), 0.0)
        p = p / jnp.maximum(jnp.sum(p, axis=-1, keepdims=True), 1e-30)
        p_cmp.append(p.astype(BF16))
        psum = psum + p
    o_cmp = [_dot(p_cmp[r], vc) for r in range(rep)]
    ov_t = ov_ref[0:n_sblk, :]
    p_hi = psum.astype(BF16)
    p_lo = (psum - p_hi.astype(F32)).astype(BF16)
    imp = _dot_nt(ov_t, p_hi) + _dot_nt(ov_t, p_lo)
    cur = (i * TQ + _lane_iota(TQ)) >> SEL_SHIFT
    blk = lax.broadcasted_iota(jnp.int32, (n_sblk, 1), 0)
    eligible = blk <= cur
    forced = (blk == 0) | (blk == cur) | (blk == cur - 1)
    score = jnp.where(eligible, imp + jnp.where(forced, NSA_FORCE_BONUS, 0.0), NEG)
    rank = jnp.zeros((n_sblk, TQ), F32)
    for sp in range(n_sblk):
        other = score[sp:sp + 1, :]
        beats = (other > score) | ((other == score) & (blk > sp))
        rank = rank + jnp.where(beats, 1.0, 0.0)
    bias_t = jnp.where(eligible & (rank < n_sel), 0.0, NEG)

    bias_lo = jnp.concatenate([bias_t, jnp.zeros((LANES - n_sblk, TQ), F32)], axis=0).T
    bias_hi = pltpu.roll(bias_lo, HEAD_DIM, 1).astype(BF16)
    bias_lo = bias_lo.astype(BF16)
    q_aug = []
    for q2 in q2s:
        q_aug.append(jnp.where(lane < HEAD_DIM, q2, bias_hi))
        q_aug.append(jnp.where(lane >= HEAD_DIM, q2, bias_lo))
    causal, strict = _diag_masks()
    wtiles = NSA_WINDOW // TK
    gt = gt_ref[0]

    for c in range(nq):
        @pl.when(i == c)
        def _():
            kv_len = (c + 1) * TK
            lo = max(c - wtiles, 0) * TK
            wmasks = {c - max(c - wtiles, 0): causal}
            if c >= wtiles:
                wmasks[0] = jnp.logical_not(causal)
            jobs = []
            for r in range(rep):
                ks_ref = kse_ref if r % 2 == 0 else kso_ref
                jobs.append((q_aug[r], ks_ref[0, :kv_len, :], vs_ref[0, :kv_len, :],
                             s_ref.at[r], p_ref.at[r], {c: causal}))
            for r in range(rep):
                jobs.append((qs[r], kw_ref[0, lo:kv_len, :], vw_ref[0, lo:kv_len, :],
                             sw_ref.at[r], pw_ref.at[r], wmasks))
            res = _attend_many(jobs)
            outs = []
            for r in range(rep):
                outs.append(gt[:, 3 * r:3 * r + 1] * o_cmp[r] + gt[:, 3 * r + 1:3 * r + 2] * res[r]
                            + gt[:, 3 * r + 2:3 * r + 3] * res[rep + r])
            o_ref[0, :, 0:LANES] = jnp.where(lane < HEAD_DIM, outs[0], outs[1])
            o_ref[0, :, LANES:2 * LANES] = jnp.where(lane < HEAD_DIM, outs[2], outs[3])


def _nsa_attention(qn, kcmp, vcmp, kse, kso, vsd, kwd, vwd, gates, overlap):
    b, t, _ = qn.shape
    nq = t // TQ
    n_cmp = (t - NSA_CMP_LEN) // NSA_CMP_STRIDE + 1
    n_sblk = t // NSA_SEL_LEN
    nrow = kcmp.shape[2]
    wlen = NSA_WINDOW + TK
    kv = lambda bb, g, i: (bb, 0, g)
    cm = lambda bb, g, i: (bb, g, 0, 0)
    return pl.pallas_call(
        functools.partial(_nsa_kernel, nq=nq, n_cmp=n_cmp, n_sblk=n_sblk, n_sel=min(NSA_N_SEL, n_sblk)),
        grid=(b, 2, nq),
        in_specs=[
            pl.BlockSpec((1, TQ, 2 * LANES), lambda bb, g, i: (bb, i, g)),
            pl.BlockSpec((1, 1, nrow, LANES), cm),
            pl.BlockSpec((1, 1, nrow, LANES), cm),
            pl.BlockSpec((1, t, LANES), kv),
            pl.BlockSpec((1, t, LANES), kv),
            pl.BlockSpec((1, t, LANES), kv),
            pl.BlockSpec((1, t, LANES), kv),
            pl.BlockSpec((1, t, LANES), kv),
            pl.BlockSpec((1, TQ, LANES), lambda bb, g, i: (bb, i, g)),
            pl.BlockSpec((LANES, LANES), lambda bb, g, i: (0, 0)),
        ],
        out_specs=pl.BlockSpec((1, TQ, 2 * LANES), lambda bb, g, i: (bb, i, g)),
        out_shape=jax.ShapeDtypeStruct((b, t, GROUP_WIDTH), F32),
        scratch_shapes=[pltpu.VMEM((4, TQ, t), F32), pltpu.VMEM((4, TQ, t), BF16),
                        pltpu.VMEM((4, TQ, wlen), F32), pltpu.VMEM((4, TQ, wlen), BF16)],
        compiler_params=_cparams("parallel", "parallel", "arbitrary"),
        name="nsa_attention",
    )(qn, kcmp, vcmp, kse, kso, vsd, kwd, vwd, gates, overlap)


def _mla_prep_kernel(u_ref, cos_ref, sin_ref, qlg_ref, kvlg_ref, wq_ref, wk_ref, wv_ref, qg_ref, kg_ref,
                     q_ref, k_ref, v_ref):
    lane = _lane_iota()
    cos = cos_ref[...]
    sin = sin_ref[...]
    half = MLA_ROPE // 2

    def lat_norm(x, g):
        ms = jnp.mean(x * x, axis=-1, keepdims=True)
        return (x * lax.rsqrt(ms + EPS) * g).astype(BF16)

    live = lane < MLA_QK

    def head_norm_rope(x, g):
        ms = jnp.sum(jnp.where(live, x * x, 0.0), axis=-1, keepdims=True) * (1.0 / MLA_QK)
        y = x * lax.rsqrt(ms + EPS) * g
        return y * cos + pltpu.roll(y, LANES - half, 1) * sin

    q = _dot(lat_norm(u_ref[:, 0:MLA_Q_RANK], qlg_ref[...]), wq_ref[...])
    kvn = lat_norm(u_ref[:, MLA_Q_RANK:MLA_Q_RANK + MLA_KV_RANK], kvlg_ref[...])
    kn = _dot(kvn, wk_ref[...])
    v_ref[...] = _dot(kvn, wv_ref[...]).astype(BF16)
    gk = u_ref[:, MLA_Q_RANK + MLA_KV_RANK:MLA_Q_RANK + MLA_KV_RANK + LANES]
    k_rope = jnp.where(lane >= MLA_NOPE, gk, 0.0)
    for h in range(MLA_HEADS):
        sl = slice(h * LANES, (h + 1) * LANES)
        q_ref[:, sl] = head_norm_rope(q[:, sl], qg_ref[...]).astype(BF16)
        k_ref[:, sl] = head_norm_rope(kn[:, sl] + k_rope, kg_ref[...]).astype(BF16)


def _mla_prep(u2, cos, sin, q_lat_gain, kv_lat_gain, wq, wk, wv, q_gain, k_gain, seq):
    m_rows = u2.shape[0]
    tm = 512
    per_seq = seq // tm
    width = MLA_Q_RANK + MLA_KV_RANK + LANES
    rowmap = lambda i: (i, 0)
    tab = lambda i: (i % per_seq, 0)
    const = lambda i: (0, 0)
    return pl.pallas_call(
        _mla_prep_kernel,
        grid=(m_rows // tm,),
        in_specs=[
            pl.BlockSpec((tm, width), lambda i: (i, U_MLA // width)),
            pl.BlockSpec((tm, LANES), tab),
            pl.BlockSpec((tm, LANES), tab),
            pl.BlockSpec((1, MLA_Q_RANK), const),
            pl.BlockSpec((1, MLA_KV_RANK), const),
            pl.BlockSpec(wq.shape, const),
            pl.BlockSpec(wk.shape, const),
            pl.BlockSpec(wv.shape, const),
            pl.BlockSpec((1, LANES), const),
            pl.BlockSpec((1, LANES), const),
        ],
        out_specs=[
            pl.BlockSpec((tm, MLA_HEADS * LANES), rowmap),
            pl.BlockSpec((tm, MLA_HEADS * LANES), rowmap),
            pl.BlockSpec((tm, GROUP_WIDTH), rowmap),
        ],
        out_shape=[
            jax.ShapeDtypeStruct((m_rows, MLA_HEADS * LANES), BF16),
            jax.ShapeDtypeStruct((m_rows, MLA_HEADS * LANES), BF16),
            jax.ShapeDtypeStruct((m_rows, GROUP_WIDTH), BF16),
        ],
        compiler_params=_cparams("parallel"),
        name="mla_prep",
    )(u2, cos, sin, q_lat_gain, kv_lat_gain, wq, wk, wv, q_gain, k_gain)


def _mla_kernel(q_ref, k_ref, v_ref, o_ref, s_ref, p_ref, *, nq):
    i = pl.program_id(2)
    lane = _lane_iota()
    causal, _ = _diag_masks()
    scale = MLA_QK ** -0.5
    for c in range(nq):
        @pl.when(i == c)
        def _():
            kv_len = (c + 1) * TK
            jobs = []
            for hd in range(HEADS_PER_STEP):
                sl = slice(hd * LANES, (hd + 1) * LANES)
                vs = slice((hd // 2) * LANES, (hd // 2 + 1) * LANES)
                jobs.append((q_ref[0, :, sl], k_ref[0, :kv_len, sl], v_ref[0, :kv_len, vs],
                             s_ref.at[hd], p_ref.at[hd], {c: causal}))
            outs = _attend_many(jobs, scale=scale)
            for pr in range(HEADS_PER_STEP // 2):
                o_ref[0, :, pr * LANES:(pr + 1) * LANES] = jnp.where(lane < HEAD_DIM, outs[2 * pr], outs[2 * pr + 1])


def _mla_attention(qm, km, vm):
    b, t, _ = qm.shape
    nq = t // TQ
    return pl.pallas_call(
        functools.partial(_mla_kernel, nq=nq),
        grid=(b, MLA_HEADS // HEADS_PER_STEP, nq),
        in_specs=[
            pl.BlockSpec((1, TQ, HEADS_PER_STEP * LANES), lambda bb, p, i: (bb, i, p)),
            pl.BlockSpec((1, t, HEADS_PER_STEP * LANES), lambda bb, p, i: (bb, 0, p)),
            pl.BlockSpec((1, t, HEADS_PER_STEP * HEAD_DIM), lambda bb, p, i: (bb, 0, p)),
        ],
        out_specs=pl.BlockSpec((1, TQ, HEADS_PER_STEP * HEAD_DIM), lambda bb, p, i: (bb, i, p)),
        out_shape=jax.ShapeDtypeStruct((b, t, GROUP_WIDTH), F32),
        scratch_shapes=[pltpu.VMEM((HEADS_PER_STEP, TQ, t), F32), pltpu.VMEM((HEADS_PER_STEP, TQ, t), BF16)],
        compiler_params=_cparams("parallel", "parallel", "arbitrary"),
        name="mla_attention",
    )(qm, km, vm)


def _out_kernel(a_ref, b_ref, c_ref, d_ref, gn_ref, w_ref, x_ref, mod_ref, o_ref):
    acc = None
    for gi, r in enumerate((a_ref, b_ref, c_ref, d_ref)):
        y = r[...]
        ms = jnp.mean(y * y, axis=-1, keepdims=True)
        sl = slice(gi * GROUP_WIDTH, (gi + 1) * GROUP_WIDTH)
        part = _dot((y * lax.rsqrt(ms + EPS) * gn_ref[:, sl]).astype(BF16), w_ref[sl, :])
        acc = part if acc is None else acc + part
    o_ref[...] = x_ref[...] + mod_ref[0, 2:3, :] * acc


def _out_proj(parts, gn, w, layer, x2, mod, seq):
    m_rows, d = x2.shape
    tm = 512
    per_seq = seq // tm
    part = pl.BlockSpec((tm, GROUP_WIDTH), lambda i: (i, 0))
    return pl.pallas_call(
        _out_kernel,
        grid=(m_rows // tm,),
        in_specs=[
            part, part, part, part,
            pl.BlockSpec((1, 4 * GROUP_WIDTH), lambda i: (0, 0)),
            pl.BlockSpec((None, 4 * GROUP_WIDTH, d), lambda i: (layer, 0, 0)),
            pl.BlockSpec((tm, d), lambda i: (i, 0)),
            pl.BlockSpec((1, 6, d), lambda i: (i // per_seq, 0, 0)),
        ],
        out_specs=pl.BlockSpec((tm, d), lambda i: (i, 0)),
        out_shape=jax.ShapeDtypeStruct((m_rows, d), F32),
        compiler_params=_cparams("parallel"),
        name="out_proj",
    )(*parts, gn, w, x2, mod)


FFN_HALO = 16


def _ffn_kernel(x_ref, xh_ref, mod_ref, g_ref, wa_ref, wg_ref, cwa_ref, cwg_ref, cba_ref, cbg_ref, wd_ref, wdl_ref,
                o_ref, h_ref, act_ref, acc_ref, *, per_seq, nj):
    i = pl.program_id(0)
    j = pl.program_id(1)
    m = mod_ref[0]

    def conv(u, cw_ref, cb_ref):
        y = cw_ref[2:3, :] * u + cw_ref[1:2, :] * pltpu.roll(u, 1, 0) + cw_ref[0:1, :] * pltpu.roll(u, 2, 0)
        return y[FFN_HALO:, :] + cb_ref[...]

    def activation():
        h = h_ref[...]
        ya = conv(_dot(h, wa_ref[...]), cwa_ref, cba_ref)
        yg = conv(_dot(h, wg_ref[...]), cwg_ref, cbg_ref)
        return ((yg * jax.nn.sigmoid(yg)) * ya).astype(BF16)

    @pl.when(j == 0)
    def _():
        h_ref[FFN_HALO:, :] = _modulated_norm(x_ref[...], g_ref[...], m[3:4], m[4:5]).astype(BF16)
        halo = _modulated_norm(xh_ref[...], g_ref[...], m[3:4], m[4:5])
        h_ref[0:FFN_HALO, :] = jnp.where(i % per_seq == 0, 0.0, halo).astype(BF16)
        act_ref[...] = activation()

    @pl.when(j == 1)
    def _():
        new = activation()
        acc_ref[...] = _dot(act_ref[...], wd_ref[...])
        act_ref[...] = new

    @pl.when((j > 1) & (j < nj - 1))
    def _():
        new = activation()
        acc_ref[...] += _dot(act_ref[...], wd_ref[...])
        act_ref[...] = new

    @pl.when(j == nj - 1)
    def _():
        new = activation()
        total = acc_ref[...] + _dot(act_ref[...], wd_ref[...]) + _dot(new, wdl_ref[...])
        o_ref[...] = x_ref[...] + m[5:6] * total


def _ffn(x2, mod, g, w_up, conv_w, conv_b, w_down, layer, seq):
    m_rows, d = x2.shape
    d_ff = w_down.shape[1]
    tm, tn = 512, 512
    per_seq = seq // tm
    nj = d_ff // tn
    hb = tm // FFN_HALO
    cb = conv_b.reshape(1, 2 * d_ff)
    assert nj >= 3
    return pl.pallas_call(
        functools.partial(_ffn_kernel, per_seq=per_seq, nj=nj),
        grid=(m_rows // tm, nj),
        in_specs=[
            pl.BlockSpec((tm, d), lambda i, j: (i, 0)),
            pl.BlockSpec((FFN_HALO, d), lambda i, j: (jnp.maximum(i * hb - 1, 0), 0)),
            pl.BlockSpec((1, 6, d), lambda i, j: (i // per_seq, 0, 0)),
            pl.BlockSpec((1, d), lambda i, j: (0, 0)),
            pl.BlockSpec((None, d, tn), lambda i, j: (layer, 0, j)),
            pl.BlockSpec((None, d, tn), lambda i, j: (layer, 0, nj + j)),
            pl.BlockSpec((3, tn), lambda i, j: (0, j)),
            pl.BlockSpec((3, tn), lambda i, j: (0, nj + j)),
            pl.BlockSpec((1, tn), lambda i, j: (0, j)),
            pl.BlockSpec((1, tn), lambda i, j: (0, nj + j)),
            pl.BlockSpec((None, tn, d), lambda i, j: (layer, jnp.maximum(j - 1, 0), 0)),
            pl.BlockSpec((None, tn, d), lambda i, j: (layer, nj - 1, 0)),
        ],
        out_specs=pl.BlockSpec((tm, d), lambda i, j: (i, 0)),
        out_shape=jax.ShapeDtypeStruct((m_rows, d), F32),
        scratch_shapes=[pltpu.VMEM((tm + FFN_HALO, d), BF16), pltpu.VMEM((tm, tn), BF16),
                        pltpu.VMEM((tm, d), F32)],
        compiler_params=_cparams("parallel", "arbitrary"),
        name="conv_glu_ffn",
    )(x2, x2, mod, g, w_up, w_up, conv_w, conv_w, cb, cb, w_down, w_down)


REORDER_COLS = 768


def _reorder_kernel(wt_ref, o_ref, t_ref, *, n_tail):
    j = pl.program_id(2)

    @pl.when(j < pl.num_programs(2) - 1)
    def _():
        o_ref[0] = wt_ref[0].T.astype(BF16)

    @pl.when(j == pl.num_programs(2) - 1)
    def _():
        n_gate, lat, half = 24, MLA_Q_RANK + MLA_KV_RANK, MLA_ROPE // 2
        rope = n_gate + lat
        assert rope + MLA_ROPE == n_tail
        t_ref[...] = jnp.zeros_like(t_ref)
        t_ref[0:lat, :] = wt_ref[0, n_gate:rope, :]
        t_ref[lat:lat + n_gate, :] = wt_ref[0, 0:n_gate, :]
        t_ref[lat + MLA_NOPE:lat + MLA_QK, :] = wt_ref[0, rope:rope + MLA_ROPE, :]
        t_ref[lat + MLA_QK:lat + MLA_QK + half, :] = wt_ref[0, rope:rope + half, :]
        o_ref[0] = t_ref[...].T.astype(BF16)


def _reorder_w_in(w):
    n_layers, d, n_in = w.shape
    tr = 512
    assert REF_ALIGNED_COLS % REORDER_COLS == 0 and U_COLS - REF_ALIGNED_COLS == REORDER_COLS
    return pl.pallas_call(
        functools.partial(_reorder_kernel, n_tail=n_in - REF_ALIGNED_COLS),
        grid=(n_layers, d // tr, U_COLS // REORDER_COLS),
        in_specs=[pl.BlockSpec((1, REORDER_COLS, tr), lambda l, i, j: (l, j, i))],
        out_specs=pl.BlockSpec((1, tr, REORDER_COLS), lambda l, i, j: (l, i, j)),
        out_shape=jax.ShapeDtypeStruct((n_layers, d, U_COLS), BF16),
        scratch_shapes=[pltpu.VMEM((REORDER_COLS, tr), F32)],
        compiler_params=_cparams("parallel", "parallel", "arbitrary"),
        name="reorder_w_in",
    )(jnp.swapaxes(w, 1, 2))


def _rope_tables(seq, dim, lane_cos, lane_sin):
    inv = ROPE_THETA ** (-jnp.arange(0, dim, 2, dtype=F32) / dim)
    ang = jnp.arange(seq).astype(F32)[:, None] * inv[None, :]
    cos, sin = jnp.cos(ang), jnp.sin(ang)
    return lane_cos(cos), lane_sin(sin)


def _nsa_tables(seq):
    return _rope_tables(seq, HEAD_DIM,
                        lambda c: jnp.concatenate([c, c, c, c], axis=1),
                        lambda s: jnp.concatenate([-s, s, -s, s], axis=1))


def _mla_tables(seq):
    ones = jnp.ones((seq, MLA_NOPE), F32)
    zeros = jnp.zeros((seq, MLA_NOPE), F32)
    pad0 = jnp.zeros((seq, LANES - MLA_QK), F32)
    return _rope_tables(seq, MLA_ROPE,
                        lambda c: jnp.concatenate([ones, c, c, pad0], axis=1),
                        lambda s: jnp.concatenate([zeros, -s, s, pad0], axis=1))


def _mla_head_lanes(v):
    half = MLA_ROPE // 2
    pad = jnp.zeros(v.shape[:-1] + (LANES - MLA_QK - half,), v.dtype)
    return jnp.concatenate([v, v[..., MLA_NOPE:MLA_NOPE + half], pad], axis=-1)


def _overlap_matrix(seq):
    n_cmp = (seq - NSA_CMP_LEN) // NSA_CMP_STRIDE + 1
    n_sblk = seq // NSA_SEL_LEN
    starts = np.arange(n_cmp) * NSA_CMP_STRIDE
    sel_start = np.arange(n_sblk) * NSA_SEL_LEN
    ov = np.clip(np.minimum(starts[:, None] + NSA_CMP_LEN, sel_start[None, :] + NSA_SEL_LEN)
                 - np.maximum(starts[:, None], sel_start[None, :]), 0, None) / NSA_CMP_LEN
    full = np.zeros((LANES, LANES), np.float32)
    full[:n_sblk, :n_cmp] = ov.T
    return jnp.asarray(full, BF16)


def _mixer(x2, mod, batch, seq, p, w_in_all, w_o_all, layer):
    m_rows = x2.shape[0]
    usb, u2 = _in_proj(x2, mod, p["norm_mix"].reshape(1, -1), w_in_all, layer, seq)
    u3 = u2.reshape(batch, seq, U_COLS - U_SB_COLS)

    o_sb = _sb_attention(usb.reshape(batch, seq, U_SB_COLS))
    o_conv = _conformer(u3, p["conv_dw_w"], p["conv_dw_b"], p["conv_ln_g"], p["conv_ln_b"],
                        p["conv_pw_w"], p["conv_pw_b"])

    cos_n, sin_n = _nsa_tables(seq)
    q_gain = jnp.tile(p["nsa_q_norm"], 2).reshape(1, LANES)
    k_gain = jnp.tile(p["nsa_k_norm"], (1, 2))
    qn, kc, kse, kso, vsd, kwd, vwd, gates = _nsa_prep(u2, cos_n, sin_n, q_gain, k_gain, seq)
    pe2 = jnp.tile(p["nsa_cmp_pe"], (1, 1, 2))
    w_l = p["nsa_cmp_w"].reshape(2, NSA_CMP_LEN, HEAD_DIM, HEAD_DIM)
    zero = jnp.zeros_like(w_l)
    w_bd = jnp.concatenate([jnp.concatenate([w_l, zero], axis=-1),
                            jnp.concatenate([zero, w_l], axis=-1)], axis=-2).astype(BF16)
    kcmp, vcmp = _compress(kc.reshape(batch, seq, LANES), u3, pe2[0], pe2[1], w_bd[0], w_bd[1])
    o_nsa = _nsa_attention(qn.reshape(batch, seq, -1), kcmp, vcmp,
                           kse.reshape(batch, seq, -1), kso.reshape(batch, seq, -1), vsd.reshape(batch, seq, -1),
                           kwd.reshape(batch, seq, -1), vwd.reshape(batch, seq, -1),
                           gates.reshape(batch, seq, -1), _overlap_matrix(seq))

    cos_m, sin_m = _mla_tables(seq)
    w_uq = p["mla_w_uq"].reshape(MLA_Q_RANK, MLA_HEADS, MLA_QK)
    wq = _mla_head_lanes(w_uq).reshape(MLA_Q_RANK, MLA_HEADS * LANES).astype(BF16)
    w_ukv = p["mla_w_ukv"].reshape(MLA_KV_RANK, MLA_HEADS, 2 * HEAD_DIM)
    wk_m = jnp.pad(w_ukv[:, :, :MLA_NOPE], ((0, 0), (0, 0), (0, LANES - MLA_NOPE)))
    wk_m = wk_m.reshape(MLA_KV_RANK, MLA_HEADS * LANES).astype(BF16)
    wv_m = w_ukv[:, :, MLA_NOPE:].reshape(MLA_KV_RANK, GROUP_WIDTH).astype(BF16)
    qm, km, vm = _mla_prep(u2, cos_m, sin_m, p["mla_q_lat_norm"].reshape(1, -1),
                           p["mla_kv_lat_norm"].reshape(1, -1), wq, wk_m, wv_m,
                           _mla_head_lanes(p["mla_q_norm"]).reshape(1, LANES),
                           _mla_head_lanes(p["mla_k_norm"]).reshape(1, LANES), seq)
    o_mla = _mla_attention(qm.reshape(batch, seq, -1), km.reshape(batch, seq, -1), vm.reshape(batch, seq, -1))

    parts = [o.reshape(m_rows, GROUP_WIDTH) for o in (o_sb, o_conv, o_nsa, o_mla)]
    return _out_proj(parts, p["group_norm"].reshape(1, -1), w_o_all, layer, x2, mod, seq)


def kernel(x, c, ada_w, ada_b, norm_mix, norm_ffn, w_in, conv_dw_w, conv_dw_b, conv_ln_g, conv_ln_b, conv_pw_w, conv_pw_b, nsa_q_norm, nsa_k_norm, nsa_cmp_pe, nsa_cmp_w, mla_q_lat_norm, mla_kv_lat_norm, mla_w_uq, mla_w_ukv, mla_q_norm, mla_k_norm, group_norm, w_o, ffn_up, ffn_conv_w, ffn_conv_b, ffn_down):
    batch, seq, d = x.shape
    depth = ada_w.shape[0]
    per_layer = dict(
        norm_mix=norm_mix, conv_dw_w=conv_dw_w, conv_dw_b=conv_dw_b, conv_ln_g=conv_ln_g,
        conv_ln_b=conv_ln_b, conv_pw_w=conv_pw_w, conv_pw_b=conv_pw_b, nsa_q_norm=nsa_q_norm,
        nsa_k_norm=nsa_k_norm, nsa_cmp_pe=nsa_cmp_pe, nsa_cmp_w=nsa_cmp_w, mla_q_lat_norm=mla_q_lat_norm,
        mla_kv_lat_norm=mla_kv_lat_norm, mla_w_uq=mla_w_uq, mla_w_ukv=mla_w_ukv, mla_q_norm=mla_q_norm,
        mla_k_norm=mla_k_norm, group_norm=group_norm)
    w_in_all = _reorder_w_in(w_in)
    w_o_all = w_o.astype(BF16)
    ffn_up_all = ffn_up.astype(BF16)
    ffn_down_all = ffn_down.astype(BF16)
    mods = _ada(c, ada_w, ada_b)
    x2 = x.reshape(batch * seq, d)
    for l in range(depth):
        p = {k: v[l] for k, v in per_layer.items()}
        x2 = _mixer(x2, mods[l], batch, seq, p, w_in_all, w_o_all, l)
        x2 = _ffn(x2, mods[l], norm_ffn[l].reshape(1, -1), ffn_up_all, ffn_conv_w[l], ffn_conv_b[l],
                  ffn_down_all, l, seq)
    return x2.reshape(batch, seq, d)
```

```python
import functools

import numpy as np
import jax
import jax.numpy as jnp
from jax import lax
from jax.experimental import pallas as pl
from jax.experimental.pallas import tpu as pltpu

F32 = jnp.float32
BF16 = jnp.bfloat16

LANES = 128
VMEM_LIMIT = 56 * 1024 * 1024

HEAD_DIM = 64
HEAD_SHIFT = 6
ROPE_THETA = 10000.0
EPS = 1e-6
GROUP_WIDTH = 512
CONV_WIDTH = 31
CONV_LN_EPS = 1e-5
NSA_CMP_LEN = 32
NSA_CMP_STRIDE = 16
NSA_SEL_LEN = 64
SEL_SHIFT = 6
NSA_N_SEL = 16
NSA_WINDOW = 512
NSA_FORCE_BONUS = 1e3
NSA_GATES = 24
MLA_Q_RANK = 384
MLA_KV_RANK = 256
MLA_NOPE = 64
MLA_ROPE = 32
MLA_QK = MLA_NOPE + MLA_ROPE
MLA_HEADS = 8
HEADS_PER_STEP = 4

U_SBQ, U_SBK, U_SBV = 0, 512, 1024
U_SB_COLS = 1536
U_CA, U_CG = 0, 512
U_NQ = 1024
U_NKV = 1536
U_MLA = 2304
U_COLS = 4608
REF_ALIGNED_COLS = 3840

TQ = 256
TK = 256
NEG = -1e30
LOG2E = 1.4426950408889634


def _cparams(*sem):
    return pltpu.CompilerParams(dimension_semantics=sem, vmem_limit_bytes=VMEM_LIMIT)


def _dot(a, b):
    return jnp.dot(a, b, preferred_element_type=F32)


def _dot_nt(a, b):
    return lax.dot_general(a, b, (((1,), (1,)), ((), ())), preferred_element_type=F32)


def _split_dot(x, w):
    hi = x.astype(BF16)
    lo = (x - hi.astype(F32)).astype(BF16)
    return _dot(hi, w) + _dot(lo, w)


def _lane_iota(n=LANES):
    return lax.broadcasted_iota(jnp.int32, (1, n), 1)


def _ada_kernel(c_ref, w_ref, b_ref, o_ref):
    c = c_ref[...]
    cond = (c * jax.nn.sigmoid(c)).astype(BF16)
    o_ref[0] = _dot(cond, w_ref[0].astype(BF16)) + b_ref[0]


def _ada(c, ada_w, ada_b):
    n_layers, d, n = ada_w.shape
    b = c.shape[0]
    rows = 8
    tn = 1024
    cpad = jnp.pad(c, ((0, rows - b), (0, 0)))
    out = pl.pallas_call(
        _ada_kernel,
        grid=(n_layers, n // tn),
        in_specs=[
            pl.BlockSpec((rows, d), lambda l, j: (0, 0)),
            pl.BlockSpec((1, d, tn), lambda l, j: (l, 0, j)),
            pl.BlockSpec((1, 1, tn), lambda l, j: (l, 0, j)),
        ],
        out_specs=pl.BlockSpec((1, rows, tn), lambda l, j: (l, 0, j)),
        out_shape=jax.ShapeDtypeStruct((n_layers, rows, n), F32),
        compiler_params=_cparams("parallel", "parallel"),
        name="ada_mod",
    )(cpad, ada_w, ada_b.reshape(n_layers, 1, n))
    return out[:, :b].reshape(n_layers, b, 6, d)


def _modulated_norm(x, g, shift, scale):
    ms = jnp.mean(x * x, axis=-1, keepdims=True)
    return x * lax.rsqrt(ms + EPS) * g * (1.0 + scale) + shift


def _in_kernel(x_ref, mod_ref, g_ref, w_ref, sb_ref, o_ref, h_ref, *, n_sb):
    j = pl.program_id(1)

    @pl.when(j == 0)
    def _():
        m = mod_ref[0]
        h_ref[...] = _modulated_norm(x_ref[...], g_ref[...], m[0:1], m[1:2]).astype(BF16)

    @pl.when(j < n_sb)
    def _():
        sb_ref[...] = _dot(h_ref[...], w_ref[...]).astype(BF16)

    @pl.when(j >= n_sb)
    def _():
        o_ref[...] = _dot(h_ref[...], w_ref[...])


def _in_proj(x2, mod, g, w, layer, seq):
    m_rows, d = x2.shape
    n = w.shape[2]
    tm, tn = 1024, 768
    per_seq = seq // tm
    n_sb = U_SB_COLS // tn
    return pl.pallas_call(
        functools.partial(_in_kernel, n_sb=n_sb),
        grid=(m_rows // tm, n // tn),
        in_specs=[
            pl.BlockSpec((tm, d), lambda i, j: (i, 0)),
            pl.BlockSpec((1, 6, d), lambda i, j: (i // per_seq, 0, 0)),
            pl.BlockSpec((1, d), lambda i, j: (0, 0)),
            pl.BlockSpec((None, d, tn), lambda i, j: (layer, 0, j)),
        ],
        out_specs=[
            pl.BlockSpec((tm, tn), lambda i, j: (i, jnp.minimum(j, n_sb - 1))),
            pl.BlockSpec((tm, tn), lambda i, j: (i, jnp.maximum(j - n_sb, 0))),
        ],
        out_shape=[jax.ShapeDtypeStruct((m_rows, U_SB_COLS), BF16),
                   jax.ShapeDtypeStruct((m_rows, n - U_SB_COLS), F32)],
        scratch_shapes=[pltpu.VMEM((tm, d), BF16)],
        compiler_params=_cparams("parallel", "arbitrary"),
        name="in_proj",
    )(x2, mod, g, w)


def _diag_masks():
    r = lax.broadcasted_iota(jnp.int32, (TQ, TK), 0)
    c = lax.broadcasted_iota(jnp.int32, (TQ, TK), 1)
    return c <= r, c < r


def _sb_kernel(q_ref, k_ref, v_ref, o_ref, z_ref, a_ref, *, nq):
    i = pl.program_id(2)
    lane = _lane_iota()
    heads = range(HEADS_PER_STEP)
    pair = lambda hd: slice((hd // 2) * LANES, (hd // 2 + 1) * LANES)
    qs = []
    for pr in range(HEADS_PER_STEP // 2):
        q2 = q_ref[0, :, pr * LANES:(pr + 1) * LANES] * (HEAD_DIM ** -0.5)
        qs.append(jnp.where(lane < HEAD_DIM, q2, jnp.zeros_like(q2)))
        qs.append(jnp.where(lane >= HEAD_DIM, q2, jnp.zeros_like(q2)))
    tri = (lax.broadcasted_iota(jnp.int32, (TK, TK), 0)
           > lax.broadcasted_iota(jnp.int32, (TK, TK), 1)).astype(BF16)
    tri2 = jnp.concatenate([tri, tri], axis=0)
    _, strict = _diag_masks()

    for c in range(nq):
        @pl.when(i == c)
        def _():
            kv_len = (c + 1) * TK
            for hd in heads:
                z_ref[hd, :, :kv_len] = _dot_nt(qs[hd], k_ref[0, :kv_len, pair(hd)])
            carry = [jnp.zeros((TQ, 1), F32) for _ in heads]
            for j in reversed(range(c + 1)):
                sl = slice(j * TK, (j + 1) * TK)
                for hd in heads:
                    z = z_ref[hd, :, sl]
                    sp = jnp.log(1.0 + jnp.exp(-jnp.abs(z)))
                    log_beta = jnp.minimum(z, 0.0) - sp
                    neg_1m = jnp.maximum(z, 0.0) + sp
                    if j == c:
                        neg_1m = jnp.where(strict, neg_1m, 0.0)
                    hi = neg_1m.astype(BF16)
                    pieces = jnp.concatenate([hi, (neg_1m - hi.astype(F32)).astype(BF16)], axis=1)
                    later = _dot(pieces, tri2)
                    a = jnp.exp(log_beta - later - carry[hd])
                    if j == c:
                        a = jnp.where(strict, a, 0.0)
                    a_ref[hd, :, sl] = a.astype(BF16)
                    carry[hd] = carry[hd] + jnp.sum(neg_1m, axis=-1, keepdims=True)
            outs = [_dot(a_ref[hd, :, :kv_len], v_ref[0, :kv_len, pair(hd)]) for hd in heads]
            for pr in range(HEADS_PER_STEP // 2):
                o_ref[0, :, pr * LANES:(pr + 1) * LANES] = jnp.where(lane < HEAD_DIM, outs[2 * pr], outs[2 * pr + 1])


def _sb_attention(usb3):
    b, t, _ = usb3.shape
    nq = t // TQ
    wide = HEADS_PER_STEP * HEAD_DIM
    return pl.pallas_call(
        functools.partial(_sb_kernel, nq=nq),
        grid=(b, GROUP_WIDTH // wide, nq),
        in_specs=[
            pl.BlockSpec((1, TQ, wide), lambda bb, p, i: (bb, i, U_SBQ // wide + p)),
            pl.BlockSpec((1, t, wide), lambda bb, p, i: (bb, 0, U_SBK // wide + p)),
            pl.BlockSpec((1, t, wide), lambda bb, p, i: (bb, 0, U_SBV // wide + p)),
        ],
        out_specs=pl.BlockSpec((1, TQ, wide), lambda bb, p, i: (bb, i, p)),
        out_shape=jax.ShapeDtypeStruct((b, t, GROUP_WIDTH), F32),
        scratch_shapes=[pltpu.VMEM((HEADS_PER_STEP, TQ, t), F32), pltpu.VMEM((HEADS_PER_STEP, TQ, t), BF16)],
        compiler_params=_cparams("parallel", "parallel", "arbitrary"),
        name="sb_attention",
    )(usb3, usb3, usb3)


CONV_HALO = 32


def _conv_kernel(a_ref, g_ref, ah_ref, gh_ref, dww_ref, dwb_ref, lng_ref, lnb_ref, pw_ref, pwb_ref,
                 o_ref, h_ref, s_ref, *, tm):
    i = pl.program_id(1)
    h_ref[CONV_HALO:, :] = a_ref[0] * jax.nn.sigmoid(g_ref[0])
    halo = ah_ref[0] * jax.nn.sigmoid(gh_ref[0])
    h_ref[0:CONV_HALO, :] = jnp.where(i == 0, 0.0, halo)
    acc = jnp.zeros((tm, GROUP_WIDTH), F32) + dwb_ref[...]
    base = CONV_HALO - (CONV_WIDTH - 1)
    sub = 8
    for r in range(sub):
        taps = [k for k in range(CONV_WIDTH) if (base + k) % sub == r]
        if not taps:
            continue
        lo = base + taps[0]
        span = (taps[-1] - taps[0]) + tm
        s_ref[r, 0:span, :] = h_ref[lo:lo + span, :]
        for k in taps:
            acc = acc + dww_ref[k:k + 1, :] * s_ref[r, k - taps[0]:k - taps[0] + tm, :]
    mu = jnp.mean(acc, axis=-1, keepdims=True)
    cen = acc - mu
    var = jnp.mean(cen * cen, axis=-1, keepdims=True)
    hn = cen * lax.rsqrt(var + CONV_LN_EPS) * lng_ref[...] + lnb_ref[...]
    act = hn * jax.nn.sigmoid(hn)
    o_ref[0] = _dot(act.astype(BF16), pw_ref[...]) + pwb_ref[...]


def _conformer(u3, dw_w, dw_b, ln_g, ln_b, pw_w, pw_b):
    b, t, _ = u3.shape
    tm = 512
    cw = GROUP_WIDTH
    hb = tm // CONV_HALO

    def halo_map(col):
        return lambda bb, i: (bb, jnp.maximum(i * hb - 1, 0), col)

    row = lambda bb, i: (0, 0)
    return pl.pallas_call(
        functools.partial(_conv_kernel, tm=tm),
        grid=(b, t // tm),
        in_specs=[
            pl.BlockSpec((1, tm, cw), lambda bb, i: (bb, i, U_CA // cw)),
            pl.BlockSpec((1, tm, cw), lambda bb, i: (bb, i, U_CG // cw)),
            pl.BlockSpec((1, CONV_HALO, cw), halo_map(U_CA // cw)),
            pl.BlockSpec((1, CONV_HALO, cw), halo_map(U_CG // cw)),
            pl.BlockSpec((CONV_WIDTH, cw), row),
            pl.BlockSpec((1, cw), row),
            pl.BlockSpec((1, cw), row),
            pl.BlockSpec((1, cw), row),
            pl.BlockSpec((cw, cw), row),
            pl.BlockSpec((1, cw), row),
        ],
        out_specs=pl.BlockSpec((1, tm, cw), lambda bb, i: (bb, i, 0)),
        out_shape=jax.ShapeDtypeStruct((b, t, cw), F32),
        scratch_shapes=[pltpu.VMEM((tm + CONV_HALO, cw), F32), pltpu.VMEM((8, tm + CONV_HALO, cw), F32)],
        compiler_params=_cparams("parallel", "parallel"),
        name="conformer_conv",
    )(u3, u3, u3, u3, dw_w, dw_b.reshape(1, cw), ln_g.reshape(1, cw), ln_b.reshape(1, cw),
      pw_w.astype(BF16), pw_b.reshape(1, cw))


def _head_norm_rope(x, gain, cos, sin_signed, seg_mean):
    ms = _split_dot(x * x, seg_mean)
    y = x * lax.rsqrt(ms + EPS) * gain
    lane = _lane_iota()
    half = HEAD_DIM // 2
    first = (lane & (HEAD_DIM - 1)) < half
    partner = jnp.where(first, pltpu.roll(y, LANES - half, 1), pltpu.roll(y, half, 1))
    return y * cos + partner * sin_signed


def _dup(x, g):
    lane = _lane_iota()
    sw = pltpu.roll(x, HEAD_DIM, 1)
    if g == 0:
        return jnp.where(lane < HEAD_DIM, x, sw)
    return jnp.where(lane < HEAD_DIM, sw, x)


def _nsa_prep_kernel(q_ref, kv_ref, gk_ref, cos_ref, sin_ref, qg_ref, kg_ref,
                     qn_ref, kc_ref, kse_ref, kso_ref, vs_ref, kw_ref, vw_ref, gt_ref, *, per_seq):
    cos = cos_ref[...]
    sin = sin_ref[...]
    seg = lax.broadcasted_iota(jnp.int32, (LANES, LANES), 0) >> HEAD_SHIFT
    seg_mean = jnp.where(seg == lax.broadcasted_iota(jnp.int32, (LANES, LANES), 1) >> HEAD_SHIFT,
                         1.0 / HEAD_DIM, 0.0).astype(BF16)
    scale = HEAD_DIM ** -0.5
    for p in range(GROUP_WIDTH // LANES):
        x = q_ref[:, p * LANES:(p + 1) * LANES]
        qn_ref[:, p * LANES:(p + 1) * LANES] = (
            _head_norm_rope(x, qg_ref[...], cos, sin, seg_mean) * scale).astype(BF16)

    def blk(n):
        return kv_ref[:, n * LANES:(n + 1) * LANES]

    kc_ref[...] = _head_norm_rope(blk(0), kg_ref[0:1, :], cos, sin, seg_mean)
    ks = _head_norm_rope(blk(2), kg_ref[1:2, :], cos, sin, seg_mean)
    kw = _head_norm_rope(blk(4), kg_ref[2:3, :], cos, sin, seg_mean)
    vs = blk(3)
    vw = blk(5)
    tm = ks.shape[0]
    lane = _lane_iota()
    t = (pl.program_id(0) % per_seq) * tm + lax.broadcasted_iota(jnp.int32, (tm, 1), 0)
    sblk = t >> SEL_SHIFT
    hot_lo = jnp.where(lane == sblk, 1.0, 0.0)
    hot_hi = jnp.where(lane - HEAD_DIM == sblk, 1.0, 0.0)
    ks_sw = pltpu.roll(ks, HEAD_DIM, 1)
    for g in range(2):
        sl = slice(g * LANES, (g + 1) * LANES)
        kse_ref[:, sl] = jnp.where(lane < HEAD_DIM, ks if g == 0 else ks_sw, hot_hi).astype(BF16)
        kso_ref[:, sl] = jnp.where(lane >= HEAD_DIM, ks_sw if g == 0 else ks, hot_lo).astype(BF16)
        vs_ref[:, sl] = _dup(vs, g).astype(BF16)
        kw_ref[:, sl] = _dup(kw, g).astype(BF16)
        vw_ref[:, sl] = _dup(vw, g).astype(BF16)
    gates = jax.nn.sigmoid(gk_ref[...])
    gt_ref[:, 0:LANES] = gates
    gt_ref[:, LANES:2 * LANES] = pltpu.roll(gates, LANES - NSA_GATES // 2, 1)


def _nsa_prep(u2, cos, sin, q_gain, k_gain, seq):
    m_rows = u2.shape[0]
    tm = 512
    per_seq = seq // tm
    rowmap = lambda i: (i, 0)
    tab = lambda i: (i % per_seq, 0)
    const = lambda i: (0, 0)
    outs = [
        jax.ShapeDtypeStruct((m_rows, GROUP_WIDTH), BF16),
        jax.ShapeDtypeStruct((m_rows, LANES), F32),
        jax.ShapeDtypeStruct((m_rows, 2 * LANES), BF16),
        jax.ShapeDtypeStruct((m_rows, 2 * LANES), BF16),
        jax.ShapeDtypeStruct((m_rows, 2 * LANES), BF16),
        jax.ShapeDtypeStruct((m_rows, 2 * LANES), BF16),
        jax.ShapeDtypeStruct((m_rows, 2 * LANES), BF16),
        jax.ShapeDtypeStruct((m_rows, 2 * LANES), F32),
    ]
    return pl.pallas_call(
        functools.partial(_nsa_prep_kernel, per_seq=per_seq),
        grid=(m_rows // tm,),
        in_specs=[
            pl.BlockSpec((tm, GROUP_WIDTH), lambda i: (i, U_NQ // GROUP_WIDTH)),
            pl.BlockSpec((tm, 6 * LANES), lambda i: (i, U_NKV // (6 * LANES))),
            pl.BlockSpec((tm, LANES), lambda i: (i, (U_MLA + MLA_Q_RANK + MLA_KV_RANK) // LANES)),
            pl.BlockSpec((tm, LANES), tab),
            pl.BlockSpec((tm, LANES), tab),
            pl.BlockSpec((1, LANES), const),
            pl.BlockSpec((3, LANES), const),
        ],
        out_specs=[
            pl.BlockSpec((tm, GROUP_WIDTH), rowmap),
            pl.BlockSpec((tm, LANES), rowmap),
            pl.BlockSpec((tm, 2 * LANES), rowmap),
            pl.BlockSpec((tm, 2 * LANES), rowmap),
            pl.BlockSpec((tm, 2 * LANES), rowmap),
            pl.BlockSpec((tm, 2 * LANES), rowmap),
            pl.BlockSpec((tm, 2 * LANES), rowmap),
            pl.BlockSpec((tm, 2 * LANES), rowmap),
        ],
        out_shape=outs,
        compiler_params=_cparams("parallel"),
        name="nsa_prep",
    )(u2, u2, u2, cos, sin, q_gain, k_gain)


def _compress_kernel(xk_ref, xv_ref, pek_ref, pev_ref, wk_ref, wv_ref, kc_ref, vc_ref, *, nrow):
    stride = NSA_CMP_STRIDE

    def comp(x_ref, pe_ref, w_ref):
        first = jnp.zeros((nrow, LANES), F32)
        second = jnp.zeros((nrow, LANES), F32)
        for r in range(stride):
            x = x_ref[0, pl.ds(r, nrow, stride=stride), :]
            first = first + _dot((x + pe_ref[r:r + 1, :]).astype(BF16), w_ref[r])
            second = second + _dot((x + pe_ref[stride + r:stride + r + 1, :]).astype(BF16), w_ref[stride + r])
        return first + pltpu.roll(second, nrow - 1, 0)

    ck = comp(xk_ref, pek_ref, wk_ref)
    cv = comp(xv_ref, pev_ref, wv_ref)
    for g in range(2):
        kc_ref[0, g] = _dup(ck, g).astype(BF16)
        vc_ref[0, g] = _dup(cv, g).astype(BF16)


def _compress(kc3, u3, pe_k, pe_v, wk, wv):
    b, t, _ = kc3.shape
    nrow = t // NSA_CMP_STRIDE
    const2 = lambda bb: (0, 0)
    const3 = lambda bb: (0, 0, 0)
    out = jax.ShapeDtypeStruct((b, 2, nrow, LANES), BF16)
    return pl.pallas_call(
        functools.partial(_compress_kernel, nrow=nrow),
        grid=(b,),
        in_specs=[
            pl.BlockSpec((1, t, LANES), lambda bb: (bb, 0, 0)),
            pl.BlockSpec((1, t, LANES), lambda bb: (bb, 0, U_NKV // LANES + 1)),
            pl.BlockSpec(pe_k.shape, const2),
            pl.BlockSpec(pe_v.shape, const2),
            pl.BlockSpec(wk.shape, const3),
            pl.BlockSpec(wv.shape, const3),
        ],
        out_specs=[pl.BlockSpec((1, 2, nrow, LANES), lambda bb: (bb, 0, 0, 0))] * 2,
        out_shape=[out, out],
        compiler_params=_cparams("parallel"),
        name="nsa_compress",
    )(kc3, u3, pe_k, pe_v, wk, wv)


def _attend_many(jobs, scale=None):
    tiles = [k.shape[0] // TK for (_, k, _, _, _, _) in jobs]
    for (q, k, _, s_ref, _, _), n in zip(jobs, tiles):
        s_ref[:, :n * TK] = _dot_nt(q, k)
    maxima = []
    for (_, _, _, s_ref, _, masks), n in zip(jobs, tiles):
        fold = None
        for j in range(n):
            sl = slice(j * TK, (j + 1) * TK)
            s = s_ref[:, sl]
            if j in masks:
                s = jnp.where(masks[j], s, NEG)
                s_ref[:, sl] = s
            f = jnp.maximum(s[:, :LANES], s[:, LANES:])
            fold = f if fold is None else jnp.maximum(fold, f)
        maxima.append(jnp.max(fold, axis=-1, keepdims=True))
    sums = []
    for (_, _, _, s_ref, p_ref, _), n, m in zip(jobs, tiles, maxima):
        fold = None
        for j in range(n):
            sl = slice(j * TK, (j + 1) * TK)
            p = jnp.exp2((s_ref[:, sl] - m) * (LOG2E if scale is None else scale * LOG2E))
            f = p[:, :LANES] + p[:, LANES:]
            fold = f if fold is None else fold + f
            p_ref[:, sl] = p.astype(BF16)
        sums.append(jnp.sum(fold, axis=-1, keepdims=True))
    return [_dot(p_ref[:, :n * TK], v) / l for (_, _, v, _, p_ref, _), n, l in zip(jobs, tiles, sums)]


def _nsa_kernel(q_ref, kc_ref, vc_ref, kse_ref, kso_ref, vs_ref, kw_ref, vw_ref, gt_ref, ov_ref,
                o_ref, s_ref, p_ref, sw_ref, pw_ref, *, nq, n_cmp, n_sblk, n_sel):
    i = pl.program_id(2)
    lane = _lane_iota()
    row = i * TQ + lax.broadcasted_iota(jnp.int32, (TQ, 1), 0)
    rep = 4
    q2s = [q_ref[0, :, p * LANES:(p + 1) * LANES] for p in range(2)]
    qs = []
    for q2 in q2s:
        qs.append(jnp.where(lane < HEAD_DIM, q2, jnp.zeros_like(q2)))
        qs.append(jnp.where(lane >= HEAD_DIM, q2, jnp.zeros_like(q2)))

    kc = kc_ref[0, 0]
    vc = vc_ref[0, 0]
    cmask = ((lane * NSA_CMP_STRIDE + (NSA_CMP_LEN - 1)) <= row) & (lane < n_cmp)
    psum = jnp.zeros((TQ, LANES), F32)
    s_cmp = [jnp.where(cmask, _dot_nt(qs[r], kc), NEG) for r in range(rep)]
    p_cmp = []
    for r in range(rep):
        mx = jnp.max(s_cmp[r], axis=-1, keepdims=True)
        p = jnp.where(cmask, jnp.exp(s_cmp[r] - mx), 0.0)
        p = p / jnp.maximum(jnp.sum(p, axis=-1, keepdims=True), 1e-30)
        p_cmp.append(p.astype(BF16))
        psum = psum + p
    o_cmp = [_dot(p_cmp[r], vc) for r in range(rep)]
    ov_t = ov_ref[0:n_sblk, :]
    p_hi = psum.astype(BF16)
    p_lo = (psum - p_hi.astype(F32)).astype(BF16)
    imp = _dot_nt(ov_t, p_hi) + _dot_nt(ov_t, p_lo)
    cur = (i * TQ + _lane_iota(TQ)) >> SEL_SHIFT
    blk = lax.broadcasted_iota(jnp.int32, (n_sblk, 1), 0)
    eligible = blk <= cur
    forced = (blk == 0) | (blk == cur) | (blk == cur - 1)
    score = jnp.where(eligible, imp + jnp.where(forced, NSA_FORCE_BONUS, 0.0), NEG)
    rank = jnp.zeros((n_sblk, TQ), F32)
    for sp in range(n_sblk):
        other = score[sp:sp + 1, :]
        beats = (other > score) | ((other == score) & (blk > sp))
        rank = rank + jnp.where(beats, 1.0, 0.0)
    bias_t = jnp.where(eligible & (rank < n_sel), 0.0, NEG)

    bias_lo = jnp.concatenate([bias_t, jnp.zeros((LANES - n_sblk, TQ), F32)], axis=0).T
    bias_hi = pltpu.roll(bias_lo, HEAD_DIM, 1).astype(BF16)
    bias_lo = bias_lo.astype(BF16)
    q_aug = []
    for q2 in q2s:
        q_aug.append(jnp.where(lane < HEAD_DIM, q2, bias_hi))
        q_aug.append(jnp.where(lane >= HEAD_DIM, q2, bias_lo))
    causal, strict = _diag_masks()
    wtiles = NSA_WINDOW // TK
    gt = gt_ref[0]

    for c in range(nq):
        @pl.when(i == c)
        def _():
            kv_len = (c + 1) * TK
            lo = max(c - wtiles, 0) * TK
            wmasks = {c - max(c - wtiles, 0): causal}
            if c >= wtiles:
                wmasks[0] = jnp.logical_not(causal)
            jobs = []
            for r in range(rep):
                ks_ref = kse_ref if r % 2 == 0 else kso_ref
                jobs.append((q_aug[r], ks_ref[0, :kv_len, :], vs_ref[0, :kv_len, :],
                             s_ref.at[r], p_ref.at[r], {c: causal}))
            for r in range(rep):
                jobs.append((qs[r], kw_ref[0, lo:kv_len, :], vw_ref[0, lo:kv_len, :],
                             sw_ref.at[r], pw_ref.at[r], wmasks))
            res = _attend_many(jobs)
            outs = []
            for r in range(rep):
                outs.append(gt[:, 3 * r:3 * r + 1] * o_cmp[r] + gt[:, 3 * r + 1:3 * r + 2] * res[r]
                            + gt[:, 3 * r + 2:3 * r + 3] * res[rep + r])
            o_ref[0, :, 0:LANES] = jnp.where(lane < HEAD_DIM, outs[0], outs[1])
            o_ref[0, :, LANES:2 * LANES] = jnp.where(lane < HEAD_DIM, outs[2], outs[3])


def _nsa_attention(qn, kcmp, vcmp, kse, kso, vsd, kwd, vwd, gates, overlap):
    b, t, _ = qn.shape
    nq = t // TQ
    n_cmp = (t - NSA_CMP_LEN) // NSA_CMP_STRIDE + 1
    n_sblk = t // NSA_SEL_LEN
    nrow = kcmp.shape[2]
    wlen = NSA_WINDOW + TK
    kv = lambda bb, g, i: (bb, 0, g)
    cm = lambda bb, g, i: (bb, g, 0, 0)
    return pl.pallas_call(
        functools.partial(_nsa_kernel, nq=nq, n_cmp=n_cmp, n_sblk=n_sblk, n_sel=min(NSA_N_SEL, n_sblk)),
        grid=(b, 2, nq),
        in_specs=[
            pl.BlockSpec((1, TQ, 2 * LANES), lambda bb, g, i: (bb, i, g)),
            pl.BlockSpec((1, 1, nrow, LANES), cm),
            pl.BlockSpec((1, 1, nrow, LANES), cm),
            pl.BlockSpec((1, t, LANES), kv),
            pl.BlockSpec((1, t, LANES), kv),
            pl.BlockSpec((1, t, LANES), kv),
            pl.BlockSpec((1, t, LANES), kv),
            pl.BlockSpec((1, t, LANES), kv),
            pl.BlockSpec((1, TQ, LANES), lambda bb, g, i: (bb, i, g)),
            pl.BlockSpec((LANES, LANES), lambda bb, g, i: (0, 0)),
        ],
        out_specs=pl.BlockSpec((1, TQ, 2 * LANES), lambda bb, g, i: (bb, i, g)),
        out_shape=jax.ShapeDtypeStruct((b, t, GROUP_WIDTH), F32),
        scratch_shapes=[pltpu.VMEM((4, TQ, t), F32), pltpu.VMEM((4, TQ, t), BF16),
                        pltpu.VMEM((4, TQ, wlen), F32), pltpu.VMEM((4, TQ, wlen), BF16)],
        compiler_params=_cparams("parallel", "parallel", "arbitrary"),
        name="nsa_attention",
    )(qn, kcmp, vcmp, kse, kso, vsd, kwd, vwd, gates, overlap)


def _mla_prep_kernel(u_ref, cos_ref, sin_ref, qlg_ref, kvlg_ref, wq_ref, wk_ref, wv_ref, qg_ref, kg_ref,
                     q_ref, k_ref, v_ref):
    lane = _lane_iota()
    cos = cos_ref[...]
    sin = sin_ref[...]
    half = MLA_ROPE // 2

    def lat_norm(x, g):
        ms = jnp.mean(x * x, axis=-1, keepdims=True)
        return (x * lax.rsqrt(ms + EPS) * g).astype(BF16)

    live = lane < MLA_QK

    def head_norm_rope(x, g):
        ms = jnp.sum(jnp.where(live, x * x, 0.0), axis=-1, keepdims=True) * (1.0 / MLA_QK)
        y = x * lax.rsqrt(ms + EPS) * g
        return y * cos + pltpu.roll(y, LANES - half, 1) * sin

    q = _dot(lat_norm(u_ref[:, 0:MLA_Q_RANK], qlg_ref[...]), wq_ref[...])
    kvn = lat_norm(u_ref[:, MLA_Q_RANK:MLA_Q_RANK + MLA_KV_RANK], kvlg_ref[...])
    kn = _dot(kvn, wk_ref[...])
    v_ref[...] = _dot(kvn, wv_ref[...]).astype(BF16)
    gk = u_ref[:, MLA_Q_RANK + MLA_KV_RANK:MLA_Q_RANK + MLA_KV_RANK + LANES]
    k_rope = jnp.where(lane >= MLA_NOPE, gk, 0.0)
    for h in range(MLA_HEADS):
        sl = slice(h * LANES, (h + 1) * LANES)
        q_ref[:, sl] = head_norm_rope(q[:, sl], qg_ref[...]).astype(BF16)
        k_ref[:, sl] = head_norm_rope(kn[:, sl] + k_rope, kg_ref[...]).astype(BF16)


def _mla_prep(u2, cos, sin, q_lat_gain, kv_lat_gain, wq, wk, wv, q_gain, k_gain, seq):
    m_rows = u2.shape[0]
    tm = 512
    per_seq = seq // tm
    width = MLA_Q_RANK + MLA_KV_RANK + LANES
    rowmap = lambda i: (i, 0)
    tab = lambda i: (i % per_seq, 0)
    const = lambda i: (0, 0)
    return pl.pallas_call(
        _mla_prep_kernel,
        grid=(m_rows // tm,),
        in_specs=[
            pl.BlockSpec((tm, width), lambda i: (i, U_MLA // width)),
            pl.BlockSpec((tm, LANES), tab),
            pl.BlockSpec((tm, LANES), tab),
            pl.BlockSpec((1, MLA_Q_RANK), const),
            pl.BlockSpec((1, MLA_KV_RANK), const),
            pl.BlockSpec(wq.shape, const),
            pl.BlockSpec(wk.shape, const),
            pl.BlockSpec(wv.shape, const),
            pl.BlockSpec((1, LANES), const),
            pl.BlockSpec((1, LANES), const),
        ],
        out_specs=[
            pl.BlockSpec((tm, MLA_HEADS * LANES), rowmap),
            pl.BlockSpec((tm, MLA_HEADS * LANES), rowmap),
            pl.BlockSpec((tm, GROUP_WIDTH), rowmap),
        ],
        out_shape=[
            jax.ShapeDtypeStruct((m_rows, MLA_HEADS * LANES), BF16),
            jax.ShapeDtypeStruct((m_rows, MLA_HEADS * LANES), BF16),
            jax.ShapeDtypeStruct((m_rows, GROUP_WIDTH), BF16),
        ],
        compiler_params=_cparams("parallel"),
        name="mla_prep",
    )(u2, cos, sin, q_lat_gain, kv_lat_gain, wq, wk, wv, q_gain, k_gain)


def _mla_kernel(q_ref, k_ref, v_ref, o_ref, s_ref, p_ref, *, nq):
    i = pl.program_id(2)
    lane = _lane_iota()
    causal, _ = _diag_masks()
    scale = MLA_QK ** -0.5
    for c in range(nq):
        @pl.when(i == c)
        def _():
            kv_len = (c + 1) * TK
            jobs = []
            for hd in range(HEADS_PER_STEP):
                sl = slice(hd * LANES, (hd + 1) * LANES)
                vs = slice((hd // 2) * LANES, (hd // 2 + 1) * LANES)
                jobs.append((q_ref[0, :, sl], k_ref[0, :kv_len, sl], v_ref[0, :kv_len, vs],
                             s_ref.at[hd], p_ref.at[hd], {c: causal}))
            outs = _attend_many(jobs, scale=scale)
            for pr in range(HEADS_PER_STEP // 2):
                o_ref[0, :, pr * LANES:(pr + 1) * LANES] = jnp.where(lane < HEAD_DIM, outs[2 * pr], outs[2 * pr + 1])


def _mla_attention(qm, km, vm):
    b, t, _ = qm.shape
    nq = t // TQ
    return pl.pallas_call(
        functools.partial(_mla_kernel, nq=nq),
        grid=(b, MLA_HEADS // HEADS_PER_STEP, nq),
        in_specs=[
            pl.BlockSpec((1, TQ, HEADS_PER_STEP * LANES), lambda bb, p, i: (bb, i, p)),
            pl.BlockSpec((1, t, HEADS_PER_STEP * LANES), lambda bb, p, i: (bb, 0, p)),
            pl.BlockSpec((1, t, HEADS_PER_STEP * HEAD_DIM), lambda bb, p, i: (bb, 0, p)),
        ],
        out_specs=pl.BlockSpec((1, TQ, HEADS_PER_STEP * HEAD_DIM), lambda bb, p, i: (bb, i, p)),
        out_shape=jax.ShapeDtypeStruct((b, t, GROUP_WIDTH), F32),
        scratch_shapes=[pltpu.VMEM((HEADS_PER_STEP, TQ, t), F32), pltpu.VMEM((HEADS_PER_STEP, TQ, t), BF16)],
        compiler_params=_cparams("parallel", "parallel", "arbitrary"),
        name="mla_attention",
    )(qm, km, vm)


def _out_kernel(a_ref, b_ref, c_ref, d_ref, gn_ref, w_ref, x_ref, mod_ref, o_ref):
    acc = None
    for gi, r in enumerate((a_ref, b_ref, c_ref, d_ref)):
        y = r[...]
        ms = jnp.mean(y * y, axis=-1, keepdims=True)
        sl = slice(gi * GROUP_WIDTH, (gi + 1) * GROUP_WIDTH)
        part = _dot((y * lax.rsqrt(ms + EPS) * gn_ref[:, sl]).astype(BF16), w_ref[sl, :])
        acc = part if acc is None else acc + part
    o_ref[...] = x_ref[...] + mod_ref[0, 2:3, :] * acc


def _out_proj(parts, gn, w, layer, x2, mod, seq):
    m_rows, d = x2.shape
    tm = 512
    per_seq = seq // tm
    part = pl.BlockSpec((tm, GROUP_WIDTH), lambda i: (i, 0))
    return pl.pallas_call(
        _out_kernel,
        grid=(m_rows // tm,),
        in_specs=[
            part, part, part, part,
            pl.BlockSpec((1, 4 * GROUP_WIDTH), lambda i: (0, 0)),
            pl.BlockSpec((None, 4 * GROUP_WIDTH, d), lambda i: (layer, 0, 0)),
            pl.BlockSpec((tm, d), lambda i: (i, 0)),
            pl.BlockSpec((1, 6, d), lambda i: (i // per_seq, 0, 0)),
        ],
        out_specs=pl.BlockSpec((tm, d), lambda i: (i, 0)),
        out_shape=jax.ShapeDtypeStruct((m_rows, d), F32),
        compiler_params=_cparams("parallel"),
        name="out_proj",
    )(*parts, gn, w, x2, mod)


FFN_HALO = 16


def _ffn_kernel(x_ref, xh_ref, mod_ref, g_ref, wa_ref, wg_ref, cwa_ref, cwg_ref, cba_ref, cbg_ref, wd_ref, wdl_ref,
                o_ref, h_ref, act_ref, acc_ref, *, per_seq, nj):
    i = pl.program_id(0)
    j = pl.program_id(1)
    m = mod_ref[0]

    def conv(u, cw_ref, cb_ref):
        y = cw_ref[2:3, :] * u + cw_ref[1:2, :] * pltpu.roll(u, 1, 0) + cw_ref[0:1, :] * pltpu.roll(u, 2, 0)
        return y[FFN_HALO:, :] + cb_ref[...]

    def activation():
        h = h_ref[...]
        ya = conv(_dot(h, wa_ref[...]), cwa_ref, cba_ref)
        yg = conv(_dot(h, wg_ref[...]), cwg_ref, cbg_ref)
        return ((yg * jax.nn.sigmoid(yg)) * ya).astype(BF16)

    @pl.when(j == 0)
    def _():
        h_ref[FFN_HALO:, :] = _modulated_norm(x_ref[...], g_ref[...], m[3:4], m[4:5]).astype(BF16)
        halo = _modulated_norm(xh_ref[...], g_ref[...], m[3:4], m[4:5])
        h_ref[0:FFN_HALO, :] = jnp.where(i % per_seq == 0, 0.0, halo).astype(BF16)
        act_ref[...] = activation()

    @pl.when(j == 1)
    def _():
        new = activation()
        acc_ref[...] = _dot(act_ref[...], wd_ref[...])
        act_ref[...] = new

    @pl.when((j > 1) & (j < nj - 1))
    def _():
        new = activation()
        acc_ref[...] += _dot(act_ref[...], wd_ref[...])
        act_ref[...] = new

    @pl.when(j == nj - 1)
    def _():
        new = activation()
        total = acc_ref[...] + _dot(act_ref[...], wd_ref[...]) + _dot(new, wdl_ref[...])
        o_ref[...] = x_ref[...] + m[5:6] * total


def _ffn(x2, mod, g, w_up, conv_w, conv_b, w_down, layer, seq):
    m_rows, d = x2.shape
    d_ff = w_down.shape[1]
    tm, tn = 512, 512
    per_seq = seq // tm
    nj = d_ff // tn
    hb = tm // FFN_HALO
    cb = conv_b.reshape(1, 2 * d_ff)
    assert nj >= 3
    return pl.pallas_call(
        functools.partial(_ffn_kernel, per_seq=per_seq, nj=nj),
        grid=(m_rows // tm, nj),
        in_specs=[
            pl.BlockSpec((tm, d), lambda i, j: (i, 0)),
            pl.BlockSpec((FFN_HALO, d), lambda i, j: (jnp.maximum(i * hb - 1, 0), 0)),
            pl.BlockSpec((1, 6, d), lambda i, j: (i // per_seq, 0, 0)),
            pl.BlockSpec((1, d), lambda i, j: (0, 0)),
            pl.BlockSpec((None, d, tn), lambda i, j: (layer, 0, j)),
            pl.BlockSpec((None, d, tn), lambda i, j: (layer, 0, nj + j)),
            pl.BlockSpec((3, tn), lambda i, j: (0, j)),
            pl.BlockSpec((3, tn), lambda i, j: (0, nj + j)),
            pl.BlockSpec((1, tn), lambda i, j: (0, j)),
            pl.BlockSpec((1, tn), lambda i, j: (0, nj + j)),
            pl.BlockSpec((None, tn, d), lambda i, j: (layer, jnp.maximum(j - 1, 0), 0)),
            pl.BlockSpec((None, tn, d), lambda i, j: (layer, nj - 1, 0)),
        ],
        out_specs=pl.BlockSpec((tm, d), lambda i, j: (i, 0)),
        out_shape=jax.ShapeDtypeStruct((m_rows, d), F32),
        scratch_shapes=[pltpu.VMEM((tm + FFN_HALO, d), BF16), pltpu.VMEM((tm, tn), BF16),
                        pltpu.VMEM((tm, d), F32)],
        compiler_params=_cparams("parallel", "arbitrary"),
        name="conv_glu_ffn",
    )(x2, x2, mod, g, w_up, w_up, conv_w, conv_w, cb, cb, w_down, w_down)


REORDER_COLS = 768


def _reorder_kernel(wt_ref, o_ref, t_ref, *, n_tail):
    j = pl.program_id(2)

    @pl.when(j < pl.num_programs(2) - 1)
    def _():
        o_ref[0] = wt_ref[0].T.astype(BF16)

    @pl.when(j == pl.num_programs(2) - 1)
    def _():
        n_gate, lat, half = NSA_GATES, MLA_Q_RANK + MLA_KV_RANK, MLA_ROPE // 2
        rope = n_gate + lat
        assert rope + MLA_ROPE == n_tail
        t_ref[...] = jnp.zeros_like(t_ref)
        t_ref[0:lat, :] = wt_ref[0, n_gate:rope, :]
        t_ref[lat:lat + n_gate, :] = wt_ref[0, 0:n_gate, :]
        t_ref[lat + MLA_NOPE:lat + MLA_QK, :] = wt_ref[0, rope:rope + MLA_ROPE, :]
        t_ref[lat + MLA_QK:lat + MLA_QK + half, :] = wt_ref[0, rope:rope + half, :]
        o_ref[0] = t_ref[...].T.astype(BF16)


def _reorder_w_in(w):
    n_layers, d, n_in = w.shape
    tr = 1024
    assert REF_ALIGNED_COLS % REORDER_COLS == 0 and U_COLS - REF_ALIGNED_COLS == REORDER_COLS
    return pl.pallas_call(
        functools.partial(_reorder_kernel, n_tail=n_in - REF_ALIGNED_COLS),
        grid=(n_layers, d // tr, U_COLS // REORDER_COLS),
        in_specs=[pl.BlockSpec((1, REORDER_COLS, tr), lambda l, i, j: (l, j, i))],
        out_specs=pl.BlockSpec((1, tr, REORDER_COLS), lambda l, i, j: (l, i, j)),
        out_shape=jax.ShapeDtypeStruct((n_layers, d, U_COLS), BF16),
        scratch_shapes=[pltpu.VMEM((REORDER_COLS, tr), F32)],
        compiler_params=_cparams("parallel", "parallel", "arbitrary"),
        name="reorder_w_in",
    )(jnp.swapaxes(w, 1, 2))


def _rope_tables(seq, dim, lane_cos, lane_sin):
    inv = ROPE_THETA ** (-jnp.arange(0, dim, 2, dtype=F32) / dim)
    ang = jnp.arange(seq).astype(F32)[:, None] * inv[None, :]
    cos, sin = jnp.cos(ang), jnp.sin(ang)
    return lane_cos(cos), lane_sin(sin)


def _nsa_tables(seq):
    return _rope_tables(seq, HEAD_DIM,
                        lambda c: jnp.concatenate([c, c, c, c], axis=1),
                        lambda s: jnp.concatenate([-s, s, -s, s], axis=1))


def _mla_tables(seq):
    ones = jnp.ones((seq, MLA_NOPE), F32)
    zeros = jnp.zeros((seq, MLA_NOPE), F32)
    pad0 = jnp.zeros((seq, LANES - MLA_QK), F32)
    return _rope_tables(seq, MLA_ROPE,
                        lambda c: jnp.concatenate([ones, c, c, pad0], axis=1),
                        lambda s: jnp.concatenate([zeros, -s, s, pad0], axis=1))


def _mla_head_lanes(v):
    half = MLA_ROPE // 2
    pad = jnp.zeros(v.shape[:-1] + (LANES - MLA_QK - half,), v.dtype)
    return jnp.concatenate([v, v[..., MLA_NOPE:MLA_NOPE + half], pad], axis=-1)


def _overlap_matrix(seq):
    n_cmp = (seq - NSA_CMP_LEN) // NSA_CMP_STRIDE + 1
    n_sblk = seq // NSA_SEL_LEN
    starts = np.arange(n_cmp) * NSA_CMP_STRIDE
    sel_start = np.arange(n_sblk) * NSA_SEL_LEN
    ov = np.clip(np.minimum(starts[:, None] + NSA_CMP_LEN, sel_start[None, :] + NSA_SEL_LEN)
                 - np.maximum(starts[:, None], sel_start[None, :]), 0, None) / NSA_CMP_LEN
    full = np.zeros((LANES, LANES), np.float32)
    full[:n_sblk, :n_cmp] = ov.T
    return jnp.asarray(full, BF16)


def _mixer(x2, mod, batch, seq, p, w_in_all, w_o_all, layer):
    m_rows = x2.shape[0]
    usb, u2 = _in_proj(x2, mod, p["norm_mix"].reshape(1, -1), w_in_all, layer, seq)
    u3 = u2.reshape(batch, seq, U_COLS - U_SB_COLS)

    o_sb = _sb_attention(usb.reshape(batch, seq, U_SB_COLS))
    o_conv = _conformer(u3, p["conv_dw_w"], p["conv_dw_b"], p["conv_ln_g"], p["conv_ln_b"],
                        p["conv_pw_w"], p["conv_pw_b"])

    cos_n, sin_n = _nsa_tables(seq)
    q_gain = jnp.tile(p["nsa_q_norm"], 2).reshape(1, LANES)
    k_gain = jnp.tile(p["nsa_k_norm"], (1, 2))
    qn, kc, kse, kso, vsd, kwd, vwd, gates = _nsa_prep(u2, cos_n, sin_n, q_gain, k_gain, seq)
    pe2 = jnp.tile(p["nsa_cmp_pe"], (1, 1, 2))
    w_l = p["nsa_cmp_w"].reshape(2, NSA_CMP_LEN, HEAD_DIM, HEAD_DIM)
    zero = jnp.zeros_like(w_l)
    w_bd = jnp.concatenate([jnp.concatenate([w_l, zero], axis=-1),
                            jnp.concatenate([zero, w_l], axis=-1)], axis=-2).astype(BF16)
    kcmp, vcmp = _compress(kc.reshape(batch, seq, LANES), u3, pe2[0], pe2[1], w_bd[0], w_bd[1])
    o_nsa = _nsa_attention(qn.reshape(batch, seq, -1), kcmp, vcmp,
                           kse.reshape(batch, seq, -1), kso.reshape(batch, seq, -1), vsd.reshape(batch, seq, -1),
                           kwd.reshape(batch, seq, -1), vwd.reshape(batch, seq, -1),
                           gates.reshape(batch, seq, -1), _overlap_matrix(seq))

    cos_m, sin_m = _mla_tables(seq)
    w_uq = p["mla_w_uq"].reshape(MLA_Q_RANK, MLA_HEADS, MLA_QK)
    wq = _mla_head_lanes(w_uq).reshape(MLA_Q_RANK, MLA_HEADS * LANES).astype(BF16)
    w_ukv = p["mla_w_ukv"].reshape(MLA_KV_RANK, MLA_HEADS, 2 * HEAD_DIM)
    wk_m = jnp.pad(w_ukv[:, :, :MLA_NOPE], ((0, 0), (0, 0), (0, LANES - MLA_NOPE)))
    wk_m = wk_m.reshape(MLA_KV_RANK, MLA_HEADS * LANES).astype(BF16)
    wv_m = w_ukv[:, :, MLA_NOPE:].reshape(MLA_KV_RANK, GROUP_WIDTH).astype(BF16)
    qm, km, vm = _mla_prep(u2, cos_m, sin_m, p["mla_q_lat_norm"].reshape(1, -1),
                           p["mla_kv_lat_norm"].reshape(1, -1), wq, wk_m, wv_m,
                           _mla_head_lanes(p["mla_q_norm"]).reshape(1, LANES),
                           _mla_head_lanes(p["mla_k_norm"]).reshape(1, LANES), seq)
    o_mla = _mla_attention(qm.reshape(batch, seq, -1), km.reshape(batch, seq, -1), vm.reshape(batch, seq, -1))

    parts = [o.reshape(m_rows, GROUP_WIDTH) for o in (o_sb, o_conv, o_nsa, o_mla)]
    return _out_proj(parts, p["group_norm"].reshape(1, -1), w_o_all, layer, x2, mod, seq)


def kernel(x, c, ada_w, ada_b, norm_mix, norm_ffn, w_in, conv_dw_w, conv_dw_b, conv_ln_g, conv_ln_b, conv_pw_w, conv_pw_b, nsa_q_norm, nsa_k_norm, nsa_cmp_pe, nsa_cmp_w, mla_q_lat_norm, mla_kv_lat_norm, mla_w_uq, mla_w_ukv, mla_q_norm, mla_k_norm, group_norm, w_o, ffn_up, ffn_conv_w, ffn_conv_b, ffn_down):
    batch, seq, d = x.shape
    depth = ada_w.shape[0]
    per_layer = dict(
        norm_mix=norm_mix, conv_dw_w=conv_dw_w, conv_dw_b=conv_dw_b, conv_ln_g=conv_ln_g,
        conv_ln_b=conv_ln_b, conv_pw_w=conv_pw_w, conv_pw_b=conv_pw_b, nsa_q_norm=nsa_q_norm,
        nsa_k_norm=nsa_k_norm, nsa_cmp_pe=nsa_cmp_pe, nsa_cmp_w=nsa_cmp_w, mla_q_lat_norm=mla_q_lat_norm,
        mla_kv_lat_norm=mla_kv_lat_norm, mla_w_uq=mla_w_uq, mla_w_ukv=mla_w_ukv, mla_q_norm=mla_q_norm,
        mla_k_norm=mla_k_norm, group_norm=group_norm)
    w_in_all = _reorder_w_in(w_in)
    w_o_all = w_o.astype(BF16)
    ffn_up_all = ffn_up.astype(BF16)
    ffn_down_all = ffn_down.astype(BF16)
    mods = _ada(c, ada_w, ada_b)
    x2 = x.reshape(batch * seq, d)
    for l in range(depth):
        p = {k: v[l] for k, v in per_layer.items()}
        x2 = _mixer(x2, mods[l], batch, seq, p, w_in_all, w_o_all, l)
        x2 = _ffn(x2, mods[l], norm_ffn[l].reshape(1, -1), ffn_up_all, ffn_conv_w[l], ffn_conv_b[l],
                  ffn_down_all, l, seq)
    return x2.reshape(batch, seq, d)
```
